```python
import math
import jax, jax.numpy as jnp
from jax import lax
import numpy as np

D_MODEL = 1024
BATCH = 2
SEQ = 16384
DEPTH = 1

CHUNK = 64
RET_HEADS = 8
RET_QK_DIM = 64
RET_V_DIM = 128
DSA_HEADS = 8
DSA_HEAD_DIM = 64
DSA_KV_LATENT = 128
IDX_HEADS = 8
IDX_DIM = 64
IDX_TOPK = 256
Q_BLOCK = 128
N_BUCKETS = 32
MAX_DISTANCE = 128
N_GROUPS = 4
EXPERTS_PER_GROUP = 4
N_EXPERTS = N_GROUPS * EXPERTS_PER_GROUP
EXPERT_FF = 256
EXPERT_TOPK = 2
MOE_BLOCK = 4096
ROPE_BASE = 10000.0
NORM_EPS = 1e-6
GN_EPS = 1e-5
IN_SPLITS = (RET_HEADS * RET_QK_DIM, RET_HEADS * RET_QK_DIM, RET_HEADS * RET_V_DIM, RET_HEADS * RET_V_DIM,
             DSA_HEADS * DSA_HEAD_DIM, DSA_KV_LATENT, IDX_HEADS * IDX_DIM, IDX_DIM, IDX_HEADS)
D_IN = sum(IN_SPLITS)

kernel_name = "chunk_causal_retention_dsa_hmoe_block"


def _split_cols(z, sizes):
    offs, acc = [], 0
    for s in sizes[:-1]:
        acc += s
        offs.append(acc)
    return jnp.split(z, offs, axis=-1)


def rmsnorm(x, g):
    xf = x.astype(jnp.float32)
    y = xf * lax.rsqrt(jnp.mean(xf * xf, axis=-1, keepdims=True) + NORM_EPS)
    return (y * g.astype(jnp.float32)).astype(x.dtype)


def modulate(xn, shift, scale):
    return xn * (1.0 + scale[:, None, :]) + shift[:, None, :]


def rope(x, pos):
    d = x.shape[-1]
    freqs = ROPE_BASE ** (-jnp.arange(0, d, 2, dtype=jnp.float32) / d)
    ang = pos.astype(jnp.float32)[:, None] * freqs[None, :]
    cos = jnp.cos(ang)[None, :, None, :]
    sin = jnp.sin(ang)[None, :, None, :]
    x1, x2 = jnp.split(x.astype(jnp.float32), 2, axis=-1)
    return jnp.concatenate([x1 * cos - x2 * sin, x2 * cos + x1 * sin], axis=-1).astype(x.dtype)


def rel_bucket(rel):
    nb = N_BUCKETS // 2
    ret = (rel > 0).astype(jnp.int32) * nb
    n = jnp.abs(rel)
    max_exact = nb // 2
    nf = jnp.maximum(n, 1).astype(jnp.float32)
    large = max_exact + (jnp.log(nf / max_exact) / math.log(MAX_DISTANCE / max_exact)
                         * (nb - max_exact)).astype(jnp.int32)
    large = jnp.minimum(large, nb - 1)
    return ret + jnp.where(n < max_exact, n, large)


def retention_branch(q, k, v, g, gn_gain):
    B, S, H, dk = q.shape
    dv = v.shape[-1]
    N = S // CHUNK
    dt = q.dtype
    log_gamma = jnp.log(1.0 - 2.0 ** (-5.0 - jnp.arange(H, dtype=jnp.float32)))
    idx = jnp.arange(CHUNK, dtype=jnp.float32)
    intra_decay = jnp.exp(log_gamma[:, None, None] * jnp.abs(idx[:, None] - idx[None, :])).astype(dt)
    k_decay = jnp.exp(log_gamma[:, None] * (CHUNK - 1 - idx)[None, :]).astype(dt)
    q_decay = jnp.exp(log_gamma[:, None] * (idx + 1)[None, :]).astype(dt)
    chunk_decay = jnp.exp(log_gamma * CHUNK)
    qc = q.reshape(B, N, CHUNK, H, dk)
    kc = k.reshape(B, N, CHUNK, H, dk)
    vc = v.reshape(B, N, CHUNK, H, dv)
    scores = jnp.einsum('bnihd,bnjhd->bhnij', qc, kc) * intra_decay[None, :, None]
    o_intra = jnp.einsum('bhnij,bnjhe->bnihe', scores, vc)
    kv = jnp.einsum('bnjhd,hj,bnjhe->nbhde', kc, k_decay, vc)
    cd = chunk_decay.astype(kv.dtype)[None, :, None, None]

    def step(state, kv_n):
        return state * cd + kv_n, state

    _, prev = lax.scan(step, jnp.zeros((B, H, dk, dv), kv.dtype), kv)
    o_inter = jnp.einsum('bnihd,hi,nbhde->bnihe', qc, q_decay, prev)
    o = (o_intra + o_inter).reshape(B, S, H, dv).astype(jnp.float32)
    mu = jnp.mean(o, axis=-1, keepdims=True)
    var = jnp.mean(jnp.square(o - mu), axis=-1, keepdims=True)
    o = ((o - mu) * lax.rsqrt(var + GN_EPS)).reshape(B, S, H * dv) * gn_gain.astype(jnp.float32)
    return (jax.nn.silu(g.astype(jnp.float32)) * o).astype(dt)


def dsa_branch(q, k, v, qi, kidx, wi, rel_bias):
    B, S, H, dh = q.shape
    topk = min(IDX_TOPK, S // 4)
    nblk = S // Q_BLOCK
    key_chunk = jnp.arange(S, dtype=jnp.int32) // CHUNK
    idx_scale = (IDX_DIM ** -0.5) * (IDX_HEADS ** -0.5)
    kidx_f = kidx.astype(jnp.float32)

    def to_blocks(a):
        return a.reshape((B, nblk, Q_BLOCK) + a.shape[2:]).swapaxes(0, 1)

    def one_block(args):
        qb, qib, wb, start = args
        qpos = start + jnp.arange(Q_BLOCK, dtype=jnp.int32)
        qchunk = qpos // CHUNK
        s = jnp.einsum('bqhd,bkd->bqhk', qib.astype(jnp.float32), kidx_f)
        score = jnp.einsum('bqhk,bqh->bqk', jax.nn.relu(s), wb.astype(jnp.float32)) * idx_scale
        admissible = key_chunk[None, :] <= qchunk[:, None]
        score = jnp.where(admissible[None], score, -jnp.inf)
        _, sel = lax.top_k(score, topk)
        valid = key_chunk[sel] <= qchunk[None, :, None]
        kg = jax.vmap(lambda a, i: a[i])(k, sel)
        vg = jax.vmap(lambda a, i: a[i])(v, sel)
        logits = jnp.einsum('bqhd,bqkd->bqhk', qb, kg).astype(jnp.float32) * (dh ** -0.5)
        bias = rel_bias[rel_bucket(sel - qpos[None, :, None])].astype(jnp.float32)
        logits = logits + jnp.transpose(bias, (0, 1, 3, 2))
        logits = jnp.where(valid[:, :, None, :], logits, -jnp.inf)
        p = jax.nn.softmax(logits, axis=-1).astype(vg.dtype)
        return jnp.einsum('bqhk,bqkd->bqhd', p, vg)

    starts = jnp.arange(nblk, dtype=jnp.int32) * Q_BLOCK
    out = lax.map(one_block, (to_blocks(q), to_blocks(qi), to_blocks(wi), starts))
    return out.swapaxes(0, 1).reshape(B, S, H * dh)


def hier_moe(u, w_gr, b_gr, w_er, b_er, w1, w3, w2):
    B, S, D = u.shape
    T = B * S
    t = u.reshape(T, D)
    gl = (t @ w_gr + b_gr).astype(jnp.float32)
    pg = jax.nn.softmax(gl, axis=-1)
    gsel = jnp.argmax(gl, axis=-1)
    gp = jnp.take_along_axis(pg, gsel[:, None], axis=1)[:, 0]
    el = (t @ w_er + b_er).astype(jnp.float32).reshape(T, N_GROUPS, EXPERTS_PER_GROUP)
    el_sel = jnp.take_along_axis(el, gsel[:, None, None], axis=1)[:, 0]
    vals, ids = lax.top_k(el_sel, EXPERT_TOPK)
    wts = jax.nn.softmax(vals, axis=-1) * gp[:, None]
    eid = gsel[:, None].astype(jnp.int32) * EXPERTS_PER_GROUP + ids
    gates = jnp.einsum('tke,tk->te', jax.nn.one_hot(eid, N_EXPERTS, dtype=jnp.float32), wts).astype(u.dtype)
    blk = math.gcd(T, MOE_BLOCK)
    nb = T // blk

    def expert_block(args):
        xb, gb = args
        a = jnp.einsum('td,edf->tef', xb, w1)
        b = jnp.einsum('td,edf->tef', xb, w3)
        hid = jax.nn.silu(a) * b * gb[:, :, None]
        return jnp.einsum('tef,efd->td', hid, w2)

    y = lax.map(expert_block, (t.reshape(nb, blk, D), gates.reshape(nb, blk, N_EXPERTS)))
    return y.reshape(B, S, D)


def setup_inputs(seed: int = 0) -> dict:
    key = jax.random.key(seed)
    ks = jax.random.split(key, 24)
    f = jnp.float32
    L, D = DEPTH, D_MODEL
    RV = RET_HEADS * RET_V_DIM
    DV = DSA_HEADS * DSA_HEAD_DIM

    def nrm(k, shape, scale):
        return jax.random.normal(k, shape, f) * scale

    def gain(k, shape):
        return 1.0 + 0.05 * jax.random.normal(k, shape, f)

    return {
        "x": nrm(ks[0], (BATCH, SEQ, D), 1.0),
        "c": nrm(ks[1], (BATCH, D), 1.0),
        "w_ada": nrm(ks[2], (L, D, 6 * D), 0.5 * D ** -0.5),
        "b_ada": nrm(ks[3], (L, 6 * D), 0.01),
        "norm_mix_g": gain(ks[4], (L, D)),
        "w_in": nrm(ks[5], (L, D, D_IN), D ** -0.5),
        "ret_gn_g": gain(ks[6], (L, RV)),
        "dsa_kv_norm_g": gain(ks[7], (L, DSA_KV_LATENT)),
        "w_dsa_kv_up": nrm(ks[8], (L, DSA_KV_LATENT, 2 * DSA_HEAD_DIM), DSA_KV_LATENT ** -0.5),
        "rel_bias": nrm(ks[9], (N_BUCKETS, DSA_HEADS), 0.5),
        "w_ret_out": nrm(ks[10], (L, RV, D), RV ** -0.5),
        "w_dsa_out": nrm(ks[11], (L, DV, D), DV ** -0.5),
        "w_gate": nrm(ks[12], (L, D, 2 * D), D ** -0.5),
        "b_gate": nrm(ks[13], (L, 2 * D), 0.01),
        "w_o": nrm(ks[14], (L, D, D), D ** -0.5),
        "norm_ffn_g": gain(ks[15], (L, D)),
        "w_group_router": nrm(ks[16], (L, D, N_GROUPS), D ** -0.5),
        "b_group_router": nrm(ks[17], (L, N_GROUPS), 0.01),
        "w_expert_router": nrm(ks[18], (L, D, N_EXPERTS), D ** -0.5),
        "b_expert_router": nrm(ks[19], (L, N_EXPERTS), 0.01),
        "w_exp_gate": nrm(ks[20], (L, N_EXPERTS, D, EXPERT_FF), D ** -0.5),
        "w_exp_up": nrm(ks[21], (L, N_EXPERTS, D, EXPERT_FF), D ** -0.5),
        "w_exp_down": nrm(ks[22], (L, N_EXPERTS, EXPERT_FF, D), EXPERT_FF ** -0.5),
        "norm_final_g": gain(ks[23], (D,)),
    }


def reference(x, c, w_ada, b_ada, norm_mix_g, w_in, ret_gn_g, dsa_kv_norm_g, w_dsa_kv_up, rel_bias,
              w_ret_out, w_dsa_out, w_gate, b_gate, w_o, norm_ffn_g, w_group_router, b_group_router,
              w_expert_router, b_expert_router, w_exp_gate, w_exp_up, w_exp_down, norm_final_g):
    B, S, D = x.shape
    pos = jnp.arange(S, dtype=jnp.int32)
    h = x
    for l in range(DEPTH):
        mod = jax.nn.silu(c) @ w_ada[l] + b_ada[l]
        sh1, sc1, g1, sh2, sc2, g2 = jnp.split(mod, 6, axis=-1)
        u = modulate(rmsnorm(h, norm_mix_g[l]), sh1, sc1)
        z = u @ w_in[l]
        rq, rk, rv, rg, dq, dkv, iq, ik, iw = _split_cols(z, IN_SPLITS)
        rq = rope(rq.reshape(B, S, RET_HEADS, RET_QK_DIM), pos)
        rk = rope(rk.reshape(B, S, RET_HEADS, RET_QK_DIM), pos) * (RET_QK_DIM ** -0.5)
        y_ret = retention_branch(rq, rk, rv.reshape(B, S, RET_HEADS, RET_V_DIM), rg, ret_gn_g[l]) @ w_ret_out[l]
        lat = rmsnorm(dkv, dsa_kv_norm_g[l]) @ w_dsa_kv_up[l]
        dk, dv = jnp.split(lat, 2, axis=-1)
        y_dsa = dsa_branch(dq.reshape(B, S, DSA_HEADS, DSA_HEAD_DIM), dk, dv,
                           iq.reshape(B, S, IDX_HEADS, IDX_DIM), ik, iw, rel_bias) @ w_dsa_out[l]
        ga, gb = jnp.split(jax.nn.sigmoid(u @ w_gate[l] + b_gate[l]), 2, axis=-1)
        h = h + g1[:, None, :] * ((ga * y_ret + gb * y_dsa) @ w_o[l])
        u2 = modulate(rmsnorm(h, norm_ffn_g[l]), sh2, sc2)
        h = h + g2[:, None, :] * hier_moe(u2, w_group_router[l], b_group_router[l], w_expert_router[l],
                                         b_expert_router[l], w_exp_gate[l], w_exp_up[l], w_exp_down[l])
    return rmsnorm(h, norm_final_g)
```

```python
import functools
import math

import jax
import jax.numpy as jnp
from jax import lax
from jax.experimental import pallas as pl
from jax.experimental.pallas import tpu as pltpu

CHUNK = 64
RET_HEADS = 8
RET_QK_DIM = 64
RET_V_DIM = 128
DSA_HEADS = 8
DSA_HEAD_DIM = 64
DSA_KV_LATENT = 128
IDX_HEADS = 8
IDX_DIM = 64
IDX_TOPK = 256
N_BUCKETS = 32
N_GROUPS = 4
EXPERTS_PER_GROUP = 4
N_EXPERTS = N_GROUPS * EXPERTS_PER_GROUP
EXPERT_FF = 256
ROPE_BASE = 10000.0
NORM_EPS = 1e-6
GN_EPS = 1e-5
IN_SPLITS = (512, 512, 1024, 1024, 512, 128, 512, 64, 8)

LANES = 128
VMEM_LIMIT_BYTES = 56 * 1024 * 1024

PROJ_TOKENS = 256
RET_TOKENS = 256
DSA_TILE = 256
BISECT_ROWS = 64
MOE_TOKENS = 256

LOG2E = 1.4426950408889634
NEG_BIG = -1e30
KEY_NEG_INF = -2139095041
KEY_POS_INF = 2139095040
KEY_MIN = -2147483648
BUCKET_STEPS = (12, 16, 23, 32, 46, 64, 91)

F32 = jnp.float32
BF16 = jnp.bfloat16
I32 = jnp.int32


def _const_spec(shape):
    nd = len(shape)
    return pl.BlockSpec(shape, lambda *_: (0,) * nd, pipeline_mode=pl.Buffered(1))


def _params(sem):
    return pltpu.CompilerParams(dimension_semantics=sem, vmem_limit_bytes=VMEM_LIMIT_BYTES)


def _split3(a):
    hi = a.astype(BF16)
    lo = (a - hi.astype(F32)).astype(BF16)
    return hi, lo


def _dot3(a, w):
    ah, al = _split3(a)
    wh, wl = _split3(w)
    d = functools.partial(jnp.dot, preferred_element_type=F32)
    return d(ah, wh) + (d(ah, wl) + d(al, wh))


def _ada_kernel(c_ref, w_ref, b_ref, o_ref):
    c = c_ref[...]
    o_ref[...] = _dot3(c * jax.nn.sigmoid(c), w_ref[...]) + b_ref[...]


def _ada(c, w, b):
    bsz, d = c.shape
    n = w.shape[1]
    rows = 8
    cp = jnp.zeros((rows, d), F32).at[:bsz].set(c)
    tn = 1536
    out = pl.pallas_call(
        _ada_kernel,
        grid=(n // tn,),
        in_specs=[pl.BlockSpec((rows, d), lambda j: (0, 0)),
                  pl.BlockSpec((d, tn), lambda j: (0, j)),
                  pl.BlockSpec((1, tn), lambda j: (0, j))],
        out_specs=pl.BlockSpec((rows, tn), lambda j: (0, j)),
        out_shape=jax.ShapeDtypeStruct((rows, n), F32),
        compiler_params=_params(("arbitrary",)),
        name="adaln",
    )(cp, w, b.reshape(1, n))
    return out[:bsz]


def _proj_kernel(x_ref, mod_ref, g_ref, cos_ref, sin_ref, wqk_ref, wvg_ref, wdq_ref, wiq_ref,
                 wkv_ref, wikw_ref, kvg_ref, wup_ref, wgate_ref, bgate_ref,
                 rq_ref, rk_ref, rv_ref, rg_ref, dq_ref, iq_ref, vp_ref, dkt_ref, ikt_ref,
                 iwb_ref, ga_ref, gb_ref):
    dot = functools.partial(jnp.dot, preferred_element_type=F32)
    x = x_ref[0]
    tm = x.shape[0]
    xn = x * lax.rsqrt(jnp.mean(x * x, axis=-1, keepdims=True) + NORM_EPS) * g_ref[...]
    u = xn * (1.0 + mod_ref[0, 1:2, :]) + mod_ref[0, 0:1, :]
    ub = u.astype(BF16)

    lane = lax.broadcasted_iota(I32, (tm, LANES), 1)
    first_half = (lane % RET_QK_DIM) < (RET_QK_DIM // 2)
    cos = cos_ref[...]
    sin = sin_ref[...]
    zqk = dot(ub, wqk_ref[...])
    for which, out_ref, scale in ((0, rq_ref, 1.0), (1, rk_ref, RET_QK_DIM ** -0.5)):
        for pair in range(RET_HEADS // 2):
            c0 = which * RET_HEADS * RET_QK_DIM + pair * LANES
            z = zqk[:, c0:c0 + LANES]
            rot = jnp.where(first_half, pltpu.roll(z, LANES - 32, 1), pltpu.roll(z, 32, 1))
            r = (z * cos + rot * sin) * scale
            out_ref[0, 2 * pair] = r[:, :RET_QK_DIM].astype(BF16)
            out_ref[0, 2 * pair + 1] = r[:, RET_QK_DIM:].astype(BF16)

    zvg = dot(ub, wvg_ref[...])
    nv = RET_HEADS * RET_V_DIM
    rv_ref[0] = zvg[:, :nv].astype(BF16)
    g = zvg[:, nv:]
    rg_ref[0] = g * jax.nn.sigmoid(g)

    zdq = dot(ub, wdq_ref[...]) * (DSA_HEAD_DIM ** -0.5 * LOG2E)
    ziq = dot(ub, wiq_ref[...])
    for h in range(DSA_HEADS):
        dq_ref[0, h] = zdq[:, h * DSA_HEAD_DIM:(h + 1) * DSA_HEAD_DIM].astype(BF16)
        iq_ref[0, h] = ziq[:, h * IDX_DIM:(h + 1) * IDX_DIM].astype(BF16)

    zkv = dot(ub, wkv_ref[...])
    kvn = zkv * lax.rsqrt(jnp.mean(zkv * zkv, axis=-1, keepdims=True) + NORM_EPS) * kvg_ref[...]
    lat = dot(kvn.astype(BF16), wup_ref[...])
    dkt_ref[0, 0] = lat.T[:DSA_HEAD_DIM, :].astype(BF16)
    vp_ref[0] = jnp.where(lane < DSA_HEAD_DIM, pltpu.roll(lat, DSA_HEAD_DIM, 1), 1.0).astype(BF16)

    zik = dot(ub, wikw_ref[...])
    ikt_ref[0, 0] = zik.T[:IDX_DIM, :].astype(BF16)
    idx_scale = (IDX_DIM ** -0.5) * (IDX_HEADS ** -0.5)
    for h in range(IDX_HEADS):
        col = zik[:, IDX_DIM + h:IDX_DIM + h + 1] * idx_scale
        iwb_ref[0, :, h * LANES:(h + 1) * LANES] = jnp.broadcast_to(col, (tm, LANES))

    zg = jax.nn.sigmoid(dot(ub, wgate_ref[...]) + bgate_ref[...])
    d = zg.shape[1] // 2
    ga_ref[0] = zg[:, :d]
    gb_ref[0] = zg[:, d:]


def _proj(x, mod, g_mix, w_in, kv_g, w_up, w_gate, b_gate):
    bsz, s, d = x.shape
    tm = PROJ_TOKENS
    nt = s // tm
    offs = [0]
    for n in IN_SPLITS:
        offs.append(offs[-1] + n)
    seg = lambda i: w_in[:, offs[i]:offs[i + 1]]
    wqk = jnp.concatenate([seg(0), seg(1)], axis=1).astype(BF16)
    wvg = jnp.concatenate([seg(2), seg(3)], axis=1).astype(BF16)
    wdq = seg(4).astype(BF16)
    wkv = seg(5).astype(BF16)
    wiq = seg(6).astype(BF16)
    wikw = jnp.concatenate([seg(7), seg(8), jnp.zeros((d, LANES - IDX_DIM - IDX_HEADS), F32)],
                           axis=1).astype(BF16)

    pos = jnp.arange(s, dtype=jnp.int32)
    freqs = ROPE_BASE ** (-jnp.arange(0, RET_QK_DIM, 2, dtype=F32) / RET_QK_DIM)
    ang = pos.astype(F32)[:, None] * freqs[None, :]
    cos_h = jnp.concatenate([jnp.cos(ang), jnp.cos(ang)], axis=1)
    sin_h = jnp.concatenate([-jnp.sin(ang), jnp.sin(ang)], axis=1)
    cos_t = jnp.concatenate([cos_h, cos_h], axis=1)
    sin_t = jnp.concatenate([sin_h, sin_h], axis=1)

    tok = lambda n: pl.BlockSpec((1, tm, n), lambda b, i: (b, i, 0))
    heads = pl.BlockSpec((1, DSA_HEADS, tm, DSA_HEAD_DIM), lambda b, i: (b, 0, i, 0))
    trans = pl.BlockSpec((1, 1, DSA_HEAD_DIM, tm), lambda b, i: (b, i, 0, 0))
    outs = pl.pallas_call(
        _proj_kernel,
        grid=(bsz, nt),
        in_specs=[tok(d),
                  pl.BlockSpec((1, 6, d), lambda b, i: (b, 0, 0)),
                  _const_spec((1, d)),
                  pl.BlockSpec((tm, LANES), lambda b, i: (i, 0)),
                  pl.BlockSpec((tm, LANES), lambda b, i: (i, 0)),
                  _const_spec(wqk.shape), _const_spec(wvg.shape), _const_spec(wdq.shape),
                  _const_spec(wiq.shape), _const_spec(wkv.shape), _const_spec(wikw.shape),
                  _const_spec((1, DSA_KV_LATENT)), _const_spec(w_up.shape),
                  _const_spec(w_gate.shape), _const_spec((1, w_gate.shape[1]))],
        out_specs=[heads, heads, tok(1024), tok(1024), heads, heads, tok(LANES), trans, trans,
                   tok(IDX_HEADS * LANES), tok(d), tok(d)],
        out_shape=[jax.ShapeDtypeStruct((bsz, RET_HEADS, s, RET_QK_DIM), BF16),
                   jax.ShapeDtypeStruct((bsz, RET_HEADS, s, RET_QK_DIM), BF16),
                   jax.ShapeDtypeStruct((bsz, s, RET_HEADS * RET_V_DIM), BF16),
                   jax.ShapeDtypeStruct((bsz, s, RET_HEADS * RET_V_DIM), F32),
                   jax.ShapeDtypeStruct((bsz, DSA_HEADS, s, DSA_HEAD_DIM), BF16),
                   jax.ShapeDtypeStruct((bsz, IDX_HEADS, s, IDX_DIM), BF16),
                   jax.ShapeDtypeStruct((bsz, s, LANES), BF16),
                   jax.ShapeDtypeStruct((bsz, nt, DSA_HEAD_DIM, tm), BF16),
                   jax.ShapeDtypeStruct((bsz, nt, IDX_DIM, tm), BF16),
                   jax.ShapeDtypeStruct((bsz, s, IDX_HEADS * LANES), F32),
                   jax.ShapeDtypeStruct((bsz, s, d), F32),
                   jax.ShapeDtypeStruct((bsz, s, d), F32)],
        compiler_params=_params(("arbitrary", "arbitrary")),
        name="in_proj",
    )(x, mod, g_mix.reshape(1, d), cos_t, sin_t, wqk, wvg, wdq, wiq, wkv, wikw,
      kv_g.reshape(1, -1), w_up.astype(BF16), w_gate.astype(BF16), b_gate.reshape(1, -1))
    return outs


def _ret_kernel(rq_ref, rk_ref, rv_ref, rg_ref, ga_ref, dmat_ref, qdec_ref, kdec_ref, cdec_ref,
                gng_ref, wout_ref, y_ref, state_ref, o_ref):
    dot = functools.partial(jnp.dot, preferred_element_type=F32)

    @pl.when(pl.program_id(1) == 0)
    def _():
        state_ref[...] = jnp.zeros_like(state_ref)

    for h in range(RET_HEADS):
        q = rq_ref[0, h]
        k = rk_ref[0, h]
        cols = slice(h * RET_V_DIM, (h + 1) * RET_V_DIM)
        v = rv_ref[0, :, cols]
        st = state_ref[h]
        s = lax.dot_general(q, k, (((1,), (1,)), ((), ())), preferred_element_type=F32)
        o = dot((s * dmat_ref[h]).astype(BF16), v)
        qd = (q.astype(F32) * qdec_ref[h]).astype(BF16)
        o = o + dot(qd, st.astype(BF16))
        kd = (k.astype(F32) * kdec_ref[h]).astype(BF16)
        kv = lax.dot_general(kd, v, (((0,), (0,)), ((), ())), preferred_element_type=F32)
        state_ref[h] = st * cdec_ref[h] + kv
        mu = jnp.mean(o, axis=-1, keepdims=True)
        oc = o - mu
        var = jnp.mean(oc * oc, axis=-1, keepdims=True)
        on = oc * lax.rsqrt(var + GN_EPS) * gng_ref[:, cols]
        o_ref[:, cols] = (rg_ref[0, :, cols] * on).astype(BF16)
    y_ref[0] = ga_ref[0] * dot(o_ref[...], wout_ref[...])


def _retention(rq, rk, rv, rg, ga, gn_g, w_out):
    bsz, s, nv = rv.shape
    d = w_out.shape[1]
    tr = RET_TOKENS
    log_gamma = jnp.log(1.0 - 2.0 ** (-5.0 - jnp.arange(RET_HEADS, dtype=F32)))
    idx = jnp.arange(tr, dtype=F32)
    ch = jnp.arange(tr, dtype=jnp.int32) // CHUNK
    dist = jnp.abs(idx[:, None] - idx[None, :])
    dmat = jnp.where((ch[None, :] <= ch[:, None])[None],
                     jnp.exp(log_gamma[:, None, None] * dist[None]), 0.0)
    qdec = jnp.broadcast_to(jnp.exp(log_gamma[:, None] * (idx + 1.0)[None, :])[:, :, None],
                            (RET_HEADS, tr, RET_QK_DIM))
    kdec = jnp.broadcast_to(jnp.exp(log_gamma[:, None] * (tr - 1.0 - idx)[None, :])[:, :, None],
                            (RET_HEADS, tr, RET_QK_DIM))
    cdec = jnp.broadcast_to(jnp.exp(log_gamma * tr)[:, None, None],
                            (RET_HEADS, RET_QK_DIM, RET_V_DIM))

    heads = pl.BlockSpec((1, RET_HEADS, tr, RET_QK_DIM), lambda b, i: (b, 0, i, 0))
    tok = lambda n: pl.BlockSpec((1, tr, n), lambda b, i: (b, i, 0))
    return pl.pallas_call(
        _ret_kernel,
        grid=(bsz, s // tr),
        in_specs=[heads, heads, tok(nv), tok(nv), tok(d),
                  _const_spec(dmat.shape), _const_spec(qdec.shape), _const_spec(kdec.shape),
                  _const_spec(cdec.shape), _const_spec((1, nv)), _const_spec(w_out.shape)],
        out_specs=tok(d),
        out_shape=jax.ShapeDtypeStruct((bsz, s, d), F32),
        scratch_shapes=[pltpu.VMEM((RET_HEADS, RET_QK_DIM, RET_V_DIM), F32),
                        pltpu.VMEM((tr, nv), BF16)],
        compiler_params=_params(("arbitrary", "arbitrary")),
        name="retention",
    )(rq, rk, rv, rg, ga, dmat, qdec, kdec, cdec, gn_g.reshape(1, nv), w_out.astype(BF16))


def _dsa_kernel(relb_ref, dq_ref, iq_ref, iwb_ref, gb_ref, ikt_ref, dkt_ref, vp_ref, wout_ref,
                y_ref, keys_ref, lo_ref, hi_ref, clo_ref, bias_ref, m_ref, acc_ref, o_ref,
                *, topk):
    t = DSA_TILE
    dot = functools.partial(jnp.dot, preferred_element_type=F32)
    i = pl.program_id(1)
    n_tiles = i + 1
    row = lax.broadcasted_iota(I32, (t, t), 0)
    col = lax.broadcasted_iota(I32, (t, t), 1)
    diag_adm = (col // CHUNK) <= (row // CHUNK)

    @pl.when((pl.program_id(0) == 0) & (i == 0))
    def _():
        for which in range(2):
            rel = col - row - (t if which == 0 else 0)
            n = jnp.abs(rel)
            large = jnp.full((t, t), 8, I32)
            for step in BUCKET_STEPS:
                large = large + (n >= step).astype(I32)
            bucket = jnp.where(n < 8, n, large) + jnp.where(rel > 0, N_BUCKETS // 2, 0)
            for h in range(DSA_HEADS):
                b = jnp.zeros((t, t), F32)
                for k in range(N_BUCKETS):
                    b = jnp.where(bucket == k, relb_ref[k * DSA_HEADS + h], b)
                bias_ref[h, which] = b * LOG2E

    def score_tile(j, diag):
        acc = jnp.zeros((t, t), F32)
        kt = ikt_ref[0, j]
        for h in range(IDX_HEADS):
            s = dot(iq_ref[0, h], kt)
            w = iwb_ref[0, :, h * LANES:(h + 1) * LANES]
            acc = acc + jnp.maximum(s, 0.0) * jnp.concatenate([w] * (t // LANES), axis=1)
        bits = pltpu.bitcast(acc, I32)
        key = jnp.where(bits < 0, bits ^ 0x7FFFFFFF, bits)
        if diag:
            key = jnp.where(diag_adm, key, KEY_MIN)
        keys_ref[j] = key

    def score_body(j, carry):
        score_tile(j, False)
        return carry

    lax.fori_loop(0, i, score_body, 0)
    score_tile(i, True)

    rows128 = lax.broadcasted_iota(I32, (t, LANES), 0)
    n_adm = ((i * t + rows128) // CHUNK + 1) * CHUNK
    lo_ref[...] = jnp.full((t, LANES), KEY_NEG_INF, I32)
    hi_ref[...] = jnp.where(n_adm <= topk, KEY_NEG_INF + 1, KEY_POS_INF + 1)
    clo_ref[...] = n_adm

    def bisect_cond(carry):
        it, active = carry
        return (it < 40) & (active > 0)

    def bisect_body(carry):
        it, _ = carry
        active = jnp.zeros((BISECT_ROWS, LANES), I32)
        for s0 in range(0, t, BISECT_ROWS):
            rows = slice(s0, s0 + BISECT_ROWS)
            lo = lo_ref[rows, :]
            hi = hi_ref[rows, :]
            mid = (lo >> 1) + (hi >> 1) + (lo & hi & 1)

            def count_body(j, cnt):
                for c0 in range(0, t, LANES):
                    cnt = cnt + (keys_ref[j, rows, c0:c0 + LANES] >= mid).astype(I32)
                return cnt

            cnt = lax.fori_loop(0, n_tiles, count_body, jnp.zeros((BISECT_ROWS, LANES), I32))
            c = jnp.broadcast_to(jnp.sum(cnt, axis=1, keepdims=True), (BISECT_ROWS, LANES))
            live = mid > lo
            up = live & (c >= topk)
            down = live & (c < topk)
            new_lo = jnp.where(up, mid, lo)
            new_hi = jnp.where(down, mid, hi)
            new_hi = jnp.where(up & (c == topk), mid + 1, new_hi)
            lo_ref[rows, :] = new_lo
            hi_ref[rows, :] = new_hi
            clo_ref[rows, :] = jnp.where(up, c, clo_ref[rows, :])
            nmid = (new_lo >> 1) + (new_hi >> 1) + (new_lo & new_hi & 1)
            active = active | (nmid > new_lo).astype(I32)
        return it + 1, jnp.max(active)

    lax.while_loop(bisect_cond, bisect_body, (jnp.int32(0), jnp.int32(1)))

    n_ties = jnp.max(jnp.where(clo_ref[...] > topk, 1, 0))

    @pl.when(n_ties > 0)
    def _():
        lo = jnp.concatenate([lo_ref[...]] * (t // LANES), axis=1)
        tie = jnp.concatenate([clo_ref[...]] * (t // LANES), axis=1) > topk

        def gt_body(j, cnt):
            return cnt + jnp.sum((keys_ref[j] > lo).astype(I32), axis=1, keepdims=True)

        need = topk - lax.fori_loop(0, n_tiles, gt_body, jnp.zeros((t, 1), I32))

        def cut_body(_, carry):
            jlo, jhi = carry
            jm = (jlo + jhi) >> 1

            def eq_body(j, cnt):
                hit = (keys_ref[j] == lo) & ((col + j * t) < jm)
                return cnt + jnp.sum(hit.astype(I32), axis=1, keepdims=True)

            c = lax.fori_loop(0, n_tiles, eq_body, jnp.zeros((t, 1), I32))
            ok = c >= need
            return jnp.where(ok, jlo, jm), jnp.where(ok, jm, jhi)

        n_steps = (keys_ref.shape[0] * t).bit_length() + 1
        _, cut = lax.fori_loop(0, n_steps, cut_body,
                               (jnp.zeros((t, 1), I32), jnp.full((t, 1), 1, I32) * (n_tiles * t)))

        def drop_body(j, carry):
            key = keys_ref[j]
            drop = tie & (key == lo) & ((col + j * t) >= cut)
            keys_ref[j] = jnp.where(drop, KEY_MIN, key)
            return carry

        lax.fori_loop(0, n_tiles, drop_body, 0)

    m_ref[...] = jnp.full(m_ref.shape, NEG_BIG, F32)
    acc_ref[...] = jnp.zeros(acc_ref.shape, F32)
    thr = jnp.concatenate([lo_ref[...]] * (t // LANES), axis=1)

    def attend_tile(j, bias_of_head):
        sel = keys_ref[j] >= thr
        kt = dkt_ref[0, j]
        v = vp_ref[0, pl.ds(pl.multiple_of(j * t, t), t), :]
        for h in range(DSA_HEADS):
            lg = dot(dq_ref[0, h], kt) + bias_of_head(h)
            lg = jnp.where(sel, lg, NEG_BIG)
            m_old = m_ref[h]
            m_new = jnp.maximum(m_old, jnp.broadcast_to(jnp.max(lg, axis=1, keepdims=True),
                                                        (t, LANES)))
            p = jnp.exp2(lg - jnp.concatenate([m_new] * (t // LANES), axis=1))
            acc_ref[h] = acc_ref[h] * jnp.exp2(m_old - m_new) + dot(p.astype(BF16), v)
            m_ref[h] = m_new

    far = (N_BUCKETS // 2 - 1) * DSA_HEADS

    def far_body(j, carry):
        attend_tile(j, lambda h: relb_ref[far + h] * LOG2E)
        return carry

    lax.fori_loop(0, i - 1, far_body, 0)

    @pl.when(i >= 1)
    def _():
        attend_tile(i - 1, lambda h: bias_ref[h, 0])

    attend_tile(i, lambda h: bias_ref[h, 1])

    lane = lax.broadcasted_iota(I32, (t, LANES), 1)
    for pair in range(DSA_HEADS // 2):
        a0 = acc_ref[2 * pair]
        a1 = acc_ref[2 * pair + 1]
        even = a0 / pltpu.roll(a0, DSA_HEAD_DIM, 1)
        odd = pltpu.roll(a1, DSA_HEAD_DIM, 1) / a1
        o_ref[:, pair * LANES:(pair + 1) * LANES] = jnp.where(lane < DSA_HEAD_DIM, even, odd).astype(BF16)
    y_ref[0] = gb_ref[0] * dot(o_ref[...], wout_ref[...])


def _dsa(dq, iq, iwb, gb, ikt, dkt, vp, rel_bias, w_out):
    bsz, _, s, _ = dq.shape
    d = w_out.shape[1]
    t = DSA_TILE
    nq = s // t
    topk = min(IDX_TOPK, s // 4)
    heads = pl.BlockSpec((1, DSA_HEADS, t, DSA_HEAD_DIM), lambda b, i: (b, 0, i, 0))
    tok = lambda n: pl.BlockSpec((1, t, n), lambda b, i: (b, i, 0))
    per_batch = lambda shape: pl.BlockSpec((1,) + shape, lambda b, i: (b,) + (0,) * len(shape),
                                           pipeline_mode=pl.Buffered(1))
    return pl.pallas_call(
        functools.partial(_dsa_kernel, topk=topk),
        grid=(bsz, nq),
        in_specs=[pl.BlockSpec(memory_space=pltpu.SMEM),
                  heads, heads, tok(IDX_HEADS * LANES), tok(d),
                  per_batch((nq, IDX_DIM, t)), per_batch((nq, DSA_HEAD_DIM, t)),
                  per_batch((s, LANES)), _const_spec(w_out.shape)],
        out_specs=tok(d),
        out_shape=jax.ShapeDtypeStruct((bsz, s, d), F32),
        scratch_shapes=[pltpu.VMEM((nq, t, t), I32),
                        pltpu.VMEM((t, LANES), I32),
                        pltpu.VMEM((t, LANES), I32),
                        pltpu.VMEM((t, LANES), I32),
                        pltpu.VMEM((DSA_HEADS, 2, t, t), F32),
                        pltpu.VMEM((DSA_HEADS, t, LANES), F32),
                        pltpu.VMEM((DSA_HEADS, t, LANES), F32),
                        pltpu.VMEM((t, DSA_HEADS * DSA_HEAD_DIM), BF16)],
        compiler_params=_params(("arbitrary", "arbitrary")),
        name="dsa",
    )(rel_bias.reshape(-1), dq, iq, iwb, gb, ikt, dkt, vp, w_out.astype(BF16))


def _moe_kernel(x_ref, yr_ref, yd_ref, mod_ref, wo_ref, gffn_ref, wr_ref, br_ref, w1_ref, w3_ref,
                w2_ref, gfin_ref, out_ref, gate_ref):
    dot = functools.partial(jnp.dot, preferred_element_type=F32)
    tm = x_ref.shape[1]
    mix = dot((yr_ref[0] + yd_ref[0]).astype(BF16), wo_ref[...])
    h1 = x_ref[0] + mod_ref[0, 2:3, :] * mix
    hn = h1 * lax.rsqrt(jnp.mean(h1 * h1, axis=-1, keepdims=True) + NORM_EPS) * gffn_ref[...]
    u2 = hn * (1.0 + mod_ref[0, 4:5, :]) + mod_ref[0, 3:4, :]

    logits = _dot3(u2, wr_ref[...]) + br_ref[...]
    lane = lax.broadcasted_iota(I32, (tm, LANES), 1)
    big = jnp.int32(LANES)
    rmax = lambda a: jnp.max(a, axis=1, keepdims=True)
    rmin = lambda a: jnp.min(a, axis=1, keepdims=True)
    is_g = lane < N_GROUPS
    gl = jnp.where(is_g, logits, -jnp.inf)
    gmax = rmax(gl)
    gsel = rmin(jnp.where(is_g & (gl == gmax), lane, big))
    gp = 1.0 / jnp.sum(jnp.where(is_g, jnp.exp(gl - gmax), 0.0), axis=1, keepdims=True)
    e_lane = lane - N_GROUPS
    in_grp = (e_lane >= 0) & (e_lane < N_EXPERTS) & ((e_lane // EXPERTS_PER_GROUP) == gsel)
    el = jnp.where(in_grp, logits, -jnp.inf)
    v1 = rmax(el)
    i1 = rmin(jnp.where(in_grp & (el == v1), lane, big))
    el2 = jnp.where(lane == i1, -jnp.inf, el)
    v2 = rmax(el2)
    i2 = rmin(jnp.where(in_grp & (lane != i1) & (el2 == v2), lane, big))
    e2 = jnp.exp(v2 - v1)
    den = 1.0 + e2
    w1 = gp * (1.0 / den)
    w2 = gp * (e2 / den)
    for e in range(N_EXPERTS):
        ge = jnp.where(i1 == e + N_GROUPS, w1, 0.0) + jnp.where(i2 == e + N_GROUPS, w2, 0.0)
        gate_ref[e] = jnp.broadcast_to(ge, (tm, LANES))

    u2b = u2.astype(BF16)
    y = jnp.zeros_like(h1)
    for e in range(N_EXPERTS):
        a = dot(u2b, w1_ref[e])
        b = dot(u2b, w3_ref[e])
        g = gate_ref[e]
        hid = a * jax.nn.sigmoid(a) * b * jnp.concatenate([g] * (EXPERT_FF // LANES), axis=1)
        y = y + dot(hid.astype(BF16), w2_ref[e])
    h2 = h1 + mod_ref[0, 5:6, :] * y
    out_ref[0] = h2 * lax.rsqrt(jnp.mean(h2 * h2, axis=-1, keepdims=True) + NORM_EPS) * gfin_ref[...]


def _merge_moe(x, y_ret, y_dsa, mod, w_o, g_ffn, w_gr, b_gr, w_er, b_er, w1, w3, w2, g_fin):
    bsz, s, d = x.shape
    tm = MOE_TOKENS
    pad = LANES - N_GROUPS - N_EXPERTS
    wr = jnp.concatenate([w_gr, w_er, jnp.zeros((d, pad), F32)], axis=1)
    br = jnp.concatenate([b_gr, b_er, jnp.zeros((pad,), F32)]).reshape(1, LANES)
    tok = pl.BlockSpec((1, tm, d), lambda b, i: (b, i, 0))
    return pl.pallas_call(
        _moe_kernel,
        grid=(bsz, s // tm),
        in_specs=[tok, tok, tok,
                  pl.BlockSpec((1, 6, d), lambda b, i: (b, 0, 0)),
                  _const_spec(w_o.shape), _const_spec((1, d)), _const_spec(wr.shape),
                  _const_spec((1, LANES)), _const_spec(w1.shape), _const_spec(w3.shape),
                  _const_spec(w2.shape), _const_spec((1, d))],
        out_specs=tok,
        out_shape=jax.ShapeDtypeStruct((bsz, s, d), F32),
        scratch_shapes=[pltpu.VMEM((N_EXPERTS, tm, LANES), F32)],
        compiler_params=_params(("arbitrary", "arbitrary")),
        name="merge_moe",
    )(x, y_ret, y_dsa, mod, w_o.astype(BF16), g_ffn.reshape(1, d), wr, br, w1.astype(BF16),
      w3.astype(BF16), w2.astype(BF16), g_fin.reshape(1, d))


def kernel(x, c, w_ada, b_ada, norm_mix_g, w_in, ret_gn_g, dsa_kv_norm_g, w_dsa_kv_up, rel_bias,
           w_ret_out, w_dsa_out, w_gate, b_gate, w_o, norm_ffn_g, w_group_router, b_group_router,
           w_expert_router, b_expert_router, w_exp_gate, w_exp_up, w_exp_down, norm_final_g):
    assert w_ada.shape[0] == 1, "single-layer block"
    bsz, s, d = x.shape
    assert s % DSA_TILE == 0 and DSA_TILE == PROJ_TOKENS
    mod = _ada(c, w_ada[0], b_ada[0]).reshape(bsz, 6, d)
    (rq, rk, rv, rg, dq, iq, vp, dkt, ikt, iwb, ga, gb) = _proj(
        x, mod, norm_mix_g[0], w_in[0], dsa_kv_norm_g[0], w_dsa_kv_up[0], w_gate[0], b_gate[0])
    y_ret = _retention(rq, rk, rv, rg, ga, ret_gn_g[0], w_ret_out[0])
    y_dsa = _dsa(dq, iq, iwb, gb, ikt, dkt, vp, rel_bias, w_dsa_out[0])
    return _merge_moe(x, y_ret, y_dsa, mod, w_o[0], norm_ffn_g[0], w_group_router[0],
                      b_group_router[0], w_expert_router[0], b_expert_router[0], w_exp_gate[0],
                      w_exp_up[0], w_exp_down[0], norm_final_g)
```

```python
import functools
import math

import jax
import jax.numpy as jnp
from jax import lax
from jax.experimental import pallas as pl
from jax.experimental.pallas import tpu as pltpu

CHUNK = 64
RET_HEADS = 8
RET_QK_DIM = 64
RET_V_DIM = 128
DSA_HEADS = 8
DSA_HEAD_DIM = 64
DSA_KV_LATENT = 128
IDX_HEADS = 8
IDX_DIM = 64
IDX_TOPK = 256
N_BUCKETS = 32
N_GROUPS = 4
EXPERTS_PER_GROUP = 4
N_EXPERTS = N_GROUPS * EXPERTS_PER_GROUP
EXPERT_FF = 256
ROPE_BASE = 10000.0
NORM_EPS = 1e-6
GN_EPS = 1e-5
IN_SPLITS = (512, 512, 1024, 1024, 512, 128, 512, 64, 8)

LANES = 128
VMEM_LIMIT_BYTES = 56 * 1024 * 1024

PROJ_TOKENS = 256
RET_TOKENS = 256
DSA_TILE = 256
DSA_SLOT = 2 * DSA_TILE
SELECT_ROWS = 64
MOE_TOKENS = 256

INTERP_CLIP = 0.02
BISECT_EVERY = 3
MAX_SELECT_STEPS = 128

LOG2E = 1.4426950408889634
NEG_BIG = -1e30
KEY_MIN = -2147483648
BUCKET_STEPS = (12, 16, 23, 32, 46, 64, 91)

F32 = jnp.float32
BF16 = jnp.bfloat16
I32 = jnp.int32


def _const_spec(shape):
    nd = len(shape)
    return pl.BlockSpec(shape, lambda *_: (0,) * nd, pipeline_mode=pl.Buffered(1))


def _params(sem):
    return pltpu.CompilerParams(dimension_semantics=sem, vmem_limit_bytes=VMEM_LIMIT_BYTES)


def _split3(a):
    hi = a.astype(BF16)
    lo = (a - hi.astype(F32)).astype(BF16)
    return hi, lo


def _dot3(a, w):
    ah, al = _split3(a)
    wh, wl = _split3(w)
    d = functools.partial(jnp.dot, preferred_element_type=F32)
    return d(ah, wh) + (d(ah, wl) + d(al, wh))


def _ada_kernel(c_ref, w_ref, b_ref, o_ref):
    c = c_ref[...]
    o_ref[...] = _dot3(c * jax.nn.sigmoid(c), w_ref[...]) + b_ref[...]


def _ada(c, w, b):
    bsz, d = c.shape
    n = w.shape[1]
    rows = 8
    cp = jnp.zeros((rows, d), F32).at[:bsz].set(c)
    tn = 1536
    out = pl.pallas_call(
        _ada_kernel,
        grid=(n // tn,),
        in_specs=[pl.BlockSpec((rows, d), lambda j: (0, 0)),
                  pl.BlockSpec((d, tn), lambda j: (0, j)),
                  pl.BlockSpec((1, tn), lambda j: (0, j))],
        out_specs=pl.BlockSpec((rows, tn), lambda j: (0, j)),
        out_shape=jax.ShapeDtypeStruct((rows, n), F32),
        compiler_params=_params(("arbitrary",)),
        name="adaln",
    )(cp, w, b.reshape(1, n))
    return out[:bsz]


def _proj_kernel(x_ref, mod_ref, g_ref, cos_ref, sin_ref, wqk_ref, wvg_ref, wdq_ref, wiq_ref,
                 wkv_ref, wikw_ref, kvg_ref, wup_ref, wgate_ref, bgate_ref,
                 rq_ref, rk_ref, rv_ref, rg_ref, dq_ref, iq_ref, vp_ref, dkt_ref, ikt_ref,
                 iwb_ref, ga_ref, gb_ref):
    dot = functools.partial(jnp.dot, preferred_element_type=F32)
    x = x_ref[0]
    tm = x.shape[0]
    xn = x * lax.rsqrt(jnp.mean(x * x, axis=-1, keepdims=True) + NORM_EPS) * g_ref[...]
    u = xn * (1.0 + mod_ref[0, 1:2, :]) + mod_ref[0, 0:1, :]
    ub = u.astype(BF16)

    lane = lax.broadcasted_iota(I32, (tm, LANES), 1)
    first_half = (lane % RET_QK_DIM) < (RET_QK_DIM // 2)
    cos = cos_ref[...]
    sin = sin_ref[...]
    zqk = dot(ub, wqk_ref[...])
    for which, out_ref, scale in ((0, rq_ref, 1.0), (1, rk_ref, RET_QK_DIM ** -0.5)):
        for pair in range(RET_HEADS // 2):
            c0 = which * RET_HEADS * RET_QK_DIM + pair * LANES
            z = zqk[:, c0:c0 + LANES]
            rot = jnp.where(first_half, pltpu.roll(z, LANES - 32, 1), pltpu.roll(z, 32, 1))
            r = (z * cos + rot * sin) * scale
            out_ref[0, 2 * pair] = r[:, :RET_QK_DIM].astype(BF16)
            out_ref[0, 2 * pair + 1] = r[:, RET_QK_DIM:].astype(BF16)

    zvg = dot(ub, wvg_ref[...])
    nv = RET_HEADS * RET_V_DIM
    rv_ref[0] = zvg[:, :nv].astype(BF16)
    g = zvg[:, nv:]
    rg_ref[0] = g * jax.nn.sigmoid(g)

    zdq = dot(ub, wdq_ref[...]) * (DSA_HEAD_DIM ** -0.5 * LOG2E)
    ziq = dot(ub, wiq_ref[...])
    for h in range(DSA_HEADS):
        dq_ref[0, h] = zdq[:, h * DSA_HEAD_DIM:(h + 1) * DSA_HEAD_DIM].astype(BF16)
        iq_ref[0, h] = ziq[:, h * IDX_DIM:(h + 1) * IDX_DIM].astype(BF16)

    zkv = dot(ub, wkv_ref[...])
    kvn = zkv * lax.rsqrt(jnp.mean(zkv * zkv, axis=-1, keepdims=True) + NORM_EPS) * kvg_ref[...]
    lat = dot(kvn.astype(BF16), wup_ref[...])
    dkt_ref[0, 0] = lat.T[:DSA_HEAD_DIM, :].astype(BF16)
    vp_ref[0] = jnp.where(lane < DSA_HEAD_DIM, pltpu.roll(lat, DSA_HEAD_DIM, 1), 1.0).astype(BF16)

    zik = dot(ub, wikw_ref[...])
    ikt_ref[0, 0] = zik.T[:IDX_DIM, :].astype(BF16)
    idx_scale = (IDX_DIM ** -0.5) * (IDX_HEADS ** -0.5)
    for h in range(IDX_HEADS):
        col = zik[:, IDX_DIM + h:IDX_DIM + h + 1] * idx_scale
        iwb_ref[0, :, h * LANES:(h + 1) * LANES] = jnp.broadcast_to(col, (tm, LANES))

    zg = jax.nn.sigmoid(dot(ub, wgate_ref[...]) + bgate_ref[...])
    d = zg.shape[1] // 2
    ga_ref[0] = zg[:, :d]
    gb_ref[0] = zg[:, d:]


def _proj(x, mod, g_mix, w_in, kv_g, w_up, w_gate, b_gate):
    bsz, s, d = x.shape
    tm = PROJ_TOKENS
    nt = s // tm
    offs = [0]
    for n in IN_SPLITS:
        offs.append(offs[-1] + n)
    seg = lambda i: w_in[:, offs[i]:offs[i + 1]]
    wqk = jnp.concatenate([seg(0), seg(1)], axis=1).astype(BF16)
    wvg = jnp.concatenate([seg(2), seg(3)], axis=1).astype(BF16)
    wdq = seg(4).astype(BF16)
    wkv = seg(5).astype(BF16)
    wiq = seg(6).astype(BF16)
    wikw = jnp.concatenate([seg(7), seg(8), jnp.zeros((d, LANES - IDX_DIM - IDX_HEADS), F32)],
                           axis=1).astype(BF16)

    pos = jnp.arange(s, dtype=jnp.int32)
    freqs = ROPE_BASE ** (-jnp.arange(0, RET_QK_DIM, 2, dtype=F32) / RET_QK_DIM)
    ang = pos.astype(F32)[:, None] * freqs[None, :]
    cos_h = jnp.concatenate([jnp.cos(ang), jnp.cos(ang)], axis=1)
    sin_h = jnp.concatenate([-jnp.sin(ang), jnp.sin(ang)], axis=1)
    cos_t = jnp.concatenate([cos_h, cos_h], axis=1)
    sin_t = jnp.concatenate([sin_h, sin_h], axis=1)

    tok = lambda n: pl.BlockSpec((1, tm, n), lambda b, i: (b, i, 0))
    heads = pl.BlockSpec((1, DSA_HEADS, tm, DSA_HEAD_DIM), lambda b, i: (b, 0, i, 0))
    trans = pl.BlockSpec((1, 1, DSA_HEAD_DIM, tm), lambda b, i: (b, i // 2, 0, i % 2))
    outs = pl.pallas_call(
        _proj_kernel,
        grid=(bsz, nt),
        in_specs=[tok(d),
                  pl.BlockSpec((1, 6, d), lambda b, i: (b, 0, 0)),
                  _const_spec((1, d)),
                  pl.BlockSpec((tm, LANES), lambda b, i: (i, 0)),
                  pl.BlockSpec((tm, LANES), lambda b, i: (i, 0)),
                  _const_spec(wqk.shape), _const_spec(wvg.shape), _const_spec(wdq.shape),
                  _const_spec(wiq.shape), _const_spec(wkv.shape), _const_spec(wikw.shape),
                  _const_spec((1, DSA_KV_LATENT)), _const_spec(w_up.shape),
                  _const_spec(w_gate.shape), _const_spec((1, w_gate.shape[1]))],
        out_specs=[heads, heads, tok(1024), tok(1024), heads, heads, tok(LANES), trans, trans,
                   tok(IDX_HEADS * LANES), tok(d), tok(d)],
        out_shape=[jax.ShapeDtypeStruct((bsz, RET_HEADS, s, RET_QK_DIM), BF16),
                   jax.ShapeDtypeStruct((bsz, RET_HEADS, s, RET_QK_DIM), BF16),
                   jax.ShapeDtypeStruct((bsz, s, RET_HEADS * RET_V_DIM), BF16),
                   jax.ShapeDtypeStruct((bsz, s, RET_HEADS * RET_V_DIM), F32),
                   jax.ShapeDtypeStruct((bsz, DSA_HEADS, s, DSA_HEAD_DIM), BF16),
                   jax.ShapeDtypeStruct((bsz, IDX_HEADS, s, IDX_DIM), BF16),
                   jax.ShapeDtypeStruct((bsz, s, LANES), BF16),
                   jax.ShapeDtypeStruct((bsz, nt // 2, DSA_HEAD_DIM, 2 * tm), BF16),
                   jax.ShapeDtypeStruct((bsz, nt // 2, IDX_DIM, 2 * tm), BF16),
                   jax.ShapeDtypeStruct((bsz, s, IDX_HEADS * LANES), F32),
                   jax.ShapeDtypeStruct((bsz, s, d), F32),
                   jax.ShapeDtypeStruct((bsz, s, d), F32)],
        compiler_params=_params(("arbitrary", "arbitrary")),
        name="in_proj",
    )(x, mod, g_mix.reshape(1, d), cos_t, sin_t, wqk, wvg, wdq, wiq, wkv, wikw,
      kv_g.reshape(1, -1), w_up.astype(BF16), w_gate.astype(BF16), b_gate.reshape(1, -1))
    return outs


def _ret_kernel(rq_ref, rk_ref, rv_ref, rg_ref, ga_ref, dmat_ref, qdec_ref, kdec_ref, cdec_ref,
                gng_ref, wout_ref, y_ref, state_ref, o_ref):
    dot = functools.partial(jnp.dot, preferred_element_type=F32)

    @pl.when(pl.program_id(1) == 0)
    def _():
        state_ref[...] = jnp.zeros_like(state_ref)

    for h in range(RET_HEADS):
        q = rq_ref[0, h]
        k = rk_ref[0, h]
        cols = slice(h * RET_V_DIM, (h + 1) * RET_V_DIM)
        v = rv_ref[0, :, cols]
        st = state_ref[h]
        s = lax.dot_general(q, k, (((1,), (1,)), ((), ())), preferred_element_type=F32)
        o = dot((s * dmat_ref[h]).astype(BF16), v)
        qd = (q.astype(F32) * qdec_ref[h]).astype(BF16)
        o = o + dot(qd, st.astype(BF16))
        kd = (k.astype(F32) * kdec_ref[h]).astype(BF16)
        kv = lax.dot_general(kd, v, (((0,), (0,)), ((), ())), preferred_element_type=F32)
        state_ref[h] = st * cdec_ref[h] + kv
        mu = jnp.mean(o, axis=-1, keepdims=True)
        oc = o - mu
        var = jnp.mean(oc * oc, axis=-1, keepdims=True)
        on = oc * lax.rsqrt(var + GN_EPS) * gng_ref[:, cols]
        o_ref[:, cols] = (rg_ref[0, :, cols] * on).astype(BF16)
    y_ref[0] = ga_ref[0] * dot(o_ref[...], wout_ref[...])


def _retention(rq, rk, rv, rg, ga, gn_g, w_out):
    bsz, s, nv = rv.shape
    d = w_out.shape[1]
    tr = RET_TOKENS
    log_gamma = jnp.log(1.0 - 2.0 ** (-5.0 - jnp.arange(RET_HEADS, dtype=F32)))
    idx = jnp.arange(tr, dtype=F32)
    ch = jnp.arange(tr, dtype=jnp.int32) // CHUNK
    dist = jnp.abs(idx[:, None] - idx[None, :])
    dmat = jnp.where((ch[None, :] <= ch[:, None])[None],
                     jnp.exp(log_gamma[:, None, None] * dist[None]), 0.0)
    qdec = jnp.broadcast_to(jnp.exp(log_gamma[:, None] * (idx + 1.0)[None, :])[:, :, None],
                            (RET_HEADS, tr, RET_QK_DIM))
    kdec = jnp.broadcast_to(jnp.exp(log_gamma[:, None] * (tr - 1.0 - idx)[None, :])[:, :, None],
                            (RET_HEADS, tr, RET_QK_DIM))
    cdec = jnp.broadcast_to(jnp.exp(log_gamma * tr)[:, None, None],
                            (RET_HEADS, RET_QK_DIM, RET_V_DIM))

    heads = pl.BlockSpec((1, RET_HEADS, tr, RET_QK_DIM), lambda b, i: (b, 0, i, 0))
    tok = lambda n: pl.BlockSpec((1, tr, n), lambda b, i: (b, i, 0))
    return pl.pallas_call(
        _ret_kernel,
        grid=(bsz, s // tr),
        in_specs=[heads, heads, tok(nv), tok(nv), tok(d),
                  _const_spec(dmat.shape), _const_spec(qdec.shape), _const_spec(kdec.shape),
                  _const_spec(cdec.shape), _const_spec((1, nv)), _const_spec(w_out.shape)],
        out_specs=tok(d),
        out_shape=jax.ShapeDtypeStruct((bsz, s, d), F32),
        scratch_shapes=[pltpu.VMEM((RET_HEADS, RET_QK_DIM, RET_V_DIM), F32),
                        pltpu.VMEM((tr, nv), BF16)],
        compiler_params=_params(("arbitrary", "arbitrary")),
        name="retention",
    )(rq, rk, rv, rg, ga, dmat, qdec, kdec, cdec, gn_g.reshape(1, nv), w_out.astype(BF16))


def _sortable(bits):
    return jnp.where(bits < 0, bits ^ 0x7FFFFFFF, bits)


def _midpoint(lo, hi):
    return (lo >> 1) + (hi >> 1) + (lo & hi & 1)


def _lanes(a, n):
    return jnp.concatenate([a] * (n // LANES), axis=1)


def _dsa_kernel(relb_ref, dq_ref, iq_ref, iwb_ref, gb_ref, ikt_ref, dkt_ref, vp_ref, wout_ref,
                y_ref, keys_ref, lo_ref, clo_ref, mid_ref, cnt_ref, smax_ref, smin_ref,
                bias_ref, m_ref, acc_ref, o_ref, *, topk):
    t = DSA_TILE
    w2 = DSA_SLOT
    dot = functools.partial(jnp.dot, preferred_element_type=F32)
    i = pl.program_id(1)
    odd = (i % 2) == 1
    last = i // 2
    n_slots = last + 1
    row = lax.broadcasted_iota(I32, (t, t), 0)
    col = lax.broadcasted_iota(I32, (t, t), 1)
    diag_adm = (col // CHUNK) <= (row // CHUNK)
    far = (N_BUCKETS // 2 - 1) * DSA_HEADS

    @pl.when((pl.program_id(0) == 0) & (i == 0))
    def _():
        for which in range(2):
            rel = col - row - (t if which == 0 else 0)
            n = jnp.abs(rel)
            large = jnp.full((t, t), 8, I32)
            for step in BUCKET_STEPS:
                large = large + (n >= step).astype(I32)
            bucket = jnp.where(n < 8, n, large) + jnp.where(rel > 0, N_BUCKETS // 2, 0)
            for h in range(DSA_HEADS):
                b = jnp.zeros((t, t), F32)
                for k in range(N_BUCKETS):
                    b = jnp.where(bucket == k, relb_ref[k * DSA_HEADS + h], b)
                bias_ref[h, which] = (b - relb_ref[far + h]) * LOG2E

    smax_ref[...] = jnp.full((t, LANES), -jnp.inf, F32)
    smin_ref[...] = jnp.full((t, LANES), jnp.inf, F32)

    def score_cols(kt, adm):
        n = kt.shape[1]
        s = dot(iq_ref[0].reshape(IDX_HEADS * t, IDX_DIM), kt).reshape(IDX_HEADS, t, n)
        acc = jnp.zeros((t, n), F32)
        for h in range(IDX_HEADS):
            acc = acc + jnp.maximum(s[h], 0.0) * _lanes(iwb_ref[0, :, h * LANES:(h + 1) * LANES], n)
        hi_src = acc if adm is None else jnp.where(adm, acc, -jnp.inf)
        lo_src = acc if adm is None else jnp.where(adm, acc, jnp.inf)
        mx = smax_ref[...]
        mn = smin_ref[...]
        for c0 in range(0, n, LANES):
            mx = jnp.maximum(mx, hi_src[:, c0:c0 + LANES])
            mn = jnp.minimum(mn, lo_src[:, c0:c0 + LANES])
        smax_ref[...] = mx
        smin_ref[...] = mn
        key = _sortable(pltpu.bitcast(acc, I32))
        return key if adm is None else jnp.where(adm, key, KEY_MIN)

    def score_body(slot, carry):
        keys_ref[slot] = score_cols(ikt_ref[0, slot], None)
        return carry

    lax.fori_loop(0, last, score_body, 0)

    @pl.when(odd)
    def _():
        keys_ref[last, :, :t] = score_cols(ikt_ref[0, last, :, :t], None)
        keys_ref[last, :, t:] = score_cols(ikt_ref[0, last, :, t:], diag_adm)

    @pl.when(jnp.logical_not(odd))
    def _():
        keys_ref[last, :, :t] = score_cols(ikt_ref[0, last, :, :t], diag_adm)
        keys_ref[last, :, t:] = jnp.full((t, t), KEY_MIN, I32)

    def spread(a):
        f = pltpu.bitcast(a[0:1, :], F32)
        return pltpu.bitcast(jnp.broadcast_to(f, (LANES, t)).T, I32)

    ones_row = jnp.ones((8, LANES), BF16)
    pos = i * t + lax.broadcasted_iota(I32, (8, t), 1)
    n_adm = (pos // CHUNK + 1) * CHUNK
    rmax = jnp.max(smax_ref[...].T, axis=0, keepdims=True)
    rmin = jnp.min(smin_ref[...].T, axis=0, keepdims=True)
    lo0 = jnp.broadcast_to(_sortable(pltpu.bitcast(rmin, I32)), (8, t))
    hi0 = jnp.broadcast_to(_sortable(pltpu.bitcast(rmax, I32)) + 1, (8, t))
    hi0 = jnp.where(n_adm <= topk, lo0 + 1, hi0)
    log_k = math.log(topk - 0.5)

    def unresolved(lo, hi, clo):
        return (clo != topk) & (_midpoint(lo, hi) > lo)

    def any_unresolved(lo, hi, clo):
        return jnp.max(jnp.where(unresolved(lo, hi, clo), 1.0, 0.0))

    def split_point(step, lo, hi, clo, chi):
        lof = pltpu.bitcast(_sortable(lo), F32)
        hif = pltpu.bitcast(_sortable(hi - 1), F32)
        la = jnp.log(clo.astype(F32))
        lb = jnp.log(jnp.maximum(chi.astype(F32), 0.5))
        frac = jnp.clip((la - log_k) / (la - lb), INTERP_CLIP, 1.0 - INTERP_CLIP)
        mid_i = _sortable(pltpu.bitcast(lof + (hif - lof) * frac, I32))
        mid = jnp.where(step % BISECT_EVERY == BISECT_EVERY - 1, _midpoint(lo, hi), mid_i)
        return jnp.minimum(jnp.maximum(mid, lo + 1), hi - 1)

    def select_cond(carry):
        return (carry[0] < MAX_SELECT_STEPS) & (carry[1] > 0.5)

    def select_body(carry):
        step, _, mid, lo, hi, clo, chi = carry
        for s0 in range(0, t, SELECT_ROWS):
            rows = slice(s0, s0 + SELECT_ROWS)
            midr = mid_ref[rows, :]

            def count_body(slot, cnt):
                for c0 in range(0, w2, LANES):
                    cnt = cnt + (keys_ref[slot, rows, c0:c0 + LANES] >= midr).astype(I32)
                return cnt

            cnt_ref[rows, :] = lax.fori_loop(0, n_slots, count_body,
                                             jnp.zeros((SELECT_ROWS, LANES), I32))

        cnt_b = cnt_ref[...].astype(F32).astype(BF16)
        c = lax.dot_general(ones_row, cnt_b, (((1,), (1,)), ((), ())),
                            preferred_element_type=F32).astype(I32)
        active = unresolved(lo, hi, clo)
        up = active & (c >= topk)
        down = active & (c < topk)
        lo = jnp.where(up, mid, lo)
        hi = jnp.where(down, mid, hi)
        clo = jnp.where(up, c, clo)
        chi = jnp.where(down, c, chi)
        mid = split_point(step + 1, lo, hi, clo, chi)
        mid_ref[...] = spread(mid)
        return step + 1, any_unresolved(lo, hi, clo), mid, lo, hi, clo, chi

    chi0 = jnp.zeros((8, t), I32)
    mid0 = split_point(0, lo0, hi0, n_adm, chi0)
    mid_ref[...] = spread(mid0)
    sel = lax.while_loop(select_cond, select_body,
                         (jnp.int32(0), any_unresolved(lo0, hi0, n_adm), mid0, lo0, hi0, n_adm,
                          chi0))
    lo_ref[...] = spread(sel[3])
    clo_ref[...] = spread(sel[5])

    n_ties = jnp.max(jnp.where(clo_ref[...] > topk, 1, 0))

    @pl.when(n_ties > 0)
    def _():
        col2 = lax.broadcasted_iota(I32, (t, w2), 1)
        lo = _lanes(lo_ref[...], w2)
        tie = _lanes(clo_ref[...], w2) > topk

        def gt_body(slot, cnt):
            return cnt + jnp.sum((keys_ref[slot] > lo).astype(I32), axis=1, keepdims=True)

        need = topk - lax.fori_loop(0, n_slots, gt_body, jnp.zeros((t, 1), I32))

        def cut_body(_, carry):
            jlo, jhi = carry
            jm = (jlo + jhi) >> 1

            def eq_body(slot, cnt):
                hit = (keys_ref[slot] == lo) & ((col2 + slot * w2) < jm)
                return cnt + jnp.sum(hit.astype(I32), axis=1, keepdims=True)

            c = lax.fori_loop(0, n_slots, eq_body, jnp.zeros((t, 1), I32))
            ok = c >= need
            return jnp.where(ok, jlo, jm), jnp.where(ok, jm, jhi)

        n_steps = (keys_ref.shape[0] * w2).bit_length() + 1
        _, cut = lax.fori_loop(0, n_steps, cut_body,
                               (jnp.zeros((t, 1), I32), jnp.full((t, 1), 1, I32) * (n_slots * w2)))

        def drop_body(slot, carry):
            key = keys_ref[slot]
            drop = tie & (key == lo) & ((col2 + slot * w2) >= cut)
            keys_ref[slot] = jnp.where(drop, KEY_MIN, key)
            return carry

        lax.fori_loop(0, n_slots, drop_body, 0)

    m_ref[...] = jnp.full(m_ref.shape, NEG_BIG, F32)
    acc_ref[...] = jnp.zeros(acc_ref.shape, F32)

    def attend(slot, biases):
        n = len(biases) * t
        nh = DSA_HEADS
        thr = _lanes(lo_ref[...], n)
        maskb = jnp.where(keys_ref[slot, :, :n] >= thr, 0.0, NEG_BIG)
        v = vp_ref[0, pl.ds(pl.multiple_of(slot * w2, w2), n), :]
        q = dq_ref[0].reshape(nh * t, DSA_HEAD_DIM)
        lg = dot(q, dkt_ref[0, slot, :, :n]).reshape(nh, t, n) + maskb[None]
        if any(which is not None for which in biases):
            lg = jnp.concatenate(
                [lg[:, :, k * t:(k + 1) * t] if which is None
                 else lg[:, :, k * t:(k + 1) * t] + bias_ref[:, which]
                 for k, which in enumerate(biases)], axis=2)
        m_old = m_ref[...]
        m_new = jnp.maximum(m_old, jnp.broadcast_to(jnp.max(lg, axis=2, keepdims=True),
                                                    (nh, t, LANES)))
        p = jnp.exp2(lg - jnp.concatenate([m_new] * (n // LANES), axis=2))
        pv = dot(p.astype(BF16).reshape(nh * t, n), v).reshape(nh, t, LANES)
        acc_ref[...] = acc_ref[...] * jnp.exp2(m_old - m_new) + pv
        m_ref[...] = m_new

    def far_body(slot, carry):
        attend(slot, (None, None))
        return carry

    lax.fori_loop(0, jnp.where(odd, last, last - 1), far_body, 0)

    @pl.when(odd)
    def _():
        attend(last, (0, 1))

    @pl.when(jnp.logical_not(odd) & (i >= 2))
    def _():
        attend(last - 1, (None, 0))

    @pl.when(jnp.logical_not(odd))
    def _():
        attend(last, (1,))

    lane = lax.broadcasted_iota(I32, (t, LANES), 1)
    for pair in range(DSA_HEADS // 2):
        a0 = acc_ref[2 * pair]
        a1 = acc_ref[2 * pair + 1]
        even = a0 / pltpu.roll(a0, DSA_HEAD_DIM, 1)
        odd_h = pltpu.roll(a1, DSA_HEAD_DIM, 1) / a1
        o_ref[:, pair * LANES:(pair + 1) * LANES] = jnp.where(lane < DSA_HEAD_DIM, even,
                                                              odd_h).astype(BF16)
    y_ref[0] = gb_ref[0] * dot(o_ref[...], wout_ref[...])


def _dsa(dq, iq, iwb, gb, ikt, dkt, vp, rel_bias, w_out):
    bsz, _, s, _ = dq.shape
    d = w_out.shape[1]
    t = DSA_TILE
    nq = s // t
    n_slots = nq // 2
    topk = min(IDX_TOPK, s // 4)
    heads = pl.BlockSpec((1, DSA_HEADS, t, DSA_HEAD_DIM), lambda b, i: (b, 0, i, 0))
    tok = lambda n: pl.BlockSpec((1, t, n), lambda b, i: (b, i, 0))
    per_batch = lambda shape: pl.BlockSpec((1,) + shape, lambda b, i: (b,) + (0,) * len(shape),
                                           pipeline_mode=pl.Buffered(1))
    row_state = lambda dt: pltpu.VMEM((t, LANES), dt)
    return pl.pallas_call(
        functools.partial(_dsa_kernel, topk=topk),
        grid=(bsz, nq),
        in_specs=[pl.BlockSpec(memory_space=pltpu.SMEM),
                  heads, heads, tok(IDX_HEADS * LANES), tok(d),
                  per_batch((n_slots, IDX_DIM, DSA_SLOT)), per_batch((n_slots, DSA_HEAD_DIM, DSA_SLOT)),
                  per_batch((s, LANES)), _const_spec(w_out.shape)],
        out_specs=tok(d),
        out_shape=jax.ShapeDtypeStruct((bsz, s, d), F32),
        scratch_shapes=[pltpu.VMEM((n_slots, t, DSA_SLOT), I32),
                        row_state(I32), row_state(I32),
                        row_state(I32), row_state(I32),
                        row_state(F32), row_state(F32),
                        pltpu.VMEM((DSA_HEADS, 2, t, t), F32),
                        pltpu.VMEM((DSA_HEADS, t, LANES), F32),
                        pltpu.VMEM((DSA_HEADS, t, LANES), F32),
                        pltpu.VMEM((t, DSA_HEADS * DSA_HEAD_DIM), BF16)],
        compiler_params=_params(("arbitrary", "arbitrary")),
        name="dsa",
    )(rel_bias.reshape(-1), dq, iq, iwb, gb, ikt, dkt, vp, w_out.astype(BF16))


def _moe_kernel(x_ref, yr_ref, yd_ref, mod_ref, wo_ref, gffn_ref, wr_ref, br_ref, w1_ref, w3_ref,
                w2_ref, gfin_ref, out_ref, gate_ref):
    dot = functools.partial(jnp.dot, preferred_element_type=F32)
    tm = x_ref.shape[1]
    mix = dot((yr_ref[0] + yd_ref[0]).astype(BF16), wo_ref[...])
    h1 = x_ref[0] + mod_ref[0, 2:3, :] * mix
    hn = h1 * lax.rsqrt(jnp.mean(h1 * h1, axis=-1, keepdims=True) + NORM_EPS) * gffn_ref[...]
    u2 = hn * (1.0 + mod_ref[0, 4:5, :]) + mod_ref[0, 3:4, :]

    logits = _dot3(u2, wr_ref[...]) + br_ref[...]
    lane = lax.broadcasted_iota(I32, (tm, LANES), 1)
    big = jnp.int32(LANES)
    rmax = lambda a: jnp.max(a, axis=1, keepdims=True)
    rmin = lambda a: jnp.min(a, axis=1, keepdims=True)
    is_g = lane < N_GROUPS
    gl = jnp.where(is_g, logits, -jnp.inf)
    gmax = rmax(gl)
    gsel = rmin(jnp.where(is_g & (gl == gmax), lane, big))
    gp = 1.0 / jnp.sum(jnp.where(is_g, jnp.exp(gl - gmax), 0.0), axis=1, keepdims=True)
    e_lane = lane - N_GROUPS
    in_grp = (e_lane >= 0) & (e_lane < N_EXPERTS) & ((e_lane // EXPERTS_PER_GROUP) == gsel)
    el = jnp.where(in_grp, logits, -jnp.inf)
    v1 = rmax(el)
    i1 = rmin(jnp.where(in_grp & (el == v1), lane, big))
    el2 = jnp.where(lane == i1, -jnp.inf, el)
    v2 = rmax(el2)
    i2 = rmin(jnp.where(in_grp & (lane != i1) & (el2 == v2), lane, big))
    e2 = jnp.exp(v2 - v1)
    den = 1.0 + e2
    w1 = gp * (1.0 / den)
    w2 = gp * (e2 / den)
    for e in range(N_EXPERTS):
        ge = jnp.where(i1 == e + N_GROUPS, w1, 0.0) + jnp.where(i2 == e + N_GROUPS, w2, 0.0)
        gate_ref[e] = jnp.broadcast_to(ge, (tm, LANES))

    u2b = u2.astype(BF16)
    y = jnp.zeros_like(h1)
    for e in range(N_EXPERTS):
        a = dot(u2b, w1_ref[e])
        b = dot(u2b, w3_ref[e])
        g = gate_ref[e]
        hid = a * jax.nn.sigmoid(a) * b * jnp.concatenate([g] * (EXPERT_FF // LANES), axis=1)
        y = y + dot(hid.astype(BF16), w2_ref[e])
    h2 = h1 + mod_ref[0, 5:6, :] * y
    out_ref[0] = h2 * lax.rsqrt(jnp.mean(h2 * h2, axis=-1, keepdims=True) + NORM_EPS) * gfin_ref[...]


def _merge_moe(x, y_ret, y_dsa, mod, w_o, g_ffn, w_gr, b_gr, w_er, b_er, w1, w3, w2, g_fin):
    bsz, s, d = x.shape
    tm = MOE_TOKENS
    pad = LANES - N_GROUPS - N_EXPERTS
    wr = jnp.concatenate([w_gr, w_er, jnp.zeros((d, pad), F32)], axis=1)
    br = jnp.concatenate([b_gr, b_er, jnp.zeros((pad,), F32)]).reshape(1, LANES)
    tok = pl.BlockSpec((1, tm, d), lambda b, i: (b, i, 0))
    return pl.pallas_call(
        _moe_kernel,
        grid=(bsz, s // tm),
        in_specs=[tok, tok, tok,
                  pl.BlockSpec((1, 6, d), lambda b, i: (b, 0, 0)),
                  _const_spec(w_o.shape), _const_spec((1, d)), _const_spec(wr.shape),
                  _const_spec((1, LANES)), _const_spec(w1.shape), _const_spec(w3.shape),
                  _const_spec(w2.shape), _const_spec((1, d))],
        out_specs=tok,
        out_shape=jax.ShapeDtypeStruct((bsz, s, d), F32),
        scratch_shapes=[pltpu.VMEM((N_EXPERTS, tm, LANES), F32)],
        compiler_params=_params(("arbitrary", "arbitrary")),
        name="merge_moe",
    )(x, y_ret, y_dsa, mod, w_o.astype(BF16), g_ffn.reshape(1, d), wr, br, w1.astype(BF16),
      w3.astype(BF16), w2.astype(BF16), g_fin.reshape(1, d))


def kernel(x, c, w_ada, b_ada, norm_mix_g, w_in, ret_gn_g, dsa_kv_norm_g, w_dsa_kv_up, rel_bias,
           w_ret_out, w_dsa_out, w_gate, b_gate, w_o, norm_ffn_g, w_group_router, b_group_router,
           w_expert_router, b_expert_router, w_exp_gate, w_exp_up, w_exp_down, norm_final_g):
    assert w_ada.shape[0] == 1, "single-layer block"
    bsz, s, d = x.shape
    assert s % DSA_SLOT == 0 and DSA_TILE == PROJ_TOKENS
    mod = _ada(c, w_ada[0], b_ada[0]).reshape(bsz, 6, d)
    (rq, rk, rv, rg, dq, iq, vp, dkt, ikt, iwb, ga, gb) = _proj(
        x, mod, norm_mix_g[0], w_in[0], dsa_kv_norm_g[0], w_dsa_kv_up[0], w_gate[0], b_gate[0])
    y_ret = _retention(rq, rk, rv, rg, ga, ret_gn_g[0], w_ret_out[0])
    y_dsa = _dsa(dq, iq, iwb, gb, ikt, dkt, vp, rel_bias, w_dsa_out[0])
    return _merge_moe(x, y_ret, y_dsa, mod, w_o[0], norm_ffn_g[0], w_group_router[0],
                      b_group_router[0], w_expert_router[0], b_expert_router[0], w_exp_gate[0],
                      w_exp_up[0], w_exp_down[0], norm_final_g)
```

```python
import functools

import jax
import jax.numpy as jnp
from jax import lax
from jax.experimental import pallas as pl
from jax.experimental.pallas import tpu as pltpu

CHUNK = 64
RET_HEADS = 8
RET_QK_DIM = 64
RET_V_DIM = 128
DSA_HEADS = 8
DSA_HEAD_DIM = 64
DSA_KV_LATENT = 128
IDX_HEADS = 8
IDX_DIM = 64
IDX_TOPK = 256
N_BUCKETS = 32
N_GROUPS = 4
EXPERTS_PER_GROUP = 4
N_EXPERTS = N_GROUPS * EXPERTS_PER_GROUP
EXPERT_FF = 256
ROPE_BASE = 10000.0
NORM_EPS = 1e-6
GN_EPS = 1e-5
IN_SPLITS = (512, 512, 1024, 1024, 512, 128, 512, 64, 8)

LANES = 128
VMEM_LIMIT_BYTES = 56 * 1024 * 1024

PROJ_TOKENS = 256
RET_TOKENS = 256
DSA_TILE = 256
DSA_SLOT = 2 * DSA_TILE
SELECT_ROWS = 128
MOE_TOKENS = 256

HALF_BITS = 16

LOG2E = 1.4426950408889634
NEG_BIG = -1e30
KEY_MIN = -2147483648
HALF_MIN = -32768
BUCKET_STEPS = (12, 16, 23, 32, 46, 64, 91)

F32 = jnp.float32
BF16 = jnp.bfloat16
I32 = jnp.int32
I16 = jnp.int16


def _const_spec(shape):
    nd = len(shape)
    return pl.BlockSpec(shape, lambda *_: (0,) * nd, pipeline_mode=pl.Buffered(1))


def _params(sem):
    return pltpu.CompilerParams(dimension_semantics=sem, vmem_limit_bytes=VMEM_LIMIT_BYTES)


def _split3(a):
    hi = a.astype(BF16)
    lo = (a - hi.astype(F32)).astype(BF16)
    return hi, lo


def _dot3(a, w):
    ah, al = _split3(a)
    wh, wl = _split3(w)
    d = functools.partial(jnp.dot, preferred_element_type=F32)
    return d(ah, wh) + (d(ah, wl) + d(al, wh))


def _ada_kernel(c_ref, w_ref, b_ref, o_ref):
    c = c_ref[...]
    o_ref[...] = _dot3(c * jax.nn.sigmoid(c), w_ref[...]) + b_ref[...]


def _ada(c, w, b):
    bsz, d = c.shape
    n = w.shape[1]
    rows = 8
    cp = jnp.zeros((rows, d), F32).at[:bsz].set(c)
    tn = 1536
    out = pl.pallas_call(
        _ada_kernel,
        grid=(n // tn,),
        in_specs=[pl.BlockSpec((rows, d), lambda j: (0, 0)),
                  pl.BlockSpec((d, tn), lambda j: (0, j)),
                  pl.BlockSpec((1, tn), lambda j: (0, j))],
        out_specs=pl.BlockSpec((rows, tn), lambda j: (0, j)),
        out_shape=jax.ShapeDtypeStruct((rows, n), F32),
        compiler_params=_params(("arbitrary",)),
        name="adaln",
    )(cp, w, b.reshape(1, n))
    return out[:bsz]


def _proj_kernel(x_ref, mod_ref, g_ref, cos_ref, sin_ref, wqk_ref, wvg_ref, wdq_ref, wiq_ref,
                 wkv_ref, wikw_ref, kvg_ref, wup_ref, wgate_ref, bgate_ref,
                 rq_ref, rk_ref, rv_ref, rg_ref, dq_ref, iq_ref, vp_ref, dkt_ref, ikt_ref,
                 iwb_ref, ga_ref, gb_ref):
    dot = functools.partial(jnp.dot, preferred_element_type=F32)
    x = x_ref[0]
    tm = x.shape[0]
    xn = x * lax.rsqrt(jnp.mean(x * x, axis=-1, keepdims=True) + NORM_EPS) * g_ref[...]
    u = xn * (1.0 + mod_ref[0, 1:2, :]) + mod_ref[0, 0:1, :]
    ub = u.astype(BF16)

    lane = lax.broadcasted_iota(I32, (tm, LANES), 1)
    first_half = (lane % RET_QK_DIM) < (RET_QK_DIM // 2)
    cos = cos_ref[...]
    sin = sin_ref[...]
    zqk = dot(ub, wqk_ref[...])
    for which, out_ref, scale in ((0, rq_ref, 1.0), (1, rk_ref, RET_QK_DIM ** -0.5)):
        for pair in range(RET_HEADS // 2):
            c0 = which * RET_HEADS * RET_QK_DIM + pair * LANES
            z = zqk[:, c0:c0 + LANES]
            rot = jnp.where(first_half, pltpu.roll(z, LANES - 32, 1), pltpu.roll(z, 32, 1))
            r = (z * cos + rot * sin) * scale
            out_ref[0, 2 * pair] = r[:, :RET_QK_DIM].astype(BF16)
            out_ref[0, 2 * pair + 1] = r[:, RET_QK_DIM:].astype(BF16)

    zvg = dot(ub, wvg_ref[...])
    nv = RET_HEADS * RET_V_DIM
    rv_ref[0] = zvg[:, :nv].astype(BF16)
    g = zvg[:, nv:]
    rg_ref[0] = g * jax.nn.sigmoid(g)

    zdq = dot(ub, wdq_ref[...]) * (DSA_HEAD_DIM ** -0.5 * LOG2E)
    ziq = dot(ub, wiq_ref[...])
    for h in range(DSA_HEADS):
        dq_ref[0, h] = zdq[:, h * DSA_HEAD_DIM:(h + 1) * DSA_HEAD_DIM].astype(BF16)
        iq_ref[0, h] = ziq[:, h * IDX_DIM:(h + 1) * IDX_DIM].astype(BF16)

    zkv = dot(ub, wkv_ref[...])
    kvn = zkv * lax.rsqrt(jnp.mean(zkv * zkv, axis=-1, keepdims=True) + NORM_EPS) * kvg_ref[...]
    lat = dot(kvn.astype(BF16), wup_ref[...])
    dkt_ref[0, 0] = lat.T[:DSA_HEAD_DIM, :].astype(BF16)
    vp_ref[0] = jnp.where(lane < DSA_HEAD_DIM, pltpu.roll(lat, DSA_HEAD_DIM, 1), 1.0).astype(BF16)

    zik = dot(ub, wikw_ref[...])
    ikt_ref[0, 0] = zik.T[:IDX_DIM, :].astype(BF16)
    idx_scale = (IDX_DIM ** -0.5) * (IDX_HEADS ** -0.5)
    for h in range(IDX_HEADS):
        col = zik[:, IDX_DIM + h:IDX_DIM + h + 1] * idx_scale
        iwb_ref[0, :, h * LANES:(h + 1) * LANES] = jnp.broadcast_to(col, (tm, LANES))

    zg = jax.nn.sigmoid(dot(ub, wgate_ref[...]) + bgate_ref[...])
    d = zg.shape[1] // 2
    ga_ref[0] = zg[:, :d]
    gb_ref[0] = zg[:, d:]


def _proj(x, mod, g_mix, w_in, kv_g, w_up, w_gate, b_gate):
    bsz, s, d = x.shape
    tm = PROJ_TOKENS
    nt = s // tm
    offs = [0]
    for n in IN_SPLITS:
        offs.append(offs[-1] + n)
    seg = lambda i: w_in[:, offs[i]:offs[i + 1]]
    wqk = jnp.concatenate([seg(0), seg(1)], axis=1).astype(BF16)
    wvg = jnp.concatenate([seg(2), seg(3)], axis=1).astype(BF16)
    wdq = seg(4).astype(BF16)
    wkv = seg(5).astype(BF16)
    wiq = seg(6).astype(BF16)
    wikw = jnp.concatenate([seg(7), seg(8), jnp.zeros((d, LANES - IDX_DIM - IDX_HEADS), F32)],
                           axis=1).astype(BF16)

    pos = jnp.arange(s, dtype=jnp.int32)
    freqs = ROPE_BASE ** (-jnp.arange(0, RET_QK_DIM, 2, dtype=F32) / RET_QK_DIM)
    ang = pos.astype(F32)[:, None] * freqs[None, :]
    cos_h = jnp.concatenate([jnp.cos(ang), jnp.cos(ang)], axis=1)
    sin_h = jnp.concatenate([-jnp.sin(ang), jnp.sin(ang)], axis=1)
    cos_t = jnp.concatenate([cos_h, cos_h], axis=1)
    sin_t = jnp.concatenate([sin_h, sin_h], axis=1)

    tok = lambda n: pl.BlockSpec((1, tm, n), lambda b, i: (b, i, 0))
    heads = pl.BlockSpec((1, DSA_HEADS, tm, DSA_HEAD_DIM), lambda b, i: (b, 0, i, 0))
    trans = pl.BlockSpec((1, 1, DSA_HEAD_DIM, tm), lambda b, i: (b, i // 2, 0, i % 2))
    outs = pl.pallas_call(
        _proj_kernel,
        grid=(bsz, nt),
        in_specs=[tok(d),
                  pl.BlockSpec((1, 6, d), lambda b, i: (b, 0, 0)),
                  _const_spec((1, d)),
                  pl.BlockSpec((tm, LANES), lambda b, i: (i, 0)),
                  pl.BlockSpec((tm, LANES), lambda b, i: (i, 0)),
                  _const_spec(wqk.shape), _const_spec(wvg.shape), _const_spec(wdq.shape),
                  _const_spec(wiq.shape), _const_spec(wkv.shape), _const_spec(wikw.shape),
                  _const_spec((1, DSA_KV_LATENT)), _const_spec(w_up.shape),
                  _const_spec(w_gate.shape), _const_spec((1, w_gate.shape[1]))],
        out_specs=[heads, heads, tok(1024), tok(1024), heads, heads, tok(LANES), trans, trans,
                   tok(IDX_HEADS * LANES), tok(d), tok(d)],
        out_shape=[jax.ShapeDtypeStruct((bsz, RET_HEADS, s, RET_QK_DIM), BF16),
                   jax.ShapeDtypeStruct((bsz, RET_HEADS, s, RET_QK_DIM), BF16),
                   jax.ShapeDtypeStruct((bsz, s, RET_HEADS * RET_V_DIM), BF16),
                   jax.ShapeDtypeStruct((bsz, s, RET_HEADS * RET_V_DIM), F32),
                   jax.ShapeDtypeStruct((bsz, DSA_HEADS, s, DSA_HEAD_DIM), BF16),
                   jax.ShapeDtypeStruct((bsz, IDX_HEADS, s, IDX_DIM), BF16),
                   jax.ShapeDtypeStruct((bsz, s, LANES), BF16),
                   jax.ShapeDtypeStruct((bsz, nt // 2, DSA_HEAD_DIM, 2 * tm), BF16),
                   jax.ShapeDtypeStruct((bsz, nt // 2, IDX_DIM, 2 * tm), BF16),
                   jax.ShapeDtypeStruct((bsz, s, IDX_HEADS * LANES), F32),
                   jax.ShapeDtypeStruct((bsz, s, d), F32),
                   jax.ShapeDtypeStruct((bsz, s, d), F32)],
        compiler_params=_params(("arbitrary", "arbitrary")),
        name="in_proj",
    )(x, mod, g_mix.reshape(1, d), cos_t, sin_t, wqk, wvg, wdq, wiq, wkv, wikw,
      kv_g.reshape(1, -1), w_up.astype(BF16), w_gate.astype(BF16), b_gate.reshape(1, -1))
    return outs


def _ret_kernel(rq_ref, rk_ref, rv_ref, rg_ref, ga_ref, dmat_ref, qdec_ref, kdec_ref, cdec_ref,
                gng_ref, wout_ref, y_ref, state_ref, o_ref):
    dot = functools.partial(jnp.dot, preferred_element_type=F32)

    @pl.when(pl.program_id(1) == 0)
    def _():
        state_ref[...] = jnp.zeros_like(state_ref)

    for h in range(RET_HEADS):
        q = rq_ref[0, h]
        k = rk_ref[0, h]
        cols = slice(h * RET_V_DIM, (h + 1) * RET_V_DIM)
        v = rv_ref[0, :, cols]
        st = state_ref[h]
        s = lax.dot_general(q, k, (((1,), (1,)), ((), ())), preferred_element_type=F32)
        o = dot((s * dmat_ref[h]).astype(BF16), v)
        qd = (q.astype(F32) * qdec_ref[h]).astype(BF16)
        o = o + dot(qd, st.astype(BF16))
        kd = (k.astype(F32) * kdec_ref[h]).astype(BF16)
        kv = lax.dot_general(kd, v, (((0,), (0,)), ((), ())), preferred_element_type=F32)
        state_ref[h] = st * cdec_ref[h] + kv
        mu = jnp.mean(o, axis=-1, keepdims=True)
        oc = o - mu
        var = jnp.mean(oc * oc, axis=-1, keepdims=True)
        on = oc * lax.rsqrt(var + GN_EPS) * gng_ref[:, cols]
        o_ref[:, cols] = (rg_ref[0, :, cols] * on).astype(BF16)
    y_ref[0] = ga_ref[0] * dot(o_ref[...], wout_ref[...])


def _retention(rq, rk, rv, rg, ga, gn_g, w_out):
    bsz, s, nv = rv.shape
    d = w_out.shape[1]
    tr = RET_TOKENS
    log_gamma = jnp.log(1.0 - 2.0 ** (-5.0 - jnp.arange(RET_HEADS, dtype=F32)))
    idx = jnp.arange(tr, dtype=F32)
    ch = jnp.arange(tr, dtype=jnp.int32) // CHUNK
    dist = jnp.abs(idx[:, None] - idx[None, :])
    dmat = jnp.where((ch[None, :] <= ch[:, None])[None],
                     jnp.exp(log_gamma[:, None, None] * dist[None]), 0.0)
    qdec = jnp.broadcast_to(jnp.exp(log_gamma[:, None] * (idx + 1.0)[None, :])[:, :, None],
                            (RET_HEADS, tr, RET_QK_DIM))
    kdec = jnp.broadcast_to(jnp.exp(log_gamma[:, None] * (tr - 1.0 - idx)[None, :])[:, :, None],
                            (RET_HEADS, tr, RET_QK_DIM))
    cdec = jnp.broadcast_to(jnp.exp(log_gamma * tr)[:, None, None],
                            (RET_HEADS, RET_QK_DIM, RET_V_DIM))

    heads = pl.BlockSpec((1, RET_HEADS, tr, RET_QK_DIM), lambda b, i: (b, 0, i, 0))
    tok = lambda n: pl.BlockSpec((1, tr, n), lambda b, i: (b, i, 0))
    return pl.pallas_call(
        _ret_kernel,
        grid=(bsz, s // tr),
        in_specs=[heads, heads, tok(nv), tok(nv), tok(d),
                  _const_spec(dmat.shape), _const_spec(qdec.shape), _const_spec(kdec.shape),
                  _const_spec(cdec.shape), _const_spec((1, nv)), _const_spec(w_out.shape)],
        out_specs=tok(d),
        out_shape=jax.ShapeDtypeStruct((bsz, s, d), F32),
        scratch_shapes=[pltpu.VMEM((RET_HEADS, RET_QK_DIM, RET_V_DIM), F32),
                        pltpu.VMEM((tr, nv), BF16)],
        compiler_params=_params(("arbitrary", "arbitrary")),
        name="retention",
    )(rq, rk, rv, rg, ga, dmat, qdec, kdec, cdec, gn_g.reshape(1, nv), w_out.astype(BF16))


def _sortable(bits):
    return jnp.where(bits < 0, bits ^ 0x7FFFFFFF, bits)


def _lanes(a, n):
    return jnp.concatenate([a] * (n // LANES), axis=1)


def _flag16(cond):
    return jnp.where(cond, jnp.int16(1), jnp.int16(0))


def _dsa_kernel(relb_ref, dq_ref, iq_ref, iwb_ref, gb_ref, ikt_ref, dkt_ref, vp_ref, wout_ref,
                y_ref, khi_ref, klo_ref, kin_ref, thi_ref, tlo_ref, mid_ref, cnt_ref, tcnt_ref,
                blo_ref, bhi_ref, bchi_ref, base_ref, smax_ref, smin_ref, bias_ref, m_ref, acc_ref,
                o_ref, *, topk):
    t = DSA_TILE
    w2 = DSA_SLOT
    dot = functools.partial(jnp.dot, preferred_element_type=F32)
    i = pl.program_id(1)
    odd = (i % 2) == 1
    last = i // 2
    n_slots = last + 1
    row = lax.broadcasted_iota(I32, (t, t), 0)
    col = lax.broadcasted_iota(I32, (t, t), 1)
    diag_adm = (col // CHUNK) <= (row // CHUNK)
    far = (N_BUCKETS // 2 - 1) * DSA_HEADS

    @pl.when((pl.program_id(0) == 0) & (i == 0))
    def _():
        for which in range(2):
            rel = col - row - (t if which == 0 else 0)
            n = jnp.abs(rel)
            large = jnp.full((t, t), 8, I32)
            for step in BUCKET_STEPS:
                large = large + (n >= step).astype(I32)
            bucket = jnp.where(n < 8, n, large) + jnp.where(rel > 0, N_BUCKETS // 2, 0)
            for h in range(DSA_HEADS):
                b = jnp.zeros((t, t), F32)
                for k in range(N_BUCKETS):
                    b = jnp.where(bucket == k, relb_ref[k * DSA_HEADS + h], b)
                bias_ref[h, which] = (b - relb_ref[far + h]) * LOG2E

    smax_ref[...] = jnp.full((t, LANES), -jnp.inf, F32)
    smin_ref[...] = jnp.full((t, LANES), jnp.inf, F32)

    def score_cols(kt, adm):
        n = kt.shape[1]
        s = dot(iq_ref[0].reshape(IDX_HEADS * t, IDX_DIM), kt).reshape(IDX_HEADS, t, n)
        acc = jnp.zeros((t, n), F32)
        for h in range(IDX_HEADS):
            acc = acc + jnp.maximum(s[h], 0.0) * _lanes(iwb_ref[0, :, h * LANES:(h + 1) * LANES], n)
        hi_src = acc if adm is None else jnp.where(adm, acc, -jnp.inf)
        lo_src = acc if adm is None else jnp.where(adm, acc, jnp.inf)
        mx = smax_ref[...]
        mn = smin_ref[...]
        for c0 in range(0, n, LANES):
            mx = jnp.maximum(mx, hi_src[:, c0:c0 + LANES])
            mn = jnp.minimum(mn, lo_src[:, c0:c0 + LANES])
        smax_ref[...] = mx
        smin_ref[...] = mn
        key = _sortable(pltpu.bitcast(acc, I32))
        if adm is not None:
            key = jnp.where(adm, key, KEY_MIN)
        return (key >> 16).astype(I16), (((key ^ 0x8000) << 16) >> 16).astype(I16)

    def score_body(slot, carry):
        khi_ref[slot], klo_ref[slot] = score_cols(ikt_ref[0, slot], None)
        return carry

    lax.fori_loop(0, last, score_body, 0)

    @pl.when(odd)
    def _():
        khi_ref[last, :, :t], klo_ref[last, :, :t] = score_cols(ikt_ref[0, last, :, :t], None)
        khi_ref[last, :, t:], klo_ref[last, :, t:] = score_cols(ikt_ref[0, last, :, t:], diag_adm)

    @pl.when(jnp.logical_not(odd))
    def _():
        khi_ref[last, :, :t], klo_ref[last, :, :t] = score_cols(ikt_ref[0, last, :, :t], diag_adm)
        khi_ref[last, :, t:] = jnp.full((t, t), HALF_MIN, I16)
        klo_ref[last, :, t:] = jnp.full((t, t), HALF_MIN, I16)

    def bisect(src_ref):
        def halve(_, carry):
            lo = blo_ref[...]
            hi = bhi_ref[...]
            mid = (lo + hi) >> 1
            mid_ref[...] = mid.astype(I16)
            for s0 in range(0, t, SELECT_ROWS):
                rows = slice(s0, s0 + SELECT_ROWS)
                midr = mid_ref[rows, :]

                def count_body(slot, cnt):
                    for c0 in range(0, w2, LANES):
                        cnt = cnt + _flag16(src_ref[slot, rows, c0:c0 + LANES] >= midr)
                    return cnt

                cnt_ref[rows, :] = lax.fori_loop(0, n_slots, count_body,
                                                 jnp.zeros((SELECT_ROWS, LANES), I16))

            lane_sum = jnp.sum(cnt_ref[...].astype(F32), axis=1, keepdims=True)
            c = base_ref[...] + jnp.broadcast_to(lane_sum, (t, LANES)).astype(I32)
            active = hi - lo > 1
            up = active & (c >= topk)
            down = active & (c < topk)
            blo_ref[...] = jnp.where(up, mid, lo)
            bhi_ref[...] = jnp.where(down, mid, hi)
            tcnt_ref[...] = jnp.where(up, c, tcnt_ref[...])
            bchi_ref[...] = jnp.where(down, c, bchi_ref[...])
            return carry

        lax.fori_loop(0, HALF_BITS, halve, 0)

    rows_i = lax.broadcasted_iota(I32, (t, LANES), 0)
    n_adm = ((i * t + rows_i) // CHUNK + 1) * CHUNK
    keep_all = n_adm <= topk
    rmax = jnp.broadcast_to(jnp.max(smax_ref[...], axis=1, keepdims=True), (t, LANES))
    rmin = jnp.broadcast_to(jnp.min(smin_ref[...], axis=1, keepdims=True), (t, LANES))
    lo0 = _sortable(pltpu.bitcast(rmin, I32)) >> 16
    blo_ref[...] = lo0
    bhi_ref[...] = jnp.where(keep_all, lo0 + 1, (_sortable(pltpu.bitcast(rmax, I32)) >> 16) + 1)
    tcnt_ref[...] = n_adm
    bchi_ref[...] = jnp.zeros((t, LANES), I32)
    base_ref[...] = jnp.zeros((t, LANES), I32)
    bisect(khi_ref)
    thi_ref[...] = blo_ref[...].astype(I16)

    def bucket_body(slot, carry):
        kin_ref[slot] = jnp.where(khi_ref[slot] == _lanes(thi_ref[...], w2), klo_ref[slot],
                                  jnp.int16(HALF_MIN))
        return carry

    lax.fori_loop(0, n_slots, bucket_body, 0)

    base_ref[...] = bchi_ref[...]
    blo_ref[...] = jnp.full((t, LANES), HALF_MIN, I32)
    bhi_ref[...] = jnp.where(keep_all, HALF_MIN + 1, -HALF_MIN)
    bisect(kin_ref)
    tlo_ref[...] = blo_ref[...].astype(I16)

    def key32(slot):
        return (khi_ref[slot].astype(I32) << 16) | ((klo_ref[slot].astype(I32) - HALF_MIN) & 0xFFFF)

    n_ties = jnp.max(jnp.where(tcnt_ref[...] > topk, 1, 0))

    @pl.when(n_ties > 0)
    def _():
        col2 = lax.broadcasted_iota(I32, (t, w2), 1)
        thr = (thi_ref[...].astype(I32) << 16) | ((tlo_ref[...].astype(I32) - HALF_MIN) & 0xFFFF)
        lo = _lanes(thr, w2)
        tie = _lanes(tcnt_ref[...], w2) > topk

        def gt_body(slot, cnt):
            return cnt + jnp.sum((key32(slot) > lo).astype(I32), axis=1, keepdims=True)

        need = topk - lax.fori_loop(0, n_slots, gt_body, jnp.zeros((t, 1), I32))

        def cut_body(_, carry):
            jlo, jhi = carry
            jm = (jlo + jhi) >> 1

            def eq_body(slot, cnt):
                hit = (key32(slot) == lo) & ((col2 + slot * w2) < jm)
                return cnt + jnp.sum(hit.astype(I32), axis=1, keepdims=True)

            c = lax.fori_loop(0, n_slots, eq_body, jnp.zeros((t, 1), I32))
            ok = c >= need
            return jnp.where(ok, jlo, jm), jnp.where(ok, jm, jhi)

        n_steps = (khi_ref.shape[0] * w2).bit_length() + 1
        _, cut = lax.fori_loop(0, n_steps, cut_body,
                               (jnp.zeros((t, 1), I32), jnp.full((t, 1), 1, I32) * (n_slots * w2)))

        def drop_body(slot, carry):
            drop = tie & (key32(slot) == lo) & ((col2 + slot * w2) >= cut)
            drop16 = jnp.where(drop, 1, 0).astype(I16) == 1
            khi_ref[slot] = jnp.where(drop16, jnp.int16(HALF_MIN), khi_ref[slot])
            klo_ref[slot] = jnp.where(drop16, jnp.int16(HALF_MIN), klo_ref[slot])
            return carry

        lax.fori_loop(0, n_slots, drop_body, 0)

    m_ref[...] = jnp.full(m_ref.shape, NEG_BIG, F32)
    acc_ref[...] = jnp.zeros(acc_ref.shape, F32)

    def attend(slot, biases):
        n = len(biases) * t
        nh = DSA_HEADS
        kh = khi_ref[slot, :, :n]
        th = _lanes(thi_ref[...], n)
        picked = _flag16(kh > th) + jnp.where(
            kh == th, _flag16(klo_ref[slot, :, :n] >= _lanes(tlo_ref[...], n)), jnp.int16(0))
        maskb = (picked.astype(I32).astype(F32) - 1.0) * (-NEG_BIG)
        v = vp_ref[0, pl.ds(pl.multiple_of(slot * w2, w2), n), :]
        q = dq_ref[0].reshape(nh * t, DSA_HEAD_DIM)
        lg = dot(q, dkt_ref[0, slot, :, :n]).reshape(nh, t, n) + maskb[None]
        if any(which is not None for which in biases):
            lg = jnp.concatenate(
                [lg[:, :, k * t:(k + 1) * t] if which is None
                 else lg[:, :, k * t:(k + 1) * t] + bias_ref[:, which]
                 for k, which in enumerate(biases)], axis=2)
        m_old = m_ref[...]
        m_new = jnp.maximum(m_old, jnp.broadcast_to(jnp.max(lg, axis=2, keepdims=True),
                                                    (nh, t, LANES)))
        p = jnp.exp2(lg - jnp.concatenate([m_new] * (n // LANES), axis=2))
        pv = dot(p.astype(BF16).reshape(nh * t, n), v).reshape(nh, t, LANES)
        acc_ref[...] = acc_ref[...] * jnp.exp2(m_old - m_new) + pv
        m_ref[...] = m_new

    def far_body(slot, carry):
        attend(slot, (None, None))
        return carry

    lax.fori_loop(0, jnp.where(odd, last, last - 1), far_body, 0)

    @pl.when(odd)
    def _():
        attend(last, (0, 1))

    @pl.when(jnp.logical_not(odd) & (i >= 2))
    def _():
        attend(last - 1, (None, 0))

    @pl.when(jnp.logical_not(odd))
    def _():
        attend(last, (1,))

    lane = lax.broadcasted_iota(I32, (t, LANES), 1)
    for pair in range(DSA_HEADS // 2):
        a0 = acc_ref[2 * pair]
        a1 = acc_ref[2 * pair + 1]
        even = a0 / pltpu.roll(a0, DSA_HEAD_DIM, 1)
        odd_h = pltpu.roll(a1, DSA_HEAD_DIM, 1) / a1
        o_ref[:, pair * LANES:(pair + 1) * LANES] = jnp.where(lane < DSA_HEAD_DIM, even,
                                                              odd_h).astype(BF16)
    y_ref[0] = gb_ref[0] * dot(o_ref[...], wout_ref[...])


def _dsa(dq, iq, iwb, gb, ikt, dkt, vp, rel_bias, w_out):
    bsz, _, s, _ = dq.shape
    d = w_out.shape[1]
    t = DSA_TILE
    nq = s // t
    n_slots = nq // 2
    topk = min(IDX_TOPK, s // 4)
    heads = pl.BlockSpec((1, DSA_HEADS, t, DSA_HEAD_DIM), lambda b, i: (b, 0, i, 0))
    tok = lambda n: pl.BlockSpec((1, t, n), lambda b, i: (b, i, 0))
    per_batch = lambda shape: pl.BlockSpec((1,) + shape, lambda b, i: (b,) + (0,) * len(shape),
                                           pipeline_mode=pl.Buffered(1))
    assert s // LANES <= 256, "per-lane key counts must stay exact in bf16"
    row_state = lambda dt: pltpu.VMEM((t, LANES), dt)
    halves = pltpu.VMEM((n_slots, t, DSA_SLOT), I16)
    return pl.pallas_call(
        functools.partial(_dsa_kernel, topk=topk),
        grid=(bsz, nq),
        in_specs=[pl.BlockSpec(memory_space=pltpu.SMEM),
                  heads, heads, tok(IDX_HEADS * LANES), tok(d),
                  per_batch((n_slots, IDX_DIM, DSA_SLOT)), per_batch((n_slots, DSA_HEAD_DIM, DSA_SLOT)),
                  per_batch((s, LANES)), _const_spec(w_out.shape)],
        out_specs=tok(d),
        out_shape=jax.ShapeDtypeStruct((bsz, s, d), F32),
        scratch_shapes=[halves, halves, halves,
                        row_state(I16), row_state(I16),
                        row_state(I16), row_state(I16),
                        row_state(I32), row_state(I32), row_state(I32),
                        row_state(I32), row_state(I32),
                        row_state(F32), row_state(F32),
                        pltpu.VMEM((DSA_HEADS, 2, t, t), F32),
                        pltpu.VMEM((DSA_HEADS, t, LANES), F32),
                        pltpu.VMEM((DSA_HEADS, t, LANES), F32),
                        pltpu.VMEM((t, DSA_HEADS * DSA_HEAD_DIM), BF16)],
        compiler_params=_params(("arbitrary", "arbitrary")),
        name="dsa",
    )(rel_bias.reshape(-1), dq, iq, iwb, gb, ikt, dkt, vp, w_out.astype(BF16))


def _moe_kernel(x_ref, yr_ref, yd_ref, mod_ref, wo_ref, gffn_ref, wr_ref, br_ref, w1_ref, w3_ref,
                w2_ref, gfin_ref, out_ref, gate_ref):
    dot = functools.partial(jnp.dot, preferred_element_type=F32)
    tm = x_ref.shape[1]
    mix = dot((yr_ref[0] + yd_ref[0]).astype(BF16), wo_ref[...])
    h1 = x_ref[0] + mod_ref[0, 2:3, :] * mix
    hn = h1 * lax.rsqrt(jnp.mean(h1 * h1, axis=-1, keepdims=True) + NORM_EPS) * gffn_ref[...]
    u2 = hn * (1.0 + mod_ref[0, 4:5, :]) + mod_ref[0, 3:4, :]

    logits = _dot3(u2, wr_ref[...]) + br_ref[...]
    lane = lax.broadcasted_iota(I32, (tm, LANES), 1)
    big = jnp.int32(LANES)
    rmax = lambda a: jnp.max(a, axis=1, keepdims=True)
    rmin = lambda a: jnp.min(a, axis=1, keepdims=True)
    is_g = lane < N_GROUPS
    gl = jnp.where(is_g, logits, -jnp.inf)
    gmax = rmax(gl)
    gsel = rmin(jnp.where(is_g & (gl == gmax), lane, big))
    gp = 1.0 / jnp.sum(jnp.where(is_g, jnp.exp(gl - gmax), 0.0), axis=1, keepdims=True)
    e_lane = lane - N_GROUPS
    in_grp = (e_lane >= 0) & (e_lane < N_EXPERTS) & ((e_lane // EXPERTS_PER_GROUP) == gsel)
    el = jnp.where(in_grp, logits, -jnp.inf)
    v1 = rmax(el)
    i1 = rmin(jnp.where(in_grp & (el == v1), lane, big))
    el2 = jnp.where(lane == i1, -jnp.inf, el)
    v2 = rmax(el2)
    i2 = rmin(jnp.where(in_grp & (lane != i1) & (el2 == v2), lane, big))
    e2 = jnp.exp(v2 - v1)
    den = 1.0 + e2
    w1 = gp * (1.0 / den)
    w2 = gp * (e2 / den)
    for e in range(N_EXPERTS):
        ge = jnp.where(i1 == e + N_GROUPS, w1, 0.0) + jnp.where(i2 == e + N_GROUPS, w2, 0.0)
        gate_ref[e] = jnp.broadcast_to(ge, (tm, LANES))

    u2b = u2.astype(BF16)
    y = jnp.zeros_like(h1)
    for e in range(N_EXPERTS):
        a = dot(u2b, w1_ref[e])
        b = dot(u2b, w3_ref[e])
        g = gate_ref[e]
        hid = a * jax.nn.sigmoid(a) * b * jnp.concatenate([g] * (EXPERT_FF // LANES), axis=1)
        y = y + dot(hid.astype(BF16), w2_ref[e])
    h2 = h1 + mod_ref[0, 5:6, :] * y
    out_ref[0] = h2 * lax.rsqrt(jnp.mean(h2 * h2, axis=-1, keepdims=True) + NORM_EPS) * gfin_ref[...]


def _merge_moe(x, y_ret, y_dsa, mod, w_o, g_ffn, w_gr, b_gr, w_er, b_er, w1, w3, w2, g_fin):
    bsz, s, d = x.shape
    tm = MOE_TOKENS
    pad = LANES - N_GROUPS - N_EXPERTS
    wr = jnp.concatenate([w_gr, w_er, jnp.zeros((d, pad), F32)], axis=1)
    br = jnp.concatenate([b_gr, b_er, jnp.zeros((pad,), F32)]).reshape(1, LANES)
    tok = pl.BlockSpec((1, tm, d), lambda b, i: (b, i, 0))
    return pl.pallas_call(
        _moe_kernel,
        grid=(bsz, s // tm),
        in_specs=[tok, tok, tok,
                  pl.BlockSpec((1, 6, d), lambda b, i: (b, 0, 0)),
                  _const_spec(w_o.shape), _const_spec((1, d)), _const_spec(wr.shape),
                  _const_spec((1, LANES)), _const_spec(w1.shape), _const_spec(w3.shape),
                  _const_spec(w2.shape), _const_spec((1, d))],
        out_specs=tok,
        out_shape=jax.ShapeDtypeStruct((bsz, s, d), F32),
        scratch_shapes=[pltpu.VMEM((N_EXPERTS, tm, LANES), F32)],
        compiler_params=_params(("arbitrary", "arbitrary")),
        name="merge_moe",
    )(x, y_ret, y_dsa, mod, w_o.astype(BF16), g_ffn.reshape(1, d), wr, br, w1.astype(BF16),
      w3.astype(BF16), w2.astype(BF16), g_fin.reshape(1, d))


def kernel(x, c, w_ada, b_ada, norm_mix_g, w_in, ret_gn_g, dsa_kv_norm_g, w_dsa_kv_up, rel_bias,
           w_ret_out, w_dsa_out, w_gate, b_gate, w_o, norm_ffn_g, w_group_router, b_group_router,
           w_expert_router, b_expert_router, w_exp_gate, w_exp_up, w_exp_down, norm_final_g):
    assert w_ada.shape[0] == 1, "single-layer block"
    bsz, s, d = x.shape
    assert s % DSA_SLOT == 0 and DSA_TILE == PROJ_TOKENS
    mod = _ada(c, w_ada[0], b_ada[0]).reshape(bsz, 6, d)
    (rq, rk, rv, rg, dq, iq, vp, dkt, ikt, iwb, ga, gb) = _proj(
        x, mod, norm_mix_g[0], w_in[0], dsa_kv_norm_g[0], w_dsa_kv_up[0], w_gate[0], b_gate[0])
    y_ret = _retention(rq, rk, rv, rg, ga, ret_gn_g[0], w_ret_out[0])
    y_dsa = _dsa(dq, iq, iwb, gb, ikt, dkt, vp, rel_bias, w_dsa_out[0])
    return _merge_moe(x, y_ret, y_dsa, mod, w_o[0], norm_ffn_g[0], w_group_router[0],
                      b_group_router[0], w_expert_router[0], b_expert_router[0], w_exp_gate[0],
                      w_exp_up[0], w_exp_down[0], norm_final_g)
```

```python
import functools
import math

import jax
import jax.numpy as jnp
from jax import lax
from jax.experimental import pallas as pl
from jax.experimental.pallas import tpu as pltpu

CHUNK = 64
RET_HEADS = 8
RET_QK_DIM = 64
RET_V_DIM = 128
DSA_HEADS = 8
DSA_HEAD_DIM = 64
DSA_KV_LATENT = 128
IDX_HEADS = 8
IDX_DIM = 64
IDX_TOPK = 256
N_BUCKETS = 32
N_GROUPS = 4
EXPERTS_PER_GROUP = 4
N_EXPERTS = N_GROUPS * EXPERTS_PER_GROUP
EXPERT_FF = 256
ROPE_BASE = 10000.0
NORM_EPS = 1e-6
GN_EPS = 1e-5
IN_SPLITS = (512, 512, 1024, 1024, 512, 128, 512, 64, 8)

LANES = 128
VMEM_LIMIT_BYTES = 56 * 1024 * 1024

PROJ_TOKENS = 256
RET_TOKENS = 256
DSA_TILE = 256
DSA_SLOT = 2 * DSA_TILE
SELECT_ROWS = 64
MOE_TOKENS = 256

INTERP_CLIP = 0.02
BISECT_EVERY = 3
MAX_SELECT_STEPS = 128
STRAGGLERS = 8

LOG2E = 1.4426950408889634
NEG_BIG = -1e30
KEY_MIN = -2147483648
BUCKET_STEPS = (12, 16, 23, 32, 46, 64, 91)

F32 = jnp.float32
BF16 = jnp.bfloat16
I32 = jnp.int32


def _const_spec(shape):
    nd = len(shape)
    return pl.BlockSpec(shape, lambda *_: (0,) * nd, pipeline_mode=pl.Buffered(1))


def _params(sem):
    return pltpu.CompilerParams(dimension_semantics=sem, vmem_limit_bytes=VMEM_LIMIT_BYTES)


def _split3(a):
    hi = a.astype(BF16)
    lo = (a - hi.astype(F32)).astype(BF16)
    return hi, lo


def _dot3(a, w):
    ah, al = _split3(a)
    wh, wl = _split3(w)
    d = functools.partial(jnp.dot, preferred_element_type=F32)
    return d(ah, wh) + (d(ah, wl) + d(al, wh))


def _ada_kernel(c_ref, w_ref, b_ref, o_ref):
    c = c_ref[...]
    o_ref[...] = _dot3(c * jax.nn.sigmoid(c), w_ref[...]) + b_ref[...]


def _ada(c, w, b):
    bsz, d = c.shape
    n = w.shape[1]
    rows = 8
    cp = jnp.zeros((rows, d), F32).at[:bsz].set(c)
    tn = 1536
    out = pl.pallas_call(
        _ada_kernel,
        grid=(n // tn,),
        in_specs=[pl.BlockSpec((rows, d), lambda j: (0, 0)),
                  pl.BlockSpec((d, tn), lambda j: (0, j)),
                  pl.BlockSpec((1, tn), lambda j: (0, j))],
        out_specs=pl.BlockSpec((rows, tn), lambda j: (0, j)),
        out_shape=jax.ShapeDtypeStruct((rows, n), F32),
        compiler_params=_params(("arbitrary",)),
        name="adaln",
    )(cp, w, b.reshape(1, n))
    return out[:bsz]


def _proj_kernel(x_ref, mod_ref, g_ref, cos_ref, sin_ref, wqk_ref, wvg_ref, wdq_ref, wiq_ref,
                 wkv_ref, wikw_ref, kvg_ref, wup_ref, wgate_ref, bgate_ref,
                 rq_ref, rk_ref, rv_ref, rg_ref, dq_ref, iq_ref, vp_ref, dkt_ref, ikt_ref,
                 iwb_ref, ga_ref, gb_ref):
    dot = functools.partial(jnp.dot, preferred_element_type=F32)
    x = x_ref[0]
    tm = x.shape[0]
    xn = x * lax.rsqrt(jnp.mean(x * x, axis=-1, keepdims=True) + NORM_EPS) * g_ref[...]
    u = xn * (1.0 + mod_ref[0, 1:2, :]) + mod_ref[0, 0:1, :]
    ub = u.astype(BF16)

    lane = lax.broadcasted_iota(I32, (tm, LANES), 1)
    first_half = (lane % RET_QK_DIM) < (RET_QK_DIM // 2)
    cos = cos_ref[...]
    sin = sin_ref[...]
    zqk = dot(ub, wqk_ref[...])
    for which, out_ref, scale in ((0, rq_ref, 1.0), (1, rk_ref, RET_QK_DIM ** -0.5)):
        for pair in range(RET_HEADS // 2):
            c0 = which * RET_HEADS * RET_QK_DIM + pair * LANES
            z = zqk[:, c0:c0 + LANES]
            rot = jnp.where(first_half, pltpu.roll(z, LANES - 32, 1), pltpu.roll(z, 32, 1))
            r = (z * cos + rot * sin) * scale
            out_ref[0, 2 * pair] = r[:, :RET_QK_DIM].astype(BF16)
            out_ref[0, 2 * pair + 1] = r[:, RET_QK_DIM:].astype(BF16)

    zvg = dot(ub, wvg_ref[...])
    nv = RET_HEADS * RET_V_DIM
    rv_ref[0] = zvg[:, :nv].astype(BF16)
    g = zvg[:, nv:]
    rg_ref[0] = g * jax.nn.sigmoid(g)

    zdq = dot(ub, wdq_ref[...]) * (DSA_HEAD_DIM ** -0.5 * LOG2E)
    ziq = dot(ub, wiq_ref[...])
    for h in range(DSA_HEADS):
        dq_ref[0, h] = zdq[:, h * DSA_HEAD_DIM:(h + 1) * DSA_HEAD_DIM].astype(BF16)
        iq_ref[0, h] = ziq[:, h * IDX_DIM:(h + 1) * IDX_DIM].astype(BF16)

    zkv = dot(ub, wkv_ref[...])
    kvn = zkv * lax.rsqrt(jnp.mean(zkv * zkv, axis=-1, keepdims=True) + NORM_EPS) * kvg_ref[...]
    lat = dot(kvn.astype(BF16), wup_ref[...])
    dkt_ref[0, 0] = lat.T[:DSA_HEAD_DIM, :].astype(BF16)
    vp_ref[0] = jnp.where(lane < DSA_HEAD_DIM, pltpu.roll(lat, DSA_HEAD_DIM, 1), 1.0).astype(BF16)

    zik = dot(ub, wikw_ref[...])
    ikt_ref[0, 0] = zik.T[:IDX_DIM, :].astype(BF16)
    idx_scale = (IDX_DIM ** -0.5) * (IDX_HEADS ** -0.5)
    for h in range(IDX_HEADS):
        col = zik[:, IDX_DIM + h:IDX_DIM + h + 1] * idx_scale
        iwb_ref[0, :, h * LANES:(h + 1) * LANES] = jnp.broadcast_to(col, (tm, LANES))

    zg = jax.nn.sigmoid(dot(ub, wgate_ref[...]) + bgate_ref[...])
    d = zg.shape[1] // 2
    ga_ref[0] = zg[:, :d]
    gb_ref[0] = zg[:, d:]


def _proj(x, mod, g_mix, w_in, kv_g, w_up, w_gate, b_gate):
    bsz, s, d = x.shape
    tm = PROJ_TOKENS
    nt = s // tm
    offs = [0]
    for n in IN_SPLITS:
        offs.append(offs[-1] + n)
    seg = lambda i: w_in[:, offs[i]:offs[i + 1]]
    wqk = jnp.concatenate([seg(0), seg(1)], axis=1).astype(BF16)
    wvg = jnp.concatenate([seg(2), seg(3)], axis=1).astype(BF16)
    wdq = seg(4).astype(BF16)
    wkv = seg(5).astype(BF16)
    wiq = seg(6).astype(BF16)
    wikw = jnp.concatenate([seg(7), seg(8), jnp.zeros((d, LANES - IDX_DIM - IDX_HEADS), F32)],
                           axis=1).astype(BF16)

    pos = jnp.arange(s, dtype=jnp.int32)
    freqs = ROPE_BASE ** (-jnp.arange(0, RET_QK_DIM, 2, dtype=F32) / RET_QK_DIM)
    ang = pos.astype(F32)[:, None] * freqs[None, :]
    cos_h = jnp.concatenate([jnp.cos(ang), jnp.cos(ang)], axis=1)
    sin_h = jnp.concatenate([-jnp.sin(ang), jnp.sin(ang)], axis=1)
    cos_t = jnp.concatenate([cos_h, cos_h], axis=1)
    sin_t = jnp.concatenate([sin_h, sin_h], axis=1)

    tok = lambda n: pl.BlockSpec((1, tm, n), lambda b, i: (b, i, 0))
    heads = pl.BlockSpec((1, DSA_HEADS, tm, DSA_HEAD_DIM), lambda b, i: (b, 0, i, 0))
    trans = pl.BlockSpec((1, 1, DSA_HEAD_DIM, tm), lambda b, i: (b, i // 2, 0, i % 2))
    outs = pl.pallas_call(
        _proj_kernel,
        grid=(bsz, nt),
        in_specs=[tok(d),
                  pl.BlockSpec((1, 6, d), lambda b, i: (b, 0, 0)),
                  _const_spec((1, d)),
                  pl.BlockSpec((tm, LANES), lambda b, i: (i, 0)),
                  pl.BlockSpec((tm, LANES), lambda b, i: (i, 0)),
                  _const_spec(wqk.shape), _const_spec(wvg.shape), _const_spec(wdq.shape),
                  _const_spec(wiq.shape), _const_spec(wkv.shape), _const_spec(wikw.shape),
                  _const_spec((1, DSA_KV_LATENT)), _const_spec(w_up.shape),
                  _const_spec(w_gate.shape), _const_spec((1, w_gate.shape[1]))],
        out_specs=[heads, heads, tok(1024), tok(1024), heads, heads, tok(LANES), trans, trans,
                   tok(IDX_HEADS * LANES), tok(d), tok(d)],
        out_shape=[jax.ShapeDtypeStruct((bsz, RET_HEADS, s, RET_QK_DIM), BF16),
                   jax.ShapeDtypeStruct((bsz, RET_HEADS, s, RET_QK_DIM), BF16),
                   jax.ShapeDtypeStruct((bsz, s, RET_HEADS * RET_V_DIM), BF16),
                   jax.ShapeDtypeStruct((bsz, s, RET_HEADS * RET_V_DIM), F32),
                   jax.ShapeDtypeStruct((bsz, DSA_HEADS, s, DSA_HEAD_DIM), BF16),
                   jax.ShapeDtypeStruct((bsz, IDX_HEADS, s, IDX_DIM), BF16),
                   jax.ShapeDtypeStruct((bsz, s, LANES), BF16),
                   jax.ShapeDtypeStruct((bsz, nt // 2, DSA_HEAD_DIM, 2 * tm), BF16),
                   jax.ShapeDtypeStruct((bsz, nt // 2, IDX_DIM, 2 * tm), BF16),
                   jax.ShapeDtypeStruct((bsz, s, IDX_HEADS * LANES), F32),
                   jax.ShapeDtypeStruct((bsz, s, d), F32),
                   jax.ShapeDtypeStruct((bsz, s, d), F32)],
        compiler_params=_params(("arbitrary", "arbitrary")),
        name="in_proj",
    )(x, mod, g_mix.reshape(1, d), cos_t, sin_t, wqk, wvg, wdq, wiq, wkv, wikw,
      kv_g.reshape(1, -1), w_up.astype(BF16), w_gate.astype(BF16), b_gate.reshape(1, -1))
    return outs


def _ret_kernel(rq_ref, rk_ref, rv_ref, rg_ref, ga_ref, dmat_ref, qdec_ref, kdec_ref, cdec_ref,
                gng_ref, wout_ref, y_ref, state_ref, o_ref):
    dot = functools.partial(jnp.dot, preferred_element_type=F32)

    @pl.when(pl.program_id(1) == 0)
    def _():
        state_ref[...] = jnp.zeros_like(state_ref)

    for h in range(RET_HEADS):
        q = rq_ref[0, h]
        k = rk_ref[0, h]
        cols = slice(h * RET_V_DIM, (h + 1) * RET_V_DIM)
        v = rv_ref[0, :, cols]
        st = state_ref[h]
        s = lax.dot_general(q, k, (((1,), (1,)), ((), ())), preferred_element_type=F32)
        o = dot((s * dmat_ref[h]).astype(BF16), v)
        qd = (q.astype(F32) * qdec_ref[h]).astype(BF16)
        o = o + dot(qd, st.astype(BF16))
        kd = (k.astype(F32) * kdec_ref[h]).astype(BF16)
        kv = lax.dot_general(kd, v, (((0,), (0,)), ((), ())), preferred_element_type=F32)
        state_ref[h] = st * cdec_ref[h] + kv
        mu = jnp.mean(o, axis=-1, keepdims=True)
        oc = o - mu
        var = jnp.mean(oc * oc, axis=-1, keepdims=True)
        on = oc * lax.rsqrt(var + GN_EPS) * gng_ref[:, cols]
        o_ref[:, cols] = (rg_ref[0, :, cols] * on).astype(BF16)
    y_ref[0] = ga_ref[0] * dot(o_ref[...], wout_ref[...])


def _retention(rq, rk, rv, rg, ga, gn_g, w_out):
    bsz, s, nv = rv.shape
    d = w_out.shape[1]
    tr = RET_TOKENS
    log_gamma = jnp.log(1.0 - 2.0 ** (-5.0 - jnp.arange(RET_HEADS, dtype=F32)))
    idx = jnp.arange(tr, dtype=F32)
    ch = jnp.arange(tr, dtype=jnp.int32) // CHUNK
    dist = jnp.abs(idx[:, None] - idx[None, :])
    dmat = jnp.where((ch[None, :] <= ch[:, None])[None],
                     jnp.exp(log_gamma[:, None, None] * dist[None]), 0.0)
    qdec = jnp.broadcast_to(jnp.exp(log_gamma[:, None] * (idx + 1.0)[None, :])[:, :, None],
                            (RET_HEADS, tr, RET_QK_DIM))
    kdec = jnp.broadcast_to(jnp.exp(log_gamma[:, None] * (tr - 1.0 - idx)[None, :])[:, :, None],
                            (RET_HEADS, tr, RET_QK_DIM))
    cdec = jnp.broadcast_to(jnp.exp(log_gamma * tr)[:, None, None],
                            (RET_HEADS, RET_QK_DIM, RET_V_DIM))

    heads = pl.BlockSpec((1, RET_HEADS, tr, RET_QK_DIM), lambda b, i: (b, 0, i, 0))
    tok = lambda n: pl.BlockSpec((1, tr, n), lambda b, i: (b, i, 0))
    return pl.pallas_call(
        _ret_kernel,
        grid=(bsz, s // tr),
        in_specs=[heads, heads, tok(nv), tok(nv), tok(d),
                  _const_spec(dmat.shape), _const_spec(qdec.shape), _const_spec(kdec.shape),
                  _const_spec(cdec.shape), _const_spec((1, nv)), _const_spec(w_out.shape)],
        out_specs=tok(d),
        out_shape=jax.ShapeDtypeStruct((bsz, s, d), F32),
        scratch_shapes=[pltpu.VMEM((RET_HEADS, RET_QK_DIM, RET_V_DIM), F32),
                        pltpu.VMEM((tr, nv), BF16)],
        compiler_params=_params(("arbitrary", "arbitrary")),
        name="retention",
    )(rq, rk, rv, rg, ga, dmat, qdec, kdec, cdec, gn_g.reshape(1, nv), w_out.astype(BF16))


def _sortable(bits):
    return jnp.where(bits < 0, bits ^ 0x7FFFFFFF, bits)


def _midpoint(lo, hi):
    return (lo >> 1) + (hi >> 1) + (lo & hi & 1)


def _lanes(a, n):
    return jnp.concatenate([a] * (n // LANES), axis=1)


def _dsa_kernel(relb_ref, dq_ref, iq_ref, iwb_ref, gb_ref, ikt_ref, dkt_ref, vp_ref, wout_ref,
                y_ref, keys_ref, lo_ref, clo_ref, hi_ref, mid_ref, cnt_ref, strag_ref, smax_ref,
                smin_ref, bias_ref, m_ref, acc_ref, o_ref, *, topk):
    t = DSA_TILE
    w2 = DSA_SLOT
    dot = functools.partial(jnp.dot, preferred_element_type=F32)
    i = pl.program_id(1)
    odd = (i % 2) == 1
    last = i // 2
    n_slots = last + 1
    row = lax.broadcasted_iota(I32, (t, t), 0)
    col = lax.broadcasted_iota(I32, (t, t), 1)
    diag_adm = (col // CHUNK) <= (row // CHUNK)
    far = (N_BUCKETS // 2 - 1) * DSA_HEADS

    @pl.when((pl.program_id(0) == 0) & (i == 0))
    def _():
        for which in range(2):
            rel = col - row - (t if which == 0 else 0)
            n = jnp.abs(rel)
            large = jnp.full((t, t), 8, I32)
            for step in BUCKET_STEPS:
                large = large + (n >= step).astype(I32)
            bucket = jnp.where(n < 8, n, large) + jnp.where(rel > 0, N_BUCKETS // 2, 0)
            for h in range(DSA_HEADS):
                b = jnp.zeros((t, t), F32)
                for k in range(N_BUCKETS):
                    b = jnp.where(bucket == k, relb_ref[k * DSA_HEADS + h], b)
                bias_ref[h, which] = (b - relb_ref[far + h]) * LOG2E

    smax_ref[...] = jnp.full((t, LANES), -jnp.inf, F32)
    smin_ref[...] = jnp.full((t, LANES), jnp.inf, F32)

    def score_cols(kt, adm):
        n = kt.shape[1]
        s = dot(iq_ref[0].reshape(IDX_HEADS * t, IDX_DIM), kt).reshape(IDX_HEADS, t, n)
        acc = jnp.zeros((t, n), F32)
        for h in range(IDX_HEADS):
            acc = acc + jnp.maximum(s[h], 0.0) * _lanes(iwb_ref[0, :, h * LANES:(h + 1) * LANES], n)
        hi_src = acc if adm is None else jnp.where(adm, acc, -jnp.inf)
        lo_src = acc if adm is None else jnp.where(adm, acc, jnp.inf)
        mx = smax_ref[...]
        mn = smin_ref[...]
        for c0 in range(0, n, LANES):
            mx = jnp.maximum(mx, hi_src[:, c0:c0 + LANES])
            mn = jnp.minimum(mn, lo_src[:, c0:c0 + LANES])
        smax_ref[...] = mx
        smin_ref[...] = mn
        key = _sortable(pltpu.bitcast(acc, I32))
        return key if adm is None else jnp.where(adm, key, KEY_MIN)

    def score_body(slot, carry):
        keys_ref[slot] = score_cols(ikt_ref[0, slot], None)
        return carry

    lax.fori_loop(0, last, score_body, 0)

    @pl.when(odd)
    def _():
        keys_ref[last, :, :t] = score_cols(ikt_ref[0, last, :, :t], None)
        keys_ref[last, :, t:] = score_cols(ikt_ref[0, last, :, t:], diag_adm)

    @pl.when(jnp.logical_not(odd))
    def _():
        keys_ref[last, :, :t] = score_cols(ikt_ref[0, last, :, :t], diag_adm)
        keys_ref[last, :, t:] = jnp.full((t, t), KEY_MIN, I32)

    def spread(a):
        f = pltpu.bitcast(a[0:1, :], F32)
        return pltpu.bitcast(jnp.broadcast_to(f, (LANES, t)).T, I32)

    ones_row = jnp.ones((8, LANES), BF16)
    pos = i * t + lax.broadcasted_iota(I32, (8, t), 1)
    n_adm = (pos // CHUNK + 1) * CHUNK
    rmax = jnp.max(smax_ref[...].T, axis=0, keepdims=True)
    rmin = jnp.min(smin_ref[...].T, axis=0, keepdims=True)
    lo0 = jnp.broadcast_to(_sortable(pltpu.bitcast(rmin, I32)), (8, t))
    hi0 = jnp.broadcast_to(_sortable(pltpu.bitcast(rmax, I32)) + 1, (8, t))
    hi0 = jnp.where(n_adm <= topk, lo0 + 1, hi0)
    log_k = math.log(topk - 0.5)

    def unresolved(lo, hi, clo):
        return (clo != topk) & (_midpoint(lo, hi) > lo)

    def count_unresolved(lo, hi, clo):
        return jnp.sum(jnp.where(unresolved(lo, hi, clo)[0:1, :], 1.0, 0.0))

    def split_point(step, lo, hi, clo, chi):
        lof = pltpu.bitcast(_sortable(lo), F32)
        hif = pltpu.bitcast(_sortable(hi - 1), F32)
        la = jnp.log(clo.astype(F32))
        lb = jnp.log(jnp.maximum(chi.astype(F32), 0.5))
        frac = jnp.clip((la - log_k) / (la - lb), INTERP_CLIP, 1.0 - INTERP_CLIP)
        mid_i = _sortable(pltpu.bitcast(lof + (hif - lof) * frac, I32))
        mid = jnp.where(step % BISECT_EVERY == BISECT_EVERY - 1, _midpoint(lo, hi), mid_i)
        return jnp.minimum(jnp.maximum(mid, lo + 1), hi - 1)

    def select_cond(carry):
        return (carry[0] < MAX_SELECT_STEPS) & (carry[1] > STRAGGLERS + 0.5)

    def select_body(carry):
        step, _, mid, lo, hi, clo, chi = carry
        for s0 in range(0, t, SELECT_ROWS):
            rows = slice(s0, s0 + SELECT_ROWS)
            midr = mid_ref[rows, :]

            def count_body(slot, cnt):
                for c0 in range(0, w2, LANES):
                    cnt = cnt + (keys_ref[slot, rows, c0:c0 + LANES] >= midr).astype(I32)
                return cnt

            cnt_ref[rows, :] = lax.fori_loop(0, n_slots, count_body,
                                             jnp.zeros((SELECT_ROWS, LANES), I32))

        cnt_b = cnt_ref[...].astype(F32).astype(BF16)
        c = lax.dot_general(ones_row, cnt_b, (((1,), (1,)), ((), ())),
                            preferred_element_type=F32).astype(I32)
        active = unresolved(lo, hi, clo)
        up = active & (c >= topk)
        down = active & (c < topk)
        lo = jnp.where(up, mid, lo)
        hi = jnp.where(down, mid, hi)
        clo = jnp.where(up, c, clo)
        chi = jnp.where(down, c, chi)
        mid = split_point(step + 1, lo, hi, clo, chi)
        mid_ref[...] = spread(mid)
        return step + 1, count_unresolved(lo, hi, clo), mid, lo, hi, clo, chi

    chi0 = jnp.zeros((8, t), I32)
    mid0 = split_point(0, lo0, hi0, n_adm, chi0)
    mid_ref[...] = spread(mid0)
    sel = lax.while_loop(select_cond, select_body,
                         (jnp.int32(0), count_unresolved(lo0, hi0, n_adm), mid0, lo0, hi0, n_adm,
                          chi0))
    lo_ref[...] = spread(sel[3])
    hi_ref[...] = spread(sel[4])
    clo_ref[...] = spread(sel[5])

    left = unresolved(sel[3], sel[4], sel[5])

    @pl.when(sel[1] > 0.5)
    def _():
        lane_f = lax.broadcasted_iota(I32, (8, t), 1).astype(F32)
        todo = left
        picks = []
        for _ in range(STRAGGLERS):
            first = jnp.min(jnp.where(todo, lane_f, float(t)))
            todo = todo & (lane_f != first)
            r = first.astype(I32)
            picks.append((jnp.minimum(r, t - 1), r < t))

        def gather_body(slot, carry):
            for j, (r, _) in enumerate(picks):
                strag_ref[slot, j:j + 1, :] = keys_ref[slot, pl.ds(r, 1), :]
            return carry

        lax.fori_loop(0, n_slots, gather_body, 0)

        def rows_of(ref):
            return jnp.concatenate([ref[pl.ds(r, 1), :] for r, _ in picks], axis=0)

        lo8 = rows_of(lo_ref)
        clo8 = rows_of(clo_ref)
        real = jnp.concatenate([jnp.broadcast_to(ok.astype(I32), (1, LANES)) for _, ok in picks],
                               axis=0) > 0
        hi8 = jnp.where(real, rows_of(hi_ref), lo8 + 1)

        def open8(lo, hi, clo):
            return jnp.max(jnp.where(unresolved(lo, hi, clo), 1.0, 0.0))

        def finish_cond(carry):
            return carry[0] > 0.5

        def finish_body(carry):
            _, lo, hi, clo = carry
            mid = _midpoint(lo, hi)

            def count_body(slot, cnt):
                for c0 in range(0, w2, LANES):
                    cnt = cnt + (strag_ref[slot, :, c0:c0 + LANES] >= mid).astype(I32)
                return cnt

            cnt = lax.fori_loop(0, n_slots, count_body, jnp.zeros((STRAGGLERS, LANES), I32))
            c = jnp.broadcast_to(jnp.sum(cnt.astype(F32), axis=1, keepdims=True),
                                 (STRAGGLERS, LANES)).astype(I32)
            active = unresolved(lo, hi, clo)
            up = active & (c >= topk)
            lo = jnp.where(up, mid, lo)
            hi = jnp.where(active & (c < topk), mid, hi)
            clo = jnp.where(up, c, clo)
            return open8(lo, hi, clo), lo, hi, clo

        _, lo8, _, clo8 = lax.while_loop(finish_cond, finish_body,
                                         (open8(lo8, hi8, clo8), lo8, hi8, clo8))
        for j, (r, ok) in enumerate(picks):
            @pl.when(ok)
            def _():
                lo_ref[pl.ds(r, 1), :] = lo8[j:j + 1, :]
                clo_ref[pl.ds(r, 1), :] = clo8[j:j + 1, :]

    n_ties = jnp.max(jnp.where(clo_ref[...] > topk, 1, 0))

    @pl.when(n_ties > 0)
    def _():
        col2 = lax.broadcasted_iota(I32, (t, w2), 1)
        lo = _lanes(lo_ref[...], w2)
        tie = _lanes(clo_ref[...], w2) > topk

        def gt_body(slot, cnt):
            return cnt + jnp.sum((keys_ref[slot] > lo).astype(I32), axis=1, keepdims=True)

        need = topk - lax.fori_loop(0, n_slots, gt_body, jnp.zeros((t, 1), I32))

        def cut_body(_, carry):
            jlo, jhi = carry
            jm = (jlo + jhi) >> 1

            def eq_body(slot, cnt):
                hit = (keys_ref[slot] == lo) & ((col2 + slot * w2) < jm)
                return cnt + jnp.sum(hit.astype(I32), axis=1, keepdims=True)

            c = lax.fori_loop(0, n_slots, eq_body, jnp.zeros((t, 1), I32))
            ok = c >= need
            return jnp.where(ok, jlo, jm), jnp.where(ok, jm, jhi)

        n_steps = (keys_ref.shape[0] * w2).bit_length() + 1
        _, cut = lax.fori_loop(0, n_steps, cut_body,
                               (jnp.zeros((t, 1), I32), jnp.full((t, 1), 1, I32) * (n_slots * w2)))

        def drop_body(slot, carry):
            key = keys_ref[slot]
            drop = tie & (key == lo) & ((col2 + slot * w2) >= cut)
            keys_ref[slot] = jnp.where(drop, KEY_MIN, key)
            return carry

        lax.fori_loop(0, n_slots, drop_body, 0)

    m_ref[...] = jnp.full(m_ref.shape, NEG_BIG, F32)
    acc_ref[...] = jnp.zeros(acc_ref.shape, F32)

    def attend(slot, biases):
        n = len(biases) * t
        nh = DSA_HEADS
        thr = _lanes(lo_ref[...], n)
        maskb = jnp.where(keys_ref[slot, :, :n] >= thr, 0.0, NEG_BIG)
        v = vp_ref[0, pl.ds(pl.multiple_of(slot * w2, w2), n), :]
        q = dq_ref[0].reshape(nh * t, DSA_HEAD_DIM)
        lg = dot(q, dkt_ref[0, slot, :, :n]).reshape(nh, t, n) + maskb[None]
        if any(which is not None for which in biases):
            lg = jnp.concatenate(
                [lg[:, :, k * t:(k + 1) * t] if which is None
                 else lg[:, :, k * t:(k + 1) * t] + bias_ref[:, which]
                 for k, which in enumerate(biases)], axis=2)
        m_old = m_ref[...]
        m_new = jnp.maximum(m_old, jnp.broadcast_to(jnp.max(lg, axis=2, keepdims=True),
                                                    (nh, t, LANES)))
        p = jnp.exp2(lg - jnp.concatenate([m_new] * (n // LANES), axis=2))
        pv = dot(p.astype(BF16).reshape(nh * t, n), v).reshape(nh, t, LANES)
        acc_ref[...] = acc_ref[...] * jnp.exp2(m_old - m_new) + pv
        m_ref[...] = m_new

    def far_body(slot, carry):
        attend(slot, (None, None))
        return carry

    lax.fori_loop(0, jnp.where(odd, last, last - 1), far_body, 0)

    @pl.when(odd)
    def _():
        attend(last, (0, 1))

    @pl.when(jnp.logical_not(odd) & (i >= 2))
    def _():
        attend(last - 1, (None, 0))

    @pl.when(jnp.logical_not(odd))
    def _():
        attend(last, (1,))

    lane = lax.broadcasted_iota(I32, (t, LANES), 1)
    for pair in range(DSA_HEADS // 2):
        a0 = acc_ref[2 * pair]
        a1 = acc_ref[2 * pair + 1]
        even = a0 / pltpu.roll(a0, DSA_HEAD_DIM, 1)
        odd_h = pltpu.roll(a1, DSA_HEAD_DIM, 1) / a1
        o_ref[:, pair * LANES:(pair + 1) * LANES] = jnp.where(lane < DSA_HEAD_DIM, even,
                                                              odd_h).astype(BF16)
    y_ref[0] = gb_ref[0] * dot(o_ref[...], wout_ref[...])


def _dsa(dq, iq, iwb, gb, ikt, dkt, vp, rel_bias, w_out):
    bsz, _, s, _ = dq.shape
    d = w_out.shape[1]
    t = DSA_TILE
    nq = s // t
    n_slots = nq // 2
    topk = min(IDX_TOPK, s // 4)
    heads = pl.BlockSpec((1, DSA_HEADS, t, DSA_HEAD_DIM), lambda b, i: (b, 0, i, 0))
    tok = lambda n: pl.BlockSpec((1, t, n), lambda b, i: (b, i, 0))
    per_batch = lambda shape: pl.BlockSpec((1,) + shape, lambda b, i: (b,) + (0,) * len(shape),
                                           pipeline_mode=pl.Buffered(1))
    row_state = lambda dt: pltpu.VMEM((t, LANES), dt)
    return pl.pallas_call(
        functools.partial(_dsa_kernel, topk=topk),
        grid=(bsz, nq),
        in_specs=[pl.BlockSpec(memory_space=pltpu.SMEM),
                  heads, heads, tok(IDX_HEADS * LANES), tok(d),
                  per_batch((n_slots, IDX_DIM, DSA_SLOT)), per_batch((n_slots, DSA_HEAD_DIM, DSA_SLOT)),
                  per_batch((s, LANES)), _const_spec(w_out.shape)],
        out_specs=tok(d),
        out_shape=jax.ShapeDtypeStruct((bsz, s, d), F32),
        scratch_shapes=[pltpu.VMEM((n_slots, t, DSA_SLOT), I32),
                        row_state(I32), row_state(I32), row_state(I32),
                        row_state(I32), row_state(I32),
                        pltpu.VMEM((n_slots, STRAGGLERS, DSA_SLOT), I32),
                        row_state(F32), row_state(F32),
                        pltpu.VMEM((DSA_HEADS, 2, t, t), F32),
                        pltpu.VMEM((DSA_HEADS, t, LANES), F32),
                        pltpu.VMEM((DSA_HEADS, t, LANES), F32),
                        pltpu.VMEM((t, DSA_HEADS * DSA_HEAD_DIM), BF16)],
        compiler_params=_params(("arbitrary", "arbitrary")),
        name="dsa",
    )(rel_bias.reshape(-1), dq, iq, iwb, gb, ikt, dkt, vp, w_out.astype(BF16))


def _moe_kernel(x_ref, yr_ref, yd_ref, mod_ref, wo_ref, gffn_ref, wr_ref, br_ref, w1_ref, w3_ref,
                w2_ref, gfin_ref, out_ref, gate_ref):
    dot = functools.partial(jnp.dot, preferred_element_type=F32)
    tm = x_ref.shape[1]
    mix = dot((yr_ref[0] + yd_ref[0]).astype(BF16), wo_ref[...])
    h1 = x_ref[0] + mod_ref[0, 2:3, :] * mix
    hn = h1 * lax.rsqrt(jnp.mean(h1 * h1, axis=-1, keepdims=True) + NORM_EPS) * gffn_ref[...]
    u2 = hn * (1.0 + mod_ref[0, 4:5, :]) + mod_ref[0, 3:4, :]

    logits = _dot3(u2, wr_ref[...]) + br_ref[...]
    lane = lax.broadcasted_iota(I32, (tm, LANES), 1)
    big = jnp.int32(LANES)
    rmax = lambda a: jnp.max(a, axis=1, keepdims=True)
    rmin = lambda a: jnp.min(a, axis=1, keepdims=True)
    is_g = lane < N_GROUPS
    gl = jnp.where(is_g, logits, -jnp.inf)
    gmax = rmax(gl)
    gsel = rmin(jnp.where(is_g & (gl == gmax), lane, big))
    gp = 1.0 / jnp.sum(jnp.where(is_g, jnp.exp(gl - gmax), 0.0), axis=1, keepdims=True)
    e_lane = lane - N_GROUPS
    in_grp = (e_lane >= 0) & (e_lane < N_EXPERTS) & ((e_lane // EXPERTS_PER_GROUP) == gsel)
    el = jnp.where(in_grp, logits, -jnp.inf)
    v1 = rmax(el)
    i1 = rmin(jnp.where(in_grp & (el == v1), lane, big))
    el2 = jnp.where(lane == i1, -jnp.inf, el)
    v2 = rmax(el2)
    i2 = rmin(jnp.where(in_grp & (lane != i1) & (el2 == v2), lane, big))
    e2 = jnp.exp(v2 - v1)
    den = 1.0 + e2
    w1 = gp * (1.0 / den)
    w2 = gp * (e2 / den)
    for e in range(N_EXPERTS):
        ge = jnp.where(i1 == e + N_GROUPS, w1, 0.0) + jnp.where(i2 == e + N_GROUPS, w2, 0.0)
        gate_ref[e] = jnp.broadcast_to(ge, (tm, LANES))

    u2b = u2.astype(BF16)
    y = jnp.zeros_like(h1)
    for e in range(N_EXPERTS):
        a = dot(u2b, w1_ref[e])
        b = dot(u2b, w3_ref[e])
        g = gate_ref[e]
        hid = a * jax.nn.sigmoid(a) * b * jnp.concatenate([g] * (EXPERT_FF // LANES), axis=1)
        y = y + dot(hid.astype(BF16), w2_ref[e])
    h2 = h1 + mod_ref[0, 5:6, :] * y
    out_ref[0] = h2 * lax.rsqrt(jnp.mean(h2 * h2, axis=-1, keepdims=True) + NORM_EPS) * gfin_ref[...]


def _merge_moe(x, y_ret, y_dsa, mod, w_o, g_ffn, w_gr, b_gr, w_er, b_er, w1, w3, w2, g_fin):
    bsz, s, d = x.shape
    tm = MOE_TOKENS
    pad = LANES - N_GROUPS - N_EXPERTS
    wr = jnp.concatenate([w_gr, w_er, jnp.zeros((d, pad), F32)], axis=1)
    br = jnp.concatenate([b_gr, b_er, jnp.zeros((pad,), F32)]).reshape(1, LANES)
    tok = pl.BlockSpec((1, tm, d), lambda b, i: (b, i, 0))
    return pl.pallas_call(
        _moe_kernel,
        grid=(bsz, s // tm),
        in_specs=[tok, tok, tok,
                  pl.BlockSpec((1, 6, d), lambda b, i: (b, 0, 0)),
                  _const_spec(w_o.shape), _const_spec((1, d)), _const_spec(wr.shape),
                  _const_spec((1, LANES)), _const_spec(w1.shape), _const_spec(w3.shape),
                  _const_spec(w2.shape), _const_spec((1, d))],
        out_specs=tok,
        out_shape=jax.ShapeDtypeStruct((bsz, s, d), F32),
        scratch_shapes=[pltpu.VMEM((N_EXPERTS, tm, LANES), F32)],
        compiler_params=_params(("arbitrary", "arbitrary")),
        name="merge_moe",
    )(x, y_ret, y_dsa, mod, w_o.astype(BF16), g_ffn.reshape(1, d), wr, br, w1.astype(BF16),
      w3.astype(BF16), w2.astype(BF16), g_fin.reshape(1, d))


def kernel(x, c, w_ada, b_ada, norm_mix_g, w_in, ret_gn_g, dsa_kv_norm_g, w_dsa_kv_up, rel_bias,
           w_ret_out, w_dsa_out, w_gate, b_gate, w_o, norm_ffn_g, w_group_router, b_group_router,
           w_expert_router, b_expert_router, w_exp_gate, w_exp_up, w_exp_down, norm_final_g):
    assert w_ada.shape[0] == 1, "single-layer block"
    bsz, s, d = x.shape
    assert s % DSA_SLOT == 0 and DSA_TILE == PROJ_TOKENS
    mod = _ada(c, w_ada[0], b_ada[0]).reshape(bsz, 6, d)
    (rq, rk, rv, rg, dq, iq, vp, dkt, ikt, iwb, ga, gb) = _proj(
        x, mod, norm_mix_g[0], w_in[0], dsa_kv_norm_g[0], w_dsa_kv_up[0], w_gate[0], b_gate[0])
    y_ret = _retention(rq, rk, rv, rg, ga, ret_gn_g[0], w_ret_out[0])
    y_dsa = _dsa(dq, iq, iwb, gb, ikt, dkt, vp, rel_bias, w_dsa_out[0])
    return _merge_moe(x, y_ret, y_dsa, mod, w_o[0], norm_ffn_g[0], w_group_router[0],
                      b_group_router[0], w_expert_router[0], b_expert_router[0], w_exp_gate[0],
                      w_exp_up[0], w_exp_down[0], norm_final_g)
```

```python
import functools
import math

import jax
import jax.numpy as jnp
from jax import lax
from jax.experimental import pallas as pl
from jax.experimental.pallas import tpu as pltpu

CHUNK = 64
RET_HEADS = 8
RET_QK_DIM = 64
RET_V_DIM = 128
DSA_HEADS = 8
DSA_HEAD_DIM = 64
DSA_KV_LATENT = 128
IDX_HEADS = 8
IDX_DIM = 64
IDX_TOPK = 256
N_BUCKETS = 32
N_GROUPS = 4
EXPERTS_PER_GROUP = 4
N_EXPERTS = N_GROUPS * EXPERTS_PER_GROUP
EXPERT_FF = 256
ROPE_BASE = 10000.0
NORM_EPS = 1e-6
GN_EPS = 1e-5
IN_SPLITS = (512, 512, 1024, 1024, 512, 128, 512, 64, 8)

LANES = 128
VMEM_LIMIT_BYTES = 56 * 1024 * 1024

PROJ_TOKENS = 256
RET_TOKENS = 256
DSA_TILE = 256
DSA_SLOT = 2 * DSA_TILE
SELECT_ROWS = 64
MOE_TOKENS = 256

INTERP_CLIP = 0.02
BISECT_EVERY = 3
MAX_SELECT_STEPS = 128
STRAGGLERS = 8

LOG2E = 1.4426950408889634
NEG_BIG = -1e30
KEY_MIN = -2147483648
KEY_MAX = 2147483647
BUCKET_STEPS = (12, 16, 23, 32, 46, 64, 91)

F32 = jnp.float32
BF16 = jnp.bfloat16
I32 = jnp.int32


def _const_spec(shape):
    nd = len(shape)
    return pl.BlockSpec(shape, lambda *_: (0,) * nd, pipeline_mode=pl.Buffered(1))


def _params(sem):
    return pltpu.CompilerParams(dimension_semantics=sem, vmem_limit_bytes=VMEM_LIMIT_BYTES)


def _split3(a):
    hi = a.astype(BF16)
    lo = (a - hi.astype(F32)).astype(BF16)
    return hi, lo


def _dot3(a, w):
    ah, al = _split3(a)
    wh, wl = _split3(w)
    d = functools.partial(jnp.dot, preferred_element_type=F32)
    return d(ah, wh) + (d(ah, wl) + d(al, wh))


def _ada_kernel(c_ref, w_ref, b_ref, o_ref):
    c = c_ref[...]
    o_ref[...] = _dot3(c * jax.nn.sigmoid(c), w_ref[...]) + b_ref[...]


def _ada(c, w, b):
    bsz, d = c.shape
    n = w.shape[1]
    rows = 8
    cp = jnp.zeros((rows, d), F32).at[:bsz].set(c)
    tn = 1536
    out = pl.pallas_call(
        _ada_kernel,
        grid=(n // tn,),
        in_specs=[pl.BlockSpec((rows, d), lambda j: (0, 0)),
                  pl.BlockSpec((d, tn), lambda j: (0, j)),
                  pl.BlockSpec((1, tn), lambda j: (0, j))],
        out_specs=pl.BlockSpec((rows, tn), lambda j: (0, j)),
        out_shape=jax.ShapeDtypeStruct((rows, n), F32),
        compiler_params=_params(("arbitrary",)),
        name="adaln",
    )(cp, w, b.reshape(1, n))
    return out[:bsz]


def _proj_kernel(x_ref, mod_ref, g_ref, cos_ref, sin_ref, wqk_ref, wvg_ref, wdq_ref, wiq_ref,
                 wkv_ref, wikw_ref, kvg_ref, wup_ref, wgate_ref, bgate_ref,
                 rq_ref, rk_ref, rv_ref, rg_ref, dq_ref, iq_ref, vp_ref, dkt_ref, ikt_ref,
                 iwb_ref, ga_ref, gb_ref):
    dot = functools.partial(jnp.dot, preferred_element_type=F32)
    x = x_ref[0]
    tm = x.shape[0]
    xn = x * lax.rsqrt(jnp.mean(x * x, axis=-1, keepdims=True) + NORM_EPS) * g_ref[...]
    u = xn * (1.0 + mod_ref[0, 1:2, :]) + mod_ref[0, 0:1, :]
    ub = u.astype(BF16)

    lane = lax.broadcasted_iota(I32, (tm, LANES), 1)
    first_half = (lane % RET_QK_DIM) < (RET_QK_DIM // 2)
    cos = cos_ref[...]
    sin = sin_ref[...]
    zqk = dot(ub, wqk_ref[...])
    for which, out_ref, scale in ((0, rq_ref, 1.0), (1, rk_ref, RET_QK_DIM ** -0.5)):
        for pair in range(RET_HEADS // 2):
            c0 = which * RET_HEADS * RET_QK_DIM + pair * LANES
            z = zqk[:, c0:c0 + LANES]
            rot = jnp.where(first_half, pltpu.roll(z, LANES - 32, 1), pltpu.roll(z, 32, 1))
            r = (z * cos + rot * sin) * scale
            out_ref[0, 2 * pair] = r[:, :RET_QK_DIM].astype(BF16)
            out_ref[0, 2 * pair + 1] = r[:, RET_QK_DIM:].astype(BF16)

    zvg = dot(ub, wvg_ref[...])
    nv = RET_HEADS * RET_V_DIM
    rv_ref[0] = zvg[:, :nv].astype(BF16)
    g = zvg[:, nv:]
    rg_ref[0] = g * jax.nn.sigmoid(g)

    zdq = dot(ub, wdq_ref[...]) * (DSA_HEAD_DIM ** -0.5 * LOG2E)
    ziq = dot(ub, wiq_ref[...])
    for h in range(DSA_HEADS):
        dq_ref[0, h] = zdq[:, h * DSA_HEAD_DIM:(h + 1) * DSA_HEAD_DIM].astype(BF16)
        iq_ref[0, h] = ziq[:, h * IDX_DIM:(h + 1) * IDX_DIM].astype(BF16)

    zkv = dot(ub, wkv_ref[...])
    kvn = zkv * lax.rsqrt(jnp.mean(zkv * zkv, axis=-1, keepdims=True) + NORM_EPS) * kvg_ref[...]
    lat = dot(kvn.astype(BF16), wup_ref[...])
    dkt_ref[0, 0] = lat.T[:DSA_HEAD_DIM, :].astype(BF16)
    vp_ref[0] = jnp.where(lane < DSA_HEAD_DIM, pltpu.roll(lat, DSA_HEAD_DIM, 1), 1.0).astype(BF16)

    zik = dot(ub, wikw_ref[...])
    ikt_ref[0, 0] = zik.T[:IDX_DIM, :].astype(BF16)
    idx_scale = (IDX_DIM ** -0.5) * (IDX_HEADS ** -0.5)
    for h in range(IDX_HEADS):
        col = zik[:, IDX_DIM + h:IDX_DIM + h + 1] * idx_scale
        iwb_ref[0, :, h * LANES:(h + 1) * LANES] = jnp.broadcast_to(col, (tm, LANES))

    zg = jax.nn.sigmoid(dot(ub, wgate_ref[...]) + bgate_ref[...])
    d = zg.shape[1] // 2
    ga_ref[0] = zg[:, :d]
    gb_ref[0] = zg[:, d:]


def _proj(x, mod, g_mix, w_in, kv_g, w_up, w_gate, b_gate):
    bsz, s, d = x.shape
    tm = PROJ_TOKENS
    nt = s // tm
    offs = [0]
    for n in IN_SPLITS:
        offs.append(offs[-1] + n)
    seg = lambda i: w_in[:, offs[i]:offs[i + 1]]
    wqk = jnp.concatenate([seg(0), seg(1)], axis=1).astype(BF16)
    wvg = jnp.concatenate([seg(2), seg(3)], axis=1).astype(BF16)
    wdq = seg(4).astype(BF16)
    wkv = seg(5).astype(BF16)
    wiq = seg(6).astype(BF16)
    wikw = jnp.concatenate([seg(7), seg(8), jnp.zeros((d, LANES - IDX_DIM - IDX_HEADS), F32)],
                           axis=1).astype(BF16)

    pos = jnp.arange(s, dtype=jnp.int32)
    freqs = ROPE_BASE ** (-jnp.arange(0, RET_QK_DIM, 2, dtype=F32) / RET_QK_DIM)
    ang = pos.astype(F32)[:, None] * freqs[None, :]
    cos_h = jnp.concatenate([jnp.cos(ang), jnp.cos(ang)], axis=1)
    sin_h = jnp.concatenate([-jnp.sin(ang), jnp.sin(ang)], axis=1)
    cos_t = jnp.concatenate([cos_h, cos_h], axis=1)
    sin_t = jnp.concatenate([sin_h, sin_h], axis=1)

    tok = lambda n: pl.BlockSpec((1, tm, n), lambda b, i: (b, i, 0))
    heads = pl.BlockSpec((1, DSA_HEADS, tm, DSA_HEAD_DIM), lambda b, i: (b, 0, i, 0))
    trans = pl.BlockSpec((1, 1, DSA_HEAD_DIM, tm), lambda b, i: (b, i // 2, 0, i % 2))
    outs = pl.pallas_call(
        _proj_kernel,
        grid=(bsz, nt),
        in_specs=[tok(d),
                  pl.BlockSpec((1, 6, d), lambda b, i: (b, 0, 0)),
                  _const_spec((1, d)),
                  pl.BlockSpec((tm, LANES), lambda b, i: (i, 0)),
                  pl.BlockSpec((tm, LANES), lambda b, i: (i, 0)),
                  _const_spec(wqk.shape), _const_spec(wvg.shape), _const_spec(wdq.shape),
                  _const_spec(wiq.shape), _const_spec(wkv.shape), _const_spec(wikw.shape),
                  _const_spec((1, DSA_KV_LATENT)), _const_spec(w_up.shape),
                  _const_spec(w_gate.shape), _const_spec((1, w_gate.shape[1]))],
        out_specs=[heads, heads, tok(1024), tok(1024), heads, heads, tok(LANES), trans, trans,
                   tok(IDX_HEADS * LANES), tok(d), tok(d)],
        out_shape=[jax.ShapeDtypeStruct((bsz, RET_HEADS, s, RET_QK_DIM), BF16),
                   jax.ShapeDtypeStruct((bsz, RET_HEADS, s, RET_QK_DIM), BF16),
                   jax.ShapeDtypeStruct((bsz, s, RET_HEADS * RET_V_DIM), BF16),
                   jax.ShapeDtypeStruct((bsz, s, RET_HEADS * RET_V_DIM), F32),
                   jax.ShapeDtypeStruct((bsz, DSA_HEADS, s, DSA_HEAD_DIM), BF16),
                   jax.ShapeDtypeStruct((bsz, IDX_HEADS, s, IDX_DIM), BF16),
                   jax.ShapeDtypeStruct((bsz, s, LANES), BF16),
                   jax.ShapeDtypeStruct((bsz, nt // 2, DSA_HEAD_DIM, 2 * tm), BF16),
                   jax.ShapeDtypeStruct((bsz, nt // 2, IDX_DIM, 2 * tm), BF16),
                   jax.ShapeDtypeStruct((bsz, s, IDX_HEADS * LANES), F32),
                   jax.ShapeDtypeStruct((bsz, s, d), F32),
                   jax.ShapeDtypeStruct((bsz, s, d), F32)],
        compiler_params=_params(("arbitrary", "arbitrary")),
        name="in_proj",
    )(x, mod, g_mix.reshape(1, d), cos_t, sin_t, wqk, wvg, wdq, wiq, wkv, wikw,
      kv_g.reshape(1, -1), w_up.astype(BF16), w_gate.astype(BF16), b_gate.reshape(1, -1))
    return outs


def _ret_kernel(rq_ref, rk_ref, rv_ref, rg_ref, ga_ref, dmat_ref, qdec_ref, kdec_ref, cdec_ref,
                gng_ref, wout_ref, y_ref, state_ref, o_ref):
    dot = functools.partial(jnp.dot, preferred_element_type=F32)

    @pl.when(pl.program_id(1) == 0)
    def _():
        state_ref[...] = jnp.zeros_like(state_ref)

    for h in range(RET_HEADS):
        q = rq_ref[0, h]
        k = rk_ref[0, h]
        cols = slice(h * RET_V_DIM, (h + 1) * RET_V_DIM)
        v = rv_ref[0, :, cols]
        st = state_ref[h]
        s = lax.dot_general(q, k, (((1,), (1,)), ((), ())), preferred_element_type=F32)
        o = dot((s * dmat_ref[h]).astype(BF16), v)
        qd = (q.astype(F32) * qdec_ref[h]).astype(BF16)
        o = o + dot(qd, st.astype(BF16))
        kd = (k.astype(F32) * kdec_ref[h]).astype(BF16)
        kv = lax.dot_general(kd, v, (((0,), (0,)), ((), ())), preferred_element_type=F32)
        state_ref[h] = st * cdec_ref[h] + kv
        mu = jnp.mean(o, axis=-1, keepdims=True)
        oc = o - mu
        var = jnp.mean(oc * oc, axis=-1, keepdims=True)
        on = oc * lax.rsqrt(var + GN_EPS) * gng_ref[:, cols]
        o_ref[:, cols] = (rg_ref[0, :, cols] * on).astype(BF16)
    y_ref[0] = ga_ref[0] * dot(o_ref[...], wout_ref[...])


def _retention(rq, rk, rv, rg, ga, gn_g, w_out):
    bsz, s, nv = rv.shape
    d = w_out.shape[1]
    tr = RET_TOKENS
    log_gamma = jnp.log(1.0 - 2.0 ** (-5.0 - jnp.arange(RET_HEADS, dtype=F32)))
    idx = jnp.arange(tr, dtype=F32)
    ch = jnp.arange(tr, dtype=jnp.int32) // CHUNK
    dist = jnp.abs(idx[:, None] - idx[None, :])
    dmat = jnp.where((ch[None, :] <= ch[:, None])[None],
                     jnp.exp(log_gamma[:, None, None] * dist[None]), 0.0)
    qdec = jnp.broadcast_to(jnp.exp(log_gamma[:, None] * (idx + 1.0)[None, :])[:, :, None],
                            (RET_HEADS, tr, RET_QK_DIM))
    kdec = jnp.broadcast_to(jnp.exp(log_gamma[:, None] * (tr - 1.0 - idx)[None, :])[:, :, None],
                            (RET_HEADS, tr, RET_QK_DIM))
    cdec = jnp.broadcast_to(jnp.exp(log_gamma * tr)[:, None, None],
                            (RET_HEADS, RET_QK_DIM, RET_V_DIM))

    heads = pl.BlockSpec((1, RET_HEADS, tr, RET_QK_DIM), lambda b, i: (b, 0, i, 0))
    tok = lambda n: pl.BlockSpec((1, tr, n), lambda b, i: (b, i, 0))
    return pl.pallas_call(
        _ret_kernel,
        grid=(bsz, s // tr),
        in_specs=[heads, heads, tok(nv), tok(nv), tok(d),
                  _const_spec(dmat.shape), _const_spec(qdec.shape), _const_spec(kdec.shape),
                  _const_spec(cdec.shape), _const_spec((1, nv)), _const_spec(w_out.shape)],
        out_specs=tok(d),
        out_shape=jax.ShapeDtypeStruct((bsz, s, d), F32),
        scratch_shapes=[pltpu.VMEM((RET_HEADS, RET_QK_DIM, RET_V_DIM), F32),
                        pltpu.VMEM((tr, nv), BF16)],
        compiler_params=_params(("arbitrary", "arbitrary")),
        name="retention",
    )(rq, rk, rv, rg, ga, dmat, qdec, kdec, cdec, gn_g.reshape(1, nv), w_out.astype(BF16))


def _sortable(bits):
    return jnp.where(bits < 0, bits ^ 0x7FFFFFFF, bits)


def _midpoint(lo, hi):
    return (lo >> 1) + (hi >> 1) + (lo & hi & 1)


def _lanes(a, n):
    return jnp.concatenate([a] * (n // LANES), axis=1)


def _dsa_kernel(relb_ref, dq_ref, iq_ref, iwb_ref, gb_ref, ikt_ref, dkt_ref, vp_ref, wout_ref,
                y_ref, keys_ref, lo_ref, clo_ref, hi_ref, mid_ref, cnt_ref, strag_ref, smax_ref,
                smin_ref, bias_ref, m_ref, acc_ref, o_ref, *, topk):
    t = DSA_TILE
    w2 = DSA_SLOT
    dot = functools.partial(jnp.dot, preferred_element_type=F32)
    i = pl.program_id(1)
    odd = (i % 2) == 1
    last = i // 2
    n_slots = last + 1
    row = lax.broadcasted_iota(I32, (t, t), 0)
    col = lax.broadcasted_iota(I32, (t, t), 1)
    diag_adm = (col // CHUNK) <= (row // CHUNK)
    far = (N_BUCKETS // 2 - 1) * DSA_HEADS

    @pl.when((pl.program_id(0) == 0) & (i == 0))
    def _():
        for which in range(2):
            rel = col - row - (t if which == 0 else 0)
            n = jnp.abs(rel)
            large = jnp.full((t, t), 8, I32)
            for step in BUCKET_STEPS:
                large = large + (n >= step).astype(I32)
            bucket = jnp.where(n < 8, n, large) + jnp.where(rel > 0, N_BUCKETS // 2, 0)
            for h in range(DSA_HEADS):
                b = jnp.zeros((t, t), F32)
                for k in range(N_BUCKETS):
                    b = jnp.where(bucket == k, relb_ref[k * DSA_HEADS + h], b)
                bias_ref[h, which] = (b - relb_ref[far + h]) * LOG2E

    smax_ref[...] = jnp.full((t, LANES), -jnp.inf, F32)
    smin_ref[...] = jnp.full((t, LANES), jnp.inf, F32)

    def score_cols(kt, adm):
        n = kt.shape[1]
        s = dot(iq_ref[0].reshape(IDX_HEADS * t, IDX_DIM), kt).reshape(IDX_HEADS, t, n)
        acc = jnp.zeros((t, n), F32)
        for h in range(IDX_HEADS):
            acc = acc + jnp.maximum(s[h], 0.0) * _lanes(iwb_ref[0, :, h * LANES:(h + 1) * LANES], n)
        hi_src = acc if adm is None else jnp.where(adm, acc, -jnp.inf)
        lo_src = acc if adm is None else jnp.where(adm, acc, jnp.inf)
        mx = smax_ref[...]
        mn = smin_ref[...]
        for c0 in range(0, n, LANES):
            mx = jnp.maximum(mx, hi_src[:, c0:c0 + LANES])
            mn = jnp.minimum(mn, lo_src[:, c0:c0 + LANES])
        smax_ref[...] = mx
        smin_ref[...] = mn
        key = _sortable(pltpu.bitcast(acc, I32))
        return key if adm is None else jnp.where(adm, key, KEY_MIN)

    def score_body(slot, carry):
        keys_ref[slot] = score_cols(ikt_ref[0, slot], None)
        return carry

    lax.fori_loop(0, last, score_body, 0)

    @pl.when(odd)
    def _():
        keys_ref[last, :, :t] = score_cols(ikt_ref[0, last, :, :t], None)
        keys_ref[last, :, t:] = score_cols(ikt_ref[0, last, :, t:], diag_adm)

    @pl.when(jnp.logical_not(odd))
    def _():
        keys_ref[last, :, :t] = score_cols(ikt_ref[0, last, :, :t], diag_adm)
        keys_ref[last, :, t:] = jnp.full((t, t), KEY_MIN, I32)

    def spread(a):
        f = pltpu.bitcast(a[0:1, :], F32)
        return pltpu.bitcast(jnp.broadcast_to(f, (LANES, t)).T, I32)

    ones_row = jnp.ones((8, LANES), BF16)
    pos = i * t + lax.broadcasted_iota(I32, (8, t), 1)
    n_adm = (pos // CHUNK + 1) * CHUNK
    rmax = jnp.max(smax_ref[...].T, axis=0, keepdims=True)
    rmin = jnp.min(smin_ref[...].T, axis=0, keepdims=True)
    lo0 = jnp.broadcast_to(_sortable(pltpu.bitcast(rmin, I32)), (8, t))
    hi0 = jnp.broadcast_to(_sortable(pltpu.bitcast(rmax, I32)) + 1, (8, t))
    hi0 = jnp.where(n_adm <= topk, lo0 + 1, hi0)
    log_k = math.log(topk - 0.5)

    def unresolved(lo, hi, clo):
        return (clo != topk) & (_midpoint(lo, hi) > lo)

    def count_unresolved(lo, hi, clo):
        return jnp.sum(jnp.where(unresolved(lo, hi, clo)[0:1, :], 1.0, 0.0))

    def split_point(step, lo, hi, clo, chi):
        lof = pltpu.bitcast(_sortable(lo), F32)
        hif = pltpu.bitcast(_sortable(hi - 1), F32)
        la = jnp.log(clo.astype(F32))
        lb = jnp.log(jnp.maximum(chi.astype(F32), 0.5))
        frac = jnp.clip((la - log_k) / (la - lb), INTERP_CLIP, 1.0 - INTERP_CLIP)
        mid_i = _sortable(pltpu.bitcast(lof + (hif - lof) * frac, I32))
        mid = jnp.where(step % BISECT_EVERY == BISECT_EVERY - 1, _midpoint(lo, hi), mid_i)
        return jnp.minimum(jnp.maximum(mid, lo + 1), hi - 1)

    def select_cond(carry):
        return (carry[0] < MAX_SELECT_STEPS) & (carry[1] > STRAGGLERS + 0.5)

    def select_body(carry):
        step, _, mid, lo, hi, clo, chi = carry
        for s0 in range(0, t, SELECT_ROWS):
            rows = slice(s0, s0 + SELECT_ROWS)
            midr = mid_ref[rows, :]

            def count_body(slot, cnt):
                for c0 in range(0, w2, LANES):
                    cnt = cnt + (keys_ref[slot, rows, c0:c0 + LANES] >= midr).astype(I32)
                return cnt

            cnt_ref[rows, :] = lax.fori_loop(0, n_slots, count_body,
                                             jnp.zeros((SELECT_ROWS, LANES), I32))

        cnt_b = cnt_ref[...].astype(F32).astype(BF16)
        c = lax.dot_general(ones_row, cnt_b, (((1,), (1,)), ((), ())),
                            preferred_element_type=F32).astype(I32)
        active = unresolved(lo, hi, clo)
        up = active & (c >= topk)
        down = active & (c < topk)
        lo = jnp.where(up, mid, lo)
        hi = jnp.where(down, mid, hi)
        clo = jnp.where(up, c, clo)
        chi = jnp.where(down, c, chi)
        mid = split_point(step + 1, lo, hi, clo, chi)
        mid_ref[...] = spread(mid)
        return step + 1, count_unresolved(lo, hi, clo), mid, lo, hi, clo, chi

    chi0 = jnp.zeros((8, t), I32)
    mid0 = split_point(0, lo0, hi0, n_adm, chi0)
    mid_ref[...] = spread(mid0)
    sel = lax.while_loop(select_cond, select_body,
                         (jnp.int32(0), count_unresolved(lo0, hi0, n_adm), mid0, lo0, hi0, n_adm,
                          chi0))
    lo_ref[...] = spread(sel[3])
    hi_ref[...] = spread(sel[4])
    clo_ref[...] = spread(sel[5])

    def pick_rows(mask8):
        lane_f = lax.broadcasted_iota(I32, (8, t), 1).astype(F32)
        picks = []
        for _ in range(STRAGGLERS):
            first = jnp.min(jnp.where(mask8, lane_f, float(t)))
            mask8 = mask8 & (lane_f != first)
            r = first.astype(I32)
            picks.append((jnp.minimum(r, t - 1), r < t))
        return picks

    def gather(picks):
        def gather_body(slot, carry):
            for j, (r, _) in enumerate(picks):
                strag_ref[slot, j:j + 1, :] = keys_ref[slot, pl.ds(r, 1), :]
            return carry

        lax.fori_loop(0, n_slots, gather_body, 0)

    def rows_of(ref, picks):
        return jnp.concatenate([ref[pl.ds(r, 1), :] for r, _ in picks], axis=0)

    def real_rows(picks):
        return jnp.concatenate([jnp.broadcast_to(ok.astype(I32), (1, LANES)) for _, ok in picks],
                               axis=0) > 0

    def strip_sum(cnt):
        return jnp.broadcast_to(jnp.sum(cnt.astype(F32), axis=1, keepdims=True),
                                (STRAGGLERS, LANES)).astype(I32)

    @pl.when(sel[1] > 0.5)
    def _():
        picks = pick_rows(unresolved(sel[3], sel[4], sel[5]))
        gather(picks)
        lo8 = rows_of(lo_ref, picks)
        clo8 = rows_of(clo_ref, picks)
        hi8 = jnp.where(real_rows(picks), rows_of(hi_ref, picks), lo8 + 1)

        def open8(lo, hi, clo):
            return jnp.max(jnp.where(unresolved(lo, hi, clo), 1.0, 0.0))

        def finish_body(carry):
            _, lo, hi, clo = carry
            mid = _midpoint(lo, hi)

            def count_body(slot, c):
                cnt, above, below = c
                for c0 in range(0, w2, LANES):
                    k = strag_ref[slot, :, c0:c0 + LANES]
                    ge = k >= mid
                    cnt = cnt + ge.astype(I32)
                    above = jnp.minimum(above, jnp.where(ge, k, KEY_MAX))
                    below = jnp.maximum(below, jnp.where(ge, KEY_MIN, k))
                return cnt, above, below

            shape = (STRAGGLERS, LANES)
            cnt, above, below = lax.fori_loop(
                0, n_slots, count_body,
                (jnp.zeros(shape, I32), jnp.full(shape, KEY_MAX, I32), jnp.full(shape, KEY_MIN, I32)))
            c = strip_sum(cnt)
            above = jnp.broadcast_to(jnp.min(above, axis=1, keepdims=True), shape)
            below = jnp.broadcast_to(jnp.max(below, axis=1, keepdims=True), shape)
            active = unresolved(lo, hi, clo)
            up = active & (c >= topk)
            lo = jnp.where(up, above, lo)
            hi = jnp.where(active & (c < topk), below + 1, hi)
            clo = jnp.where(up, c, clo)
            return open8(lo, hi, clo), lo, hi, clo

        _, lo8, _, clo8 = lax.while_loop(lambda carry: carry[0] > 0.5, finish_body,
                                         (open8(lo8, hi8, clo8), lo8, hi8, clo8))
        for j, (r, ok) in enumerate(picks):
            @pl.when(ok)
            def _():
                lo_ref[pl.ds(r, 1), :] = lo8[j:j + 1, :]
                clo_ref[pl.ds(r, 1), :] = clo8[j:j + 1, :]

    def count_ties():
        return jnp.max(jnp.where(clo_ref[...] > topk, 1.0, 0.0))

    def tie_round(_):
        tied8 = jnp.where(clo_ref[...] > topk, 1.0, 0.0).T[0:8, :] > 0.5
        picks = pick_rows(tied8)
        gather(picks)
        real = _lanes(real_rows(picks).astype(I32), w2) > 0
        thr = _lanes(rows_of(lo_ref, picks), w2)
        col2 = lax.broadcasted_iota(I32, (STRAGGLERS, w2), 1)

        def lane_fold(x):
            return sum(x[:, c0:c0 + LANES] for c0 in range(0, w2, LANES))

        def gt_body(slot, cnt):
            return cnt + lane_fold((strag_ref[slot] > thr).astype(I32))

        shape = (STRAGGLERS, LANES)
        need = topk - strip_sum(lax.fori_loop(0, n_slots, gt_body, jnp.zeros(shape, I32)))

        def cut_body(_, carry):
            jlo, jhi = carry
            jm = _lanes((jlo + jhi) >> 1, w2)

            def eq_body(slot, cnt):
                hit = (strag_ref[slot] == thr) & ((col2 + slot * w2) < jm)
                return cnt + lane_fold(hit.astype(I32))

            c = strip_sum(lax.fori_loop(0, n_slots, eq_body, jnp.zeros(shape, I32)))
            ok = c >= need
            mid = (jlo + jhi) >> 1
            return jnp.where(ok, jlo, mid), jnp.where(ok, mid, jhi)

        n_steps = (keys_ref.shape[0] * w2).bit_length() + 1
        _, cut = lax.fori_loop(0, n_steps, cut_body,
                               (jnp.zeros(shape, I32), jnp.full(shape, 1, I32) * (n_slots * w2)))
        cut = _lanes(cut, w2)

        def drop_body(slot, carry):
            key = strag_ref[slot]
            drop = real & (key == thr) & ((col2 + slot * w2) >= cut)
            strag_ref[slot] = jnp.where(drop, KEY_MIN, key)
            for j, (r, ok) in enumerate(picks):
                @pl.when(ok)
                def _():
                    keys_ref[slot, pl.ds(r, 1), :] = strag_ref[slot, j:j + 1, :]
            return carry

        lax.fori_loop(0, n_slots, drop_body, 0)
        for j, (r, ok) in enumerate(picks):
            @pl.when(ok)
            def _():
                clo_ref[pl.ds(r, 1), :] = jnp.full((1, LANES), topk, I32)
        return count_ties()

    lax.while_loop(lambda n: n > 0.5, tie_round, count_ties())

    m_ref[...] = jnp.full(m_ref.shape, NEG_BIG, F32)
    acc_ref[...] = jnp.zeros(acc_ref.shape, F32)

    def attend(slot, biases):
        n = len(biases) * t
        nh = DSA_HEADS
        thr = _lanes(lo_ref[...], n)
        maskb = jnp.where(keys_ref[slot, :, :n] >= thr, 0.0, NEG_BIG)
        v = vp_ref[0, pl.ds(pl.multiple_of(slot * w2, w2), n), :]
        q = dq_ref[0].reshape(nh * t, DSA_HEAD_DIM)
        lg = dot(q, dkt_ref[0, slot, :, :n]).reshape(nh, t, n) + maskb[None]
        if any(which is not None for which in biases):
            lg = jnp.concatenate(
                [lg[:, :, k * t:(k + 1) * t] if which is None
                 else lg[:, :, k * t:(k + 1) * t] + bias_ref[:, which]
                 for k, which in enumerate(biases)], axis=2)
        m_old = m_ref[...]
        m_new = jnp.maximum(m_old, jnp.broadcast_to(jnp.max(lg, axis=2, keepdims=True),
                                                    (nh, t, LANES)))
        p = jnp.exp2(lg - jnp.concatenate([m_new] * (n // LANES), axis=2))
        pv = dot(p.astype(BF16).reshape(nh * t, n), v).reshape(nh, t, LANES)
        acc_ref[...] = acc_ref[...] * jnp.exp2(m_old - m_new) + pv
        m_ref[...] = m_new

    def far_body(slot, carry):
        attend(slot, (None, None))
        return carry

    lax.fori_loop(0, jnp.where(odd, last, last - 1), far_body, 0)

    @pl.when(odd)
    def _():
        attend(last, (0, 1))

    @pl.when(jnp.logical_not(odd) & (i >= 2))
    def _():
        attend(last - 1, (None, 0))

    @pl.when(jnp.logical_not(odd))
    def _():
        attend(last, (1,))

    lane = lax.broadcasted_iota(I32, (t, LANES), 1)
    for pair in range(DSA_HEADS // 2):
        a0 = acc_ref[2 * pair]
        a1 = acc_ref[2 * pair + 1]
        even = a0 / pltpu.roll(a0, DSA_HEAD_DIM, 1)
        odd_h = pltpu.roll(a1, DSA_HEAD_DIM, 1) / a1
        o_ref[:, pair * LANES:(pair + 1) * LANES] = jnp.where(lane < DSA_HEAD_DIM, even,
                                                              odd_h).astype(BF16)
    y_ref[0] = gb_ref[0] * dot(o_ref[...], wout_ref[...])


def _dsa(dq, iq, iwb, gb, ikt, dkt, vp, rel_bias, w_out):
    bsz, _, s, _ = dq.shape
    d = w_out.shape[1]
    t = DSA_TILE
    nq = s // t
    n_slots = nq // 2
    topk = min(IDX_TOPK, s // 4)
    heads = pl.BlockSpec((1, DSA_HEADS, t, DSA_HEAD_DIM), lambda b, i: (b, 0, i, 0))
    tok = lambda n: pl.BlockSpec((1, t, n), lambda b, i: (b, i, 0))
    per_batch = lambda shape: pl.BlockSpec((1,) + shape, lambda b, i: (b,) + (0,) * len(shape),
                                           pipeline_mode=pl.Buffered(1))
    row_state = lambda dt: pltpu.VMEM((t, LANES), dt)
    return pl.pallas_call(
        functools.partial(_dsa_kernel, topk=topk),
        grid=(bsz, nq),
        in_specs=[pl.BlockSpec(memory_space=pltpu.SMEM),
                  heads, heads, tok(IDX_HEADS * LANES), tok(d),
                  per_batch((n_slots, IDX_DIM, DSA_SLOT)), per_batch((n_slots, DSA_HEAD_DIM, DSA_SLOT)),
                  per_batch((s, LANES)), _const_spec(w_out.shape)],
        out_specs=tok(d),
        out_shape=jax.ShapeDtypeStruct((bsz, s, d), F32),
        scratch_shapes=[pltpu.VMEM((n_slots, t, DSA_SLOT), I32),
                        row_state(I32), row_state(I32), row_state(I32),
                        row_state(I32), row_state(I32),
                        pltpu.VMEM((n_slots, STRAGGLERS, DSA_SLOT), I32),
                        row_state(F32), row_state(F32),
                        pltpu.VMEM((DSA_HEADS, 2, t, t), F32),
                        pltpu.VMEM((DSA_HEADS, t, LANES), F32),
                        pltpu.VMEM((DSA_HEADS, t, LANES), F32),
                        pltpu.VMEM((t, DSA_HEADS * DSA_HEAD_DIM), BF16)],
        compiler_params=_params(("arbitrary", "arbitrary")),
        name="dsa",
    )(rel_bias.reshape(-1), dq, iq, iwb, gb, ikt, dkt, vp, w_out.astype(BF16))


def _moe_kernel(x_ref, yr_ref, yd_ref, mod_ref, wo_ref, gffn_ref, wr_ref, br_ref, w1_ref, w3_ref,
                w2_ref, gfin_ref, out_ref, gate_ref):
    dot = functools.partial(jnp.dot, preferred_element_type=F32)
    tm = x_ref.shape[1]
    mix = dot((yr_ref[0] + yd_ref[0]).astype(BF16), wo_ref[...])
    h1 = x_ref[0] + mod_ref[0, 2:3, :] * mix
    hn = h1 * lax.rsqrt(jnp.mean(h1 * h1, axis=-1, keepdims=True) + NORM_EPS) * gffn_ref[...]
    u2 = hn * (1.0 + mod_ref[0, 4:5, :]) + mod_ref[0, 3:4, :]

    logits = _dot3(u2, wr_ref[...]) + br_ref[...]
    lane = lax.broadcasted_iota(I32, (tm, LANES), 1)
    big = jnp.int32(LANES)
    rmax = lambda a: jnp.max(a, axis=1, keepdims=True)
    rmin = lambda a: jnp.min(a, axis=1, keepdims=True)
    is_g = lane < N_GROUPS
    gl = jnp.where(is_g, logits, -jnp.inf)
    gmax = rmax(gl)
    gsel = rmin(jnp.where(is_g & (gl == gmax), lane, big))
    gp = 1.0 / jnp.sum(jnp.where(is_g, jnp.exp(gl - gmax), 0.0), axis=1, keepdims=True)
    e_lane = lane - N_GROUPS
    in_grp = (e_lane >= 0) & (e_lane < N_EXPERTS) & ((e_lane // EXPERTS_PER_GROUP) == gsel)
    el = jnp.where(in_grp, logits, -jnp.inf)
    v1 = rmax(el)
    i1 = rmin(jnp.where(in_grp & (el == v1), lane, big))
    el2 = jnp.where(lane == i1, -jnp.inf, el)
    v2 = rmax(el2)
    i2 = rmin(jnp.where(in_grp & (lane != i1) & (el2 == v2), lane, big))
    e2 = jnp.exp(v2 - v1)
    den = 1.0 + e2
    w1 = gp * (1.0 / den)
    w2 = gp * (e2 / den)
    for e in range(N_EXPERTS):
        ge = jnp.where(i1 == e + N_GROUPS, w1, 0.0) + jnp.where(i2 == e + N_GROUPS, w2, 0.0)
        gate_ref[e] = jnp.broadcast_to(ge, (tm, LANES))

    u2b = u2.astype(BF16)
    y = jnp.zeros_like(h1)
    for e in range(N_EXPERTS):
        a = dot(u2b, w1_ref[e])
        b = dot(u2b, w3_ref[e])
        g = gate_ref[e]
        hid = a * jax.nn.sigmoid(a) * b * jnp.concatenate([g] * (EXPERT_FF // LANES), axis=1)
        y = y + dot(hid.astype(BF16), w2_ref[e])
    h2 = h1 + mod_ref[0, 5:6, :] * y
    out_ref[0] = h2 * lax.rsqrt(jnp.mean(h2 * h2, axis=-1, keepdims=True) + NORM_EPS) * gfin_ref[...]


def _merge_moe(x, y_ret, y_dsa, mod, w_o, g_ffn, w_gr, b_gr, w_er, b_er, w1, w3, w2, g_fin):
    bsz, s, d = x.shape
    tm = MOE_TOKENS
    pad = LANES - N_GROUPS - N_EXPERTS
    wr = jnp.concatenate([w_gr, w_er, jnp.zeros((d, pad), F32)], axis=1)
    br = jnp.concatenate([b_gr, b_er, jnp.zeros((pad,), F32)]).reshape(1, LANES)
    tok = pl.BlockSpec((1, tm, d), lambda b, i: (b, i, 0))
    return pl.pallas_call(
        _moe_kernel,
        grid=(bsz, s // tm),
        in_specs=[tok, tok, tok,
                  pl.BlockSpec((1, 6, d), lambda b, i: (b, 0, 0)),
                  _const_spec(w_o.shape), _const_spec((1, d)), _const_spec(wr.shape),
                  _const_spec((1, LANES)), _const_spec(w1.shape), _const_spec(w3.shape),
                  _const_spec(w2.shape), _const_spec((1, d))],
        out_specs=tok,
        out_shape=jax.ShapeDtypeStruct((bsz, s, d), F32),
        scratch_shapes=[pltpu.VMEM((N_EXPERTS, tm, LANES), F32)],
        compiler_params=_params(("arbitrary", "arbitrary")),
        name="merge_moe",
    )(x, y_ret, y_dsa, mod, w_o.astype(BF16), g_ffn.reshape(1, d), wr, br, w1.astype(BF16),
      w3.astype(BF16), w2.astype(BF16), g_fin.reshape(1, d))


def kernel(x, c, w_ada, b_ada, norm_mix_g, w_in, ret_gn_g, dsa_kv_norm_g, w_dsa_kv_up, rel_bias,
           w_ret_out, w_dsa_out, w_gate, b_gate, w_o, norm_ffn_g, w_group_router, b_group_router,
           w_expert_router, b_expert_router, w_exp_gate, w_exp_up, w_exp_down, norm_final_g):
    assert w_ada.shape[0] == 1, "single-layer block"
    bsz, s, d = x.shape
    assert s % DSA_SLOT == 0 and DSA_TILE == PROJ_TOKENS
    mod = _ada(c, w_ada[0], b_ada[0]).reshape(bsz, 6, d)
    (rq, rk, rv, rg, dq, iq, vp, dkt, ikt, iwb, ga, gb) = _proj(
        x, mod, norm_mix_g[0], w_in[0], dsa_kv_norm_g[0], w_dsa_kv_up[0], w_gate[0], b_gate[0])
    y_ret = _retention(rq, rk, rv, rg, ga, ret_gn_g[0], w_ret_out[0])
    y_dsa = _dsa(dq, iq, iwb, gb, ikt, dkt, vp, rel_bias, w_dsa_out[0])
    return _merge_moe(x, y_ret, y_dsa, mod, w_o[0], norm_ffn_g[0], w_group_router[0],
                      b_group_router[0], w_expert_router[0], b_expert_router[0], w_exp_gate[0],
                      w_exp_up[0], w_exp_down[0], norm_final_g)
```

```python
import functools

import jax
import jax.numpy as jnp
from jax import lax
from jax.experimental import pallas as pl
from jax.experimental.pallas import tpu as pltpu

CHUNK = 64
RET_HEADS = 8
RET_QK_DIM = 64
RET_V_DIM = 128
DSA_HEADS = 8
DSA_HEAD_DIM = 64
DSA_KV_LATENT = 128
IDX_HEADS = 8
IDX_DIM = 64
IDX_TOPK = 256
N_BUCKETS = 32
N_GROUPS = 4
EXPERTS_PER_GROUP = 4
N_EXPERTS = N_GROUPS * EXPERTS_PER_GROUP
EXPERT_FF = 256
ROPE_BASE = 10000.0
NORM_EPS = 1e-6
GN_EPS = 1e-5
IN_SPLITS = (512, 512, 1024, 1024, 512, 128, 512, 64, 8)

LANES = 128
VMEM_LIMIT_BYTES = 56 * 1024 * 1024

PROJ_TOKENS = 256
RET_TOKENS = 256
DSA_TILE = 256
DSA_SLOT = 2 * DSA_TILE
SELECT_ROWS = 128
MOE_TOKENS = 256

INTERP_CLIP = 0.02
BISECT_EVERY = 4
MAX_SELECT_STEPS = 128
STRAGGLERS = 8

LOG2E = 1.4426950408889634
NEG_BIG = -1e30
KEY_MIN = -2147483648
KEY_MAX = 2147483647
BUCKET_STEPS = (12, 16, 23, 32, 46, 64, 91)

F32 = jnp.float32
BF16 = jnp.bfloat16
I32 = jnp.int32


def _const_spec(shape):
    nd = len(shape)
    return pl.BlockSpec(shape, lambda *_: (0,) * nd, pipeline_mode=pl.Buffered(1))


def _params(sem):
    return pltpu.CompilerParams(dimension_semantics=sem, vmem_limit_bytes=VMEM_LIMIT_BYTES)


def _split3(a):
    hi = a.astype(BF16)
    lo = (a - hi.astype(F32)).astype(BF16)
    return hi, lo


def _dot3(a, w):
    ah, al = _split3(a)
    wh, wl = _split3(w)
    d = functools.partial(jnp.dot, preferred_element_type=F32)
    return d(ah, wh) + (d(ah, wl) + d(al, wh))


def _ada_kernel(c_ref, w_ref, b_ref, o_ref):
    c = c_ref[...]
    o_ref[...] = _dot3(c * jax.nn.sigmoid(c), w_ref[...]) + b_ref[...]


def _ada(c, w, b):
    bsz, d = c.shape
    n = w.shape[1]
    rows = 8
    cp = jnp.zeros((rows, d), F32).at[:bsz].set(c)
    tn = 1536
    out = pl.pallas_call(
        _ada_kernel,
        grid=(n // tn,),
        in_specs=[pl.BlockSpec((rows, d), lambda j: (0, 0)),
                  pl.BlockSpec((d, tn), lambda j: (0, j)),
                  pl.BlockSpec((1, tn), lambda j: (0, j))],
        out_specs=pl.BlockSpec((rows, tn), lambda j: (0, j)),
        out_shape=jax.ShapeDtypeStruct((rows, n), F32),
        compiler_params=_params(("arbitrary",)),
        name="adaln",
    )(cp, w, b.reshape(1, n))
    return out[:bsz]


def _proj_kernel(x_ref, mod_ref, g_ref, cos_ref, sin_ref, wqk_ref, wvg_ref, wdq_ref, wiq_ref,
                 wkv_ref, wikw_ref, kvg_ref, wup_ref, wgate_ref, bgate_ref,
                 rq_ref, rk_ref, rv_ref, rg_ref, dq_ref, iq_ref, vp_ref, dkt_ref, ikt_ref,
                 iwb_ref, ga_ref, gb_ref):
    dot = functools.partial(jnp.dot, preferred_element_type=F32)
    x = x_ref[0]
    tm = x.shape[0]
    xn = x * lax.rsqrt(jnp.mean(x * x, axis=-1, keepdims=True) + NORM_EPS) * g_ref[...]
    u = xn * (1.0 + mod_ref[0, 1:2, :]) + mod_ref[0, 0:1, :]
    ub = u.astype(BF16)

    lane = lax.broadcasted_iota(I32, (tm, LANES), 1)
    first_half = (lane % RET_QK_DIM) < (RET_QK_DIM // 2)
    cos = cos_ref[...]
    sin = sin_ref[...]
    zqk = dot(ub, wqk_ref[...])
    for which, out_ref, scale in ((0, rq_ref, 1.0), (1, rk_ref, RET_QK_DIM ** -0.5)):
        for pair in range(RET_HEADS // 2):
            c0 = which * RET_HEADS * RET_QK_DIM + pair * LANES
            z = zqk[:, c0:c0 + LANES]
            rot = jnp.where(first_half, pltpu.roll(z, LANES - 32, 1), pltpu.roll(z, 32, 1))
            r = (z * cos + rot * sin) * scale
            out_ref[0, 2 * pair] = r[:, :RET_QK_DIM].astype(BF16)
            out_ref[0, 2 * pair + 1] = r[:, RET_QK_DIM:].astype(BF16)

    zvg = dot(ub, wvg_ref[...])
    nv = RET_HEADS * RET_V_DIM
    rv_ref[0] = zvg[:, :nv].astype(BF16)
    g = zvg[:, nv:]
    rg_ref[0] = g * jax.nn.sigmoid(g)

    zdq = dot(ub, wdq_ref[...]) * (DSA_HEAD_DIM ** -0.5 * LOG2E)
    ziq = dot(ub, wiq_ref[...])
    for h in range(DSA_HEADS):
        dq_ref[0, h] = zdq[:, h * DSA_HEAD_DIM:(h + 1) * DSA_HEAD_DIM].astype(BF16)
        iq_ref[0, h] = ziq[:, h * IDX_DIM:(h + 1) * IDX_DIM].astype(BF16)

    zkv = dot(ub, wkv_ref[...])
    kvn = zkv * lax.rsqrt(jnp.mean(zkv * zkv, axis=-1, keepdims=True) + NORM_EPS) * kvg_ref[...]
    lat = dot(kvn.astype(BF16), wup_ref[...])
    dkt_ref[0, 0] = lat.T[:DSA_HEAD_DIM, :].astype(BF16)
    vp_ref[0] = jnp.where(lane < DSA_HEAD_DIM, pltpu.roll(lat, DSA_HEAD_DIM, 1), 1.0).astype(BF16)

    zik = dot(ub, wikw_ref[...])
    ikt_ref[0, 0] = zik.T[:IDX_DIM, :].astype(BF16)
    idx_scale = (IDX_DIM ** -0.5) * (IDX_HEADS ** -0.5)
    for h in range(IDX_HEADS):
        col = zik[:, IDX_DIM + h:IDX_DIM + h + 1] * idx_scale
        iwb_ref[0, :, h * LANES:(h + 1) * LANES] = jnp.broadcast_to(col, (tm, LANES))

    zg = jax.nn.sigmoid(dot(ub, wgate_ref[...]) + bgate_ref[...])
    d = zg.shape[1] // 2
    ga_ref[0] = zg[:, :d]
    gb_ref[0] = zg[:, d:]


def _proj(x, mod, g_mix, w_in, kv_g, w_up, w_gate, b_gate):
    bsz, s, d = x.shape
    tm = PROJ_TOKENS
    nt = s // tm
    offs = [0]
    for n in IN_SPLITS:
        offs.append(offs[-1] + n)
    seg = lambda i: w_in[:, offs[i]:offs[i + 1]]
    wqk = jnp.concatenate([seg(0), seg(1)], axis=1).astype(BF16)
    wvg = jnp.concatenate([seg(2), seg(3)], axis=1).astype(BF16)
    wdq = seg(4).astype(BF16)
    wkv = seg(5).astype(BF16)
    wiq = seg(6).astype(BF16)
    wikw = jnp.concatenate([seg(7), seg(8), jnp.zeros((d, LANES - IDX_DIM - IDX_HEADS), F32)],
                           axis=1).astype(BF16)

    pos = jnp.arange(s, dtype=jnp.int32)
    freqs = ROPE_BASE ** (-jnp.arange(0, RET_QK_DIM, 2, dtype=F32) / RET_QK_DIM)
    ang = pos.astype(F32)[:, None] * freqs[None, :]
    cos_h = jnp.concatenate([jnp.cos(ang), jnp.cos(ang)], axis=1)
    sin_h = jnp.concatenate([-jnp.sin(ang), jnp.sin(ang)], axis=1)
    cos_t = jnp.concatenate([cos_h, cos_h], axis=1)
    sin_t = jnp.concatenate([sin_h, sin_h], axis=1)

    tok = lambda n: pl.BlockSpec((1, tm, n), lambda b, i: (b, i, 0))
    heads = pl.BlockSpec((1, DSA_HEADS, tm, DSA_HEAD_DIM), lambda b, i: (b, 0, i, 0))
    trans = pl.BlockSpec((1, 1, DSA_HEAD_DIM, tm), lambda b, i: (b, i // 2, 0, i % 2))
    outs = pl.pallas_call(
        _proj_kernel,
        grid=(bsz, nt),
        in_specs=[tok(d),
                  pl.BlockSpec((1, 6, d), lambda b, i: (b, 0, 0)),
                  _const_spec((1, d)),
                  pl.BlockSpec((tm, LANES), lambda b, i: (i, 0)),
                  pl.BlockSpec((tm, LANES), lambda b, i: (i, 0)),
                  _const_spec(wqk.shape), _const_spec(wvg.shape), _const_spec(wdq.shape),
                  _const_spec(wiq.shape), _const_spec(wkv.shape), _const_spec(wikw.shape),
                  _const_spec((1, DSA_KV_LATENT)), _const_spec(w_up.shape),
                  _const_spec(w_gate.shape), _const_spec((1, w_gate.shape[1]))],
        out_specs=[heads, heads, tok(1024), tok(1024), heads, heads, tok(LANES), trans, trans,
                   tok(IDX_HEADS * LANES), tok(d), tok(d)],
        out_shape=[jax.ShapeDtypeStruct((bsz, RET_HEADS, s, RET_QK_DIM), BF16),
                   jax.ShapeDtypeStruct((bsz, RET_HEADS, s, RET_QK_DIM), BF16),
                   jax.ShapeDtypeStruct((bsz, s, RET_HEADS * RET_V_DIM), BF16),
                   jax.ShapeDtypeStruct((bsz, s, RET_HEADS * RET_V_DIM), F32),
                   jax.ShapeDtypeStruct((bsz, DSA_HEADS, s, DSA_HEAD_DIM), BF16),
                   jax.ShapeDtypeStruct((bsz, IDX_HEADS, s, IDX_DIM), BF16),
                   jax.ShapeDtypeStruct((bsz, s, LANES), BF16),
                   jax.ShapeDtypeStruct((bsz, nt // 2, DSA_HEAD_DIM, 2 * tm), BF16),
                   jax.ShapeDtypeStruct((bsz, nt // 2, IDX_DIM, 2 * tm), BF16),
                   jax.ShapeDtypeStruct((bsz, s, IDX_HEADS * LANES), F32),
                   jax.ShapeDtypeStruct((bsz, s, d), F32),
                   jax.ShapeDtypeStruct((bsz, s, d), F32)],
        compiler_params=_params(("arbitrary", "arbitrary")),
        name="in_proj",
    )(x, mod, g_mix.reshape(1, d), cos_t, sin_t, wqk, wvg, wdq, wiq, wkv, wikw,
      kv_g.reshape(1, -1), w_up.astype(BF16), w_gate.astype(BF16), b_gate.reshape(1, -1))
    return outs


def _ret_kernel(rq_ref, rk_ref, rv_ref, rg_ref, ga_ref, dmat_ref, qdec_ref, kdec_ref, cdec_ref,
                gng_ref, wout_ref, y_ref, state_ref, o_ref):
    dot = functools.partial(jnp.dot, preferred_element_type=F32)

    @pl.when(pl.program_id(1) == 0)
    def _():
        state_ref[...] = jnp.zeros_like(state_ref)

    for h in range(RET_HEADS):
        q = rq_ref[0, h]
        k = rk_ref[0, h]
        cols = slice(h * RET_V_DIM, (h + 1) * RET_V_DIM)
        v = rv_ref[0, :, cols]
        st = state_ref[h]
        s = lax.dot_general(q, k, (((1,), (1,)), ((), ())), preferred_element_type=F32)
        o = dot((s * dmat_ref[h]).astype(BF16), v)
        qd = (q.astype(F32) * qdec_ref[h]).astype(BF16)
        o = o + dot(qd, st.astype(BF16))
        kd = (k.astype(F32) * kdec_ref[h]).astype(BF16)
        kv = lax.dot_general(kd, v, (((0,), (0,)), ((), ())), preferred_element_type=F32)
        state_ref[h] = st * cdec_ref[h] + kv
        mu = jnp.mean(o, axis=-1, keepdims=True)
        oc = o - mu
        var = jnp.mean(oc * oc, axis=-1, keepdims=True)
        on = oc * lax.rsqrt(var + GN_EPS) * gng_ref[:, cols]
        o_ref[:, cols] = (rg_ref[0, :, cols] * on).astype(BF16)
    y_ref[0] = ga_ref[0] * dot(o_ref[...], wout_ref[...])


def _retention(rq, rk, rv, rg, ga, gn_g, w_out):
    bsz, s, nv = rv.shape
    d = w_out.shape[1]
    tr = RET_TOKENS
    log_gamma = jnp.log(1.0 - 2.0 ** (-5.0 - jnp.arange(RET_HEADS, dtype=F32)))
    idx = jnp.arange(tr, dtype=F32)
    ch = jnp.arange(tr, dtype=jnp.int32) // CHUNK
    dist = jnp.abs(idx[:, None] - idx[None, :])
    dmat = jnp.where((ch[None, :] <= ch[:, None])[None],
                     jnp.exp(log_gamma[:, None, None] * dist[None]), 0.0)
    qdec = jnp.broadcast_to(jnp.exp(log_gamma[:, None] * (idx + 1.0)[None, :])[:, :, None],
                            (RET_HEADS, tr, RET_QK_DIM))
    kdec = jnp.broadcast_to(jnp.exp(log_gamma[:, None] * (tr - 1.0 - idx)[None, :])[:, :, None],
                            (RET_HEADS, tr, RET_QK_DIM))
    cdec = jnp.broadcast_to(jnp.exp(log_gamma * tr)[:, None, None],
                            (RET_HEADS, RET_QK_DIM, RET_V_DIM))

    heads = pl.BlockSpec((1, RET_HEADS, tr, RET_QK_DIM), lambda b, i: (b, 0, i, 0))
    tok = lambda n: pl.BlockSpec((1, tr, n), lambda b, i: (b, i, 0))
    return pl.pallas_call(
        _ret_kernel,
        grid=(bsz, s // tr),
        in_specs=[heads, heads, tok(nv), tok(nv), tok(d),
                  _const_spec(dmat.shape), _const_spec(qdec.shape), _const_spec(kdec.shape),
                  _const_spec(cdec.shape), _const_spec((1, nv)), _const_spec(w_out.shape)],
        out_specs=tok(d),
        out_shape=jax.ShapeDtypeStruct((bsz, s, d), F32),
        scratch_shapes=[pltpu.VMEM((RET_HEADS, RET_QK_DIM, RET_V_DIM), F32),
                        pltpu.VMEM((tr, nv), BF16)],
        compiler_params=_params(("arbitrary", "arbitrary")),
        name="retention",
    )(rq, rk, rv, rg, ga, dmat, qdec, kdec, cdec, gn_g.reshape(1, nv), w_out.astype(BF16))


def _sortable(bits):
    return jnp.where(bits < 0, bits ^ 0x7FFFFFFF, bits)


def _midpoint(lo, hi):
    return (lo >> 1) + (hi >> 1) + (lo & hi & 1)


def _lanes(a, n):
    return jnp.concatenate([a] * (n // LANES), axis=1)


def _dsa_kernel(relb_ref, dq_ref, iq_ref, iwb_ref, gb_ref, ikt_ref, dkt_ref, vp_ref, wout_ref,
                y_ref, keys_ref, lo_ref, clo_ref, hi_ref, mid_ref, cnt_ref, strag_ref, smax_ref,
                smin_ref, bias_ref, m_ref, acc_ref, o_ref, *, topk):
    t = DSA_TILE
    w2 = DSA_SLOT
    dot = functools.partial(jnp.dot, preferred_element_type=F32)
    i = pl.program_id(1)
    odd = (i % 2) == 1
    last = i // 2
    n_slots = last + 1
    row = lax.broadcasted_iota(I32, (t, t), 0)
    col = lax.broadcasted_iota(I32, (t, t), 1)
    diag_adm = (col // CHUNK) <= (row // CHUNK)
    far = (N_BUCKETS // 2 - 1) * DSA_HEADS

    @pl.when((pl.program_id(0) == 0) & (i == 0))
    def _():
        for which in range(2):
            rel = col - row - (t if which == 0 else 0)
            n = jnp.abs(rel)
            large = jnp.full((t, t), 8, I32)
            for step in BUCKET_STEPS:
                large = large + (n >= step).astype(I32)
            bucket = jnp.where(n < 8, n, large) + jnp.where(rel > 0, N_BUCKETS // 2, 0)
            for h in range(DSA_HEADS):
                b = jnp.zeros((t, t), F32)
                for k in range(N_BUCKETS):
                    b = jnp.where(bucket == k, relb_ref[k * DSA_HEADS + h], b)
                bias_ref[h, which] = (b - relb_ref[far + h]) * LOG2E

    smax_ref[...] = jnp.full((t, LANES), -jnp.inf, F32)
    smin_ref[...] = jnp.full((t, LANES), jnp.inf, F32)

    def score_cols(kt, adm):
        n = kt.shape[1]
        s = dot(iq_ref[0].reshape(IDX_HEADS * t, IDX_DIM), kt).reshape(IDX_HEADS, t, n)
        acc = jnp.zeros((t, n), F32)
        for h in range(IDX_HEADS):
            acc = acc + jnp.maximum(s[h], 0.0) * _lanes(iwb_ref[0, :, h * LANES:(h + 1) * LANES], n)
        hi_src = acc if adm is None else jnp.where(adm, acc, -jnp.inf)
        lo_src = acc if adm is None else jnp.where(adm, acc, jnp.inf)
        mx = smax_ref[...]
        mn = smin_ref[...]
        for c0 in range(0, n, LANES):
            mx = jnp.maximum(mx, hi_src[:, c0:c0 + LANES])
            mn = jnp.minimum(mn, lo_src[:, c0:c0 + LANES])
        smax_ref[...] = mx
        smin_ref[...] = mn
        key = _sortable(pltpu.bitcast(acc, I32))
        return key if adm is None else jnp.where(adm, key, KEY_MIN)

    def score_body(slot, carry):
        keys_ref[slot] = score_cols(ikt_ref[0, slot], None)
        return carry

    lax.fori_loop(0, last, score_body, 0)

    @pl.when(odd)
    def _():
        keys_ref[last, :, :t] = score_cols(ikt_ref[0, last, :, :t], None)
        keys_ref[last, :, t:] = score_cols(ikt_ref[0, last, :, t:], diag_adm)

    @pl.when(jnp.logical_not(odd))
    def _():
        keys_ref[last, :, :t] = score_cols(ikt_ref[0, last, :, :t], diag_adm)
        keys_ref[last, :, t:] = jnp.full((t, t), KEY_MIN, I32)

    def spread(a):
        f = pltpu.bitcast(a[0:1, :], F32)
        return pltpu.bitcast(jnp.broadcast_to(f, (LANES, t)).T, I32)

    ones_row = jnp.ones((8, LANES), BF16)
    pos = i * t + lax.broadcasted_iota(I32, (8, t), 1)
    n_adm = (pos // CHUNK + 1) * CHUNK
    rmax = jnp.max(smax_ref[...].T, axis=0, keepdims=True)
    rmin = jnp.min(smin_ref[...].T, axis=0, keepdims=True)
    lo0 = jnp.broadcast_to(_sortable(pltpu.bitcast(rmin, I32)), (8, t))
    hi0 = jnp.broadcast_to(_sortable(pltpu.bitcast(rmax, I32)) + 1, (8, t))
    hi0 = jnp.where(n_adm <= topk, lo0 + 1, hi0)

    def unresolved(lo, hi, clo):
        return (clo != topk) & (_midpoint(lo, hi) > lo)

    def count_unresolved(lo, hi, clo):
        return jnp.sum(jnp.where(unresolved(lo, hi, clo)[0:1, :], 1.0, 0.0))

    def tail_quantile(q):
        q = jnp.clip(q, 1e-7, 1.0 - 1e-7)
        u = jnp.sqrt(-2.0 * jnp.log(jnp.minimum(q, 1.0 - q)))
        x = u - ((0.010328 * u + 0.802853) * u + 2.515517) / (
            ((0.001308 * u + 0.189269) * u + 1.432788) * u + 1.0)
        return jnp.where(q <= 0.5, x, -x)

    inv_n = 1.0 / (n_adm.astype(F32) + 1.0)
    z_k = tail_quantile((topk - 0.5) * inv_n)

    def split_point(step, lo, hi, clo, chi):
        lof = pltpu.bitcast(_sortable(lo), F32)
        hif = pltpu.bitcast(_sortable(hi - 1), F32)
        za = tail_quantile((clo.astype(F32) - 0.5) * inv_n)
        zb = tail_quantile(jnp.maximum(chi.astype(F32), 0.5) * inv_n)
        frac = jnp.clip((z_k - za) / (zb - za), INTERP_CLIP, 1.0 - INTERP_CLIP)
        mid_i = _sortable(pltpu.bitcast(lof + (hif - lof) * frac, I32))
        mid = jnp.where(step % BISECT_EVERY == BISECT_EVERY - 1, _midpoint(lo, hi), mid_i)
        return jnp.minimum(jnp.maximum(mid, lo + 1), hi - 1)

    def select_cond(carry):
        return (carry[0] < MAX_SELECT_STEPS) & (carry[1] > STRAGGLERS + 0.5)

    def select_body(carry):
        step, _, mid, lo, hi, clo, chi = carry
        for s0 in range(0, t, SELECT_ROWS):
            rows = slice(s0, s0 + SELECT_ROWS)
            midr = mid_ref[rows, :]

            def count_body(slot, cnt):
                for c0 in range(0, w2, LANES):
                    cnt = cnt + (keys_ref[slot, rows, c0:c0 + LANES] >= midr).astype(I32)
                return cnt

            cnt_ref[rows, :] = lax.fori_loop(0, n_slots, count_body,
                                             jnp.zeros((SELECT_ROWS, LANES), I32))

        cnt_b = cnt_ref[...].astype(F32).astype(BF16)
        c = lax.dot_general(ones_row, cnt_b, (((1,), (1,)), ((), ())),
                            preferred_element_type=F32).astype(I32)
        active = unresolved(lo, hi, clo)
        up = active & (c >= topk)
        down = active & (c < topk)
        lo = jnp.where(up, mid, lo)
        hi = jnp.where(down, mid, hi)
        clo = jnp.where(up, c, clo)
        chi = jnp.where(down, c, chi)
        mid = split_point(step + 1, lo, hi, clo, chi)
        mid_ref[...] = spread(mid)
        return step + 1, count_unresolved(lo, hi, clo), mid, lo, hi, clo, chi

    chi0 = jnp.zeros((8, t), I32)
    mid0 = split_point(0, lo0, hi0, n_adm, chi0)
    mid_ref[...] = spread(mid0)
    sel = lax.while_loop(select_cond, select_body,
                         (jnp.int32(0), count_unresolved(lo0, hi0, n_adm), mid0, lo0, hi0, n_adm,
                          chi0))
    lo_ref[...] = spread(sel[3])
    hi_ref[...] = spread(sel[4])
    clo_ref[...] = spread(sel[5])

    def pick_rows(mask8):
        lane_f = lax.broadcasted_iota(I32, (8, t), 1).astype(F32)
        picks = []
        for _ in range(STRAGGLERS):
            first = jnp.min(jnp.where(mask8, lane_f, float(t)))
            mask8 = mask8 & (lane_f != first)
            r = first.astype(I32)
            picks.append((jnp.minimum(r, t - 1), r < t))
        return picks

    def gather(picks):
        def gather_body(slot, carry):
            for j, (r, _) in enumerate(picks):
                strag_ref[slot, j:j + 1, :] = keys_ref[slot, pl.ds(r, 1), :]
            return carry

        lax.fori_loop(0, n_slots, gather_body, 0)

    def rows_of(ref, picks):
        return jnp.concatenate([ref[pl.ds(r, 1), :] for r, _ in picks], axis=0)

    def real_rows(picks):
        return jnp.concatenate([jnp.broadcast_to(ok.astype(I32), (1, LANES)) for _, ok in picks],
                               axis=0) > 0

    def strip_sum(cnt):
        return jnp.broadcast_to(jnp.sum(cnt.astype(F32), axis=1, keepdims=True),
                                (STRAGGLERS, LANES)).astype(I32)

    @pl.when(sel[1] > 0.5)
    def _():
        picks = pick_rows(unresolved(sel[3], sel[4], sel[5]))
        gather(picks)
        lo8 = rows_of(lo_ref, picks)
        clo8 = rows_of(clo_ref, picks)
        hi8 = jnp.where(real_rows(picks), rows_of(hi_ref, picks), lo8 + 1)

        def open8(lo, hi, clo):
            return jnp.max(jnp.where(unresolved(lo, hi, clo), 1.0, 0.0))

        def finish_body(carry):
            _, lo, hi, clo = carry
            mid = _midpoint(lo, hi)

            def count_body(slot, c):
                cnt, above, below = c
                for c0 in range(0, w2, LANES):
                    k = strag_ref[slot, :, c0:c0 + LANES]
                    ge = k >= mid
                    cnt = cnt + ge.astype(I32)
                    above = jnp.minimum(above, jnp.where(ge, k, KEY_MAX))
                    below = jnp.maximum(below, jnp.where(ge, KEY_MIN, k))
                return cnt, above, below

            shape = (STRAGGLERS, LANES)
            cnt, above, below = lax.fori_loop(
                0, n_slots, count_body,
                (jnp.zeros(shape, I32), jnp.full(shape, KEY_MAX, I32), jnp.full(shape, KEY_MIN, I32)))
            c = strip_sum(cnt)
            above = jnp.broadcast_to(jnp.min(above, axis=1, keepdims=True), shape)
            below = jnp.broadcast_to(jnp.max(below, axis=1, keepdims=True), shape)
            active = unresolved(lo, hi, clo)
            up = active & (c >= topk)
            lo = jnp.where(up, above, lo)
            hi = jnp.where(active & (c < topk), below + 1, hi)
            clo = jnp.where(up, c, clo)
            return open8(lo, hi, clo), lo, hi, clo

        _, lo8, _, clo8 = lax.while_loop(lambda carry: carry[0] > 0.5, finish_body,
                                         (open8(lo8, hi8, clo8), lo8, hi8, clo8))
        for j, (r, ok) in enumerate(picks):
            @pl.when(ok)
            def _():
                lo_ref[pl.ds(r, 1), :] = lo8[j:j + 1, :]
                clo_ref[pl.ds(r, 1), :] = clo8[j:j + 1, :]

    def count_ties():
        return jnp.max(jnp.where(clo_ref[...] > topk, 1.0, 0.0))

    def tie_round(_):
        tied8 = jnp.where(clo_ref[...] > topk, 1.0, 0.0).T[0:8, :] > 0.5
        picks = pick_rows(tied8)
        gather(picks)
        real = _lanes(real_rows(picks).astype(I32), w2) > 0
        thr = _lanes(rows_of(lo_ref, picks), w2)
        col2 = lax.broadcasted_iota(I32, (STRAGGLERS, w2), 1)

        def lane_fold(x):
            return sum(x[:, c0:c0 + LANES] for c0 in range(0, w2, LANES))

        def gt_body(slot, cnt):
            return cnt + lane_fold((strag_ref[slot] > thr).astype(I32))

        shape = (STRAGGLERS, LANES)
        need = topk - strip_sum(lax.fori_loop(0, n_slots, gt_body, jnp.zeros(shape, I32)))

        def cut_body(_, carry):
            jlo, jhi = carry
            jm = _lanes((jlo + jhi) >> 1, w2)

            def eq_body(slot, cnt):
                hit = (strag_ref[slot] == thr) & ((col2 + slot * w2) < jm)
                return cnt + lane_fold(hit.astype(I32))

            c = strip_sum(lax.fori_loop(0, n_slots, eq_body, jnp.zeros(shape, I32)))
            ok = c >= need
            mid = (jlo + jhi) >> 1
            return jnp.where(ok, jlo, mid), jnp.where(ok, mid, jhi)

        n_steps = (keys_ref.shape[0] * w2).bit_length() + 1
        _, cut = lax.fori_loop(0, n_steps, cut_body,
                               (jnp.zeros(shape, I32), jnp.full(shape, 1, I32) * (n_slots * w2)))
        cut = _lanes(cut, w2)

        def drop_body(slot, carry):
            key = strag_ref[slot]
            drop = real & (key == thr) & ((col2 + slot * w2) >= cut)
            strag_ref[slot] = jnp.where(drop, KEY_MIN, key)
            for j, (r, ok) in enumerate(picks):
                @pl.when(ok)
                def _():
                    keys_ref[slot, pl.ds(r, 1), :] = strag_ref[slot, j:j + 1, :]
            return carry

        lax.fori_loop(0, n_slots, drop_body, 0)
        for j, (r, ok) in enumerate(picks):
            @pl.when(ok)
            def _():
                clo_ref[pl.ds(r, 1), :] = jnp.full((1, LANES), topk, I32)
        return count_ties()

    lax.while_loop(lambda n: n > 0.5, tie_round, count_ties())

    m_ref[...] = jnp.full(m_ref.shape, NEG_BIG, F32)
    acc_ref[...] = jnp.zeros(acc_ref.shape, F32)

    def attend(slot, biases):
        n = len(biases) * t
        nh = DSA_HEADS
        thr = _lanes(lo_ref[...], n)
        maskb = jnp.where(keys_ref[slot, :, :n] >= thr, 0.0, NEG_BIG)
        v = vp_ref[0, pl.ds(pl.multiple_of(slot * w2, w2), n), :]
        q = dq_ref[0].reshape(nh * t, DSA_HEAD_DIM)
        lg = dot(q, dkt_ref[0, slot, :, :n]).reshape(nh, t, n) + maskb[None]
        if any(which is not None for which in biases):
            lg = jnp.concatenate(
                [lg[:, :, k * t:(k + 1) * t] if which is None
                 else lg[:, :, k * t:(k + 1) * t] + bias_ref[:, which]
                 for k, which in enumerate(biases)], axis=2)
        m_old = m_ref[...]
        m_new = jnp.maximum(m_old, jnp.broadcast_to(jnp.max(lg, axis=2, keepdims=True),
                                                    (nh, t, LANES)))
        p = jnp.exp2(lg - jnp.concatenate([m_new] * (n // LANES), axis=2))
        pv = dot(p.astype(BF16).reshape(nh * t, n), v).reshape(nh, t, LANES)
        acc_ref[...] = acc_ref[...] * jnp.exp2(m_old - m_new) + pv
        m_ref[...] = m_new

    def far_body(slot, carry):
        attend(slot, (None, None))
        return carry

    lax.fori_loop(0, jnp.where(odd, last, last - 1), far_body, 0)

    @pl.when(odd)
    def _():
        attend(last, (0, 1))

    @pl.when(jnp.logical_not(odd) & (i >= 2))
    def _():
        attend(last - 1, (None, 0))

    @pl.when(jnp.logical_not(odd))
    def _():
        attend(last, (1,))

    lane = lax.broadcasted_iota(I32, (t, LANES), 1)
    for pair in range(DSA_HEADS // 2):
        a0 = acc_ref[2 * pair]
        a1 = acc_ref[2 * pair + 1]
        even = a0 / pltpu.roll(a0, DSA_HEAD_DIM, 1)
        odd_h = pltpu.roll(a1, DSA_HEAD_DIM, 1) / a1
        o_ref[:, pair * LANES:(pair + 1) * LANES] = jnp.where(lane < DSA_HEAD_DIM, even,
                                                              odd_h).astype(BF16)
    y_ref[0] = gb_ref[0] * dot(o_ref[...], wout_ref[...])


def _dsa(dq, iq, iwb, gb, ikt, dkt, vp, rel_bias, w_out):
    bsz, _, s, _ = dq.shape
    d = w_out.shape[1]
    t = DSA_TILE
    nq = s // t
    n_slots = nq // 2
    topk = min(IDX_TOPK, s // 4)
    heads = pl.BlockSpec((1, DSA_HEADS, t, DSA_HEAD_DIM), lambda b, i: (b, 0, i, 0))
    tok = lambda n: pl.BlockSpec((1, t, n), lambda b, i: (b, i, 0))
    per_batch = lambda shape: pl.BlockSpec((1,) + shape, lambda b, i: (b,) + (0,) * len(shape),
                                           pipeline_mode=pl.Buffered(1))
    row_state = lambda dt: pltpu.VMEM((t, LANES), dt)
    return pl.pallas_call(
        functools.partial(_dsa_kernel, topk=topk),
        grid=(bsz, nq),
        in_specs=[pl.BlockSpec(memory_space=pltpu.SMEM),
                  heads, heads, tok(IDX_HEADS * LANES), tok(d),
                  per_batch((n_slots, IDX_DIM, DSA_SLOT)), per_batch((n_slots, DSA_HEAD_DIM, DSA_SLOT)),
                  per_batch((s, LANES)), _const_spec(w_out.shape)],
        out_specs=tok(d),
        out_shape=jax.ShapeDtypeStruct((bsz, s, d), F32),
        scratch_shapes=[pltpu.VMEM((n_slots, t, DSA_SLOT), I32),
                        row_state(I32), row_state(I32), row_state(I32),
                        row_state(I32), row_state(I32),
                        pltpu.VMEM((n_slots, STRAGGLERS, DSA_SLOT), I32),
                        row_state(F32), row_state(F32),
                        pltpu.VMEM((DSA_HEADS, 2, t, t), F32),
                        pltpu.VMEM((DSA_HEADS, t, LANES), F32),
                        pltpu.VMEM((DSA_HEADS, t, LANES), F32),
                        pltpu.VMEM((t, DSA_HEADS * DSA_HEAD_DIM), BF16)],
        compiler_params=_params(("arbitrary", "arbitrary")),
        name="dsa",
    )(rel_bias.reshape(-1), dq, iq, iwb, gb, ikt, dkt, vp, w_out.astype(BF16))


def _moe_kernel(x_ref, yr_ref, yd_ref, mod_ref, wo_ref, gffn_ref, wr_ref, br_ref, w1_ref, w3_ref,
                w2_ref, gfin_ref, out_ref, gate_ref):
    dot = functools.partial(jnp.dot, preferred_element_type=F32)
    tm = x_ref.shape[1]
    mix = dot((yr_ref[0] + yd_ref[0]).astype(BF16), wo_ref[...])
    h1 = x_ref[0] + mod_ref[0, 2:3, :] * mix
    hn = h1 * lax.rsqrt(jnp.mean(h1 * h1, axis=-1, keepdims=True) + NORM_EPS) * gffn_ref[...]
    u2 = hn * (1.0 + mod_ref[0, 4:5, :]) + mod_ref[0, 3:4, :]

    logits = _dot3(u2, wr_ref[...]) + br_ref[...]
    lane = lax.broadcasted_iota(I32, (tm, LANES), 1)
    big = jnp.int32(LANES)
    rmax = lambda a: jnp.max(a, axis=1, keepdims=True)
    rmin = lambda a: jnp.min(a, axis=1, keepdims=True)
    is_g = lane < N_GROUPS
    gl = jnp.where(is_g, logits, -jnp.inf)
    gmax = rmax(gl)
    gsel = rmin(jnp.where(is_g & (gl == gmax), lane, big))
    gp = 1.0 / jnp.sum(jnp.where(is_g, jnp.exp(gl - gmax), 0.0), axis=1, keepdims=True)
    e_lane = lane - N_GROUPS
    in_grp = (e_lane >= 0) & (e_lane < N_EXPERTS) & ((e_lane // EXPERTS_PER_GROUP) == gsel)
    el = jnp.where(in_grp, logits, -jnp.inf)
    v1 = rmax(el)
    i1 = rmin(jnp.where(in_grp & (el == v1), lane, big))
    el2 = jnp.where(lane == i1, -jnp.inf, el)
    v2 = rmax(el2)
    i2 = rmin(jnp.where(in_grp & (lane != i1) & (el2 == v2), lane, big))
    e2 = jnp.exp(v2 - v1)
    den = 1.0 + e2
    w1 = gp * (1.0 / den)
    w2 = gp * (e2 / den)
    for e in range(N_EXPERTS):
        ge = jnp.where(i1 == e + N_GROUPS, w1, 0.0) + jnp.where(i2 == e + N_GROUPS, w2, 0.0)
        gate_ref[e] = jnp.broadcast_to(ge, (tm, LANES))

    u2b = u2.astype(BF16)
    y = jnp.zeros_like(h1)
    for e in range(N_EXPERTS):
        a = dot(u2b, w1_ref[e])
        b = dot(u2b, w3_ref[e])
        g = gate_ref[e]
        hid = a * jax.nn.sigmoid(a) * b * jnp.concatenate([g] * (EXPERT_FF // LANES), axis=1)
        y = y + dot(hid.astype(BF16), w2_ref[e])
    h2 = h1 + mod_ref[0, 5:6, :] * y
    out_ref[0] = h2 * lax.rsqrt(jnp.mean(h2 * h2, axis=-1, keepdims=True) + NORM_EPS) * gfin_ref[...]


def _merge_moe(x, y_ret, y_dsa, mod, w_o, g_ffn, w_gr, b_gr, w_er, b_er, w1, w3, w2, g_fin):
    bsz, s, d = x.shape
    tm = MOE_TOKENS
    pad = LANES - N_GROUPS - N_EXPERTS
    wr = jnp.concatenate([w_gr, w_er, jnp.zeros((d, pad), F32)], axis=1)
    br = jnp.concatenate([b_gr, b_er, jnp.zeros((pad,), F32)]).reshape(1, LANES)
    tok = pl.BlockSpec((1, tm, d), lambda b, i: (b, i, 0))
    return pl.pallas_call(
        _moe_kernel,
        grid=(bsz, s // tm),
        in_specs=[tok, tok, tok,
                  pl.BlockSpec((1, 6, d), lambda b, i: (b, 0, 0)),
                  _const_spec(w_o.shape), _const_spec((1, d)), _const_spec(wr.shape),
                  _const_spec((1, LANES)), _const_spec(w1.shape), _const_spec(w3.shape),
                  _const_spec(w2.shape), _const_spec((1, d))],
        out_specs=tok,
        out_shape=jax.ShapeDtypeStruct((bsz, s, d), F32),
        scratch_shapes=[pltpu.VMEM((N_EXPERTS, tm, LANES), F32)],
        compiler_params=_params(("arbitrary", "arbitrary")),
        name="merge_moe",
    )(x, y_ret, y_dsa, mod, w_o.astype(BF16), g_ffn.reshape(1, d), wr, br, w1.astype(BF16),
      w3.astype(BF16), w2.astype(BF16), g_fin.reshape(1, d))


def kernel(x, c, w_ada, b_ada, norm_mix_g, w_in, ret_gn_g, dsa_kv_norm_g, w_dsa_kv_up, rel_bias,
           w_ret_out, w_dsa_out, w_gate, b_gate, w_o, norm_ffn_g, w_group_router, b_group_router,
           w_expert_router, b_expert_router, w_exp_gate, w_exp_up, w_exp_down, norm_final_g):
    assert w_ada.shape[0] == 1, "single-layer block"
    bsz, s, d = x.shape
    assert s % DSA_SLOT == 0 and DSA_TILE == PROJ_TOKENS
    mod = _ada(c, w_ada[0], b_ada[0]).reshape(bsz, 6, d)
    (rq, rk, rv, rg, dq, iq, vp, dkt, ikt, iwb, ga, gb) = _proj(
        x, mod, norm_mix_g[0], w_in[0], dsa_kv_norm_g[0], w_dsa_kv_up[0], w_gate[0], b_gate[0])
    y_ret = _retention(rq, rk, rv, rg, ga, ret_gn_g[0], w_ret_out[0])
    y_dsa = _dsa(dq, iq, iwb, gb, ikt, dkt, vp, rel_bias, w_dsa_out[0])
    return _merge_moe(x, y_ret, y_dsa, mod, w_o[0], norm_ffn_g[0], w_group_router[0],
                      b_group_router[0], w_expert_router[0], b_expert_router[0], w_exp_gate[0],
                      w_exp_up[0], w_exp_down[0], norm_final_g)
```

```python
import functools

import jax
import jax.numpy as jnp
from jax import lax
from jax.experimental import pallas as pl
from jax.experimental.pallas import tpu as pltpu

CHUNK = 64
RET_HEADS = 8
RET_QK_DIM = 64
RET_V_DIM = 128
DSA_HEADS = 8
DSA_HEAD_DIM = 64
DSA_KV_LATENT = 128
IDX_HEADS = 8
IDX_DIM = 64
IDX_TOPK = 256
N_BUCKETS = 32
N_GROUPS = 4
EXPERTS_PER_GROUP = 4
N_EXPERTS = N_GROUPS * EXPERTS_PER_GROUP
EXPERT_FF = 256
ROPE_BASE = 10000.0
NORM_EPS = 1e-6
GN_EPS = 1e-5
IN_SPLITS = (512, 512, 1024, 1024, 512, 128, 512, 64, 8)

LANES = 128
VMEM_LIMIT_BYTES = 56 * 1024 * 1024

PROJ_TOKENS = 256
RET_TOKENS = 256
DSA_TILE = 256
DSA_SLOT = 2 * DSA_TILE
SELECT_ROWS = 128
MOE_TOKENS = 256

INTERP_CLIP = 0.02
BISECT_EVERY = 4
MAX_SELECT_STEPS = 128
STRAGGLERS = 8

LOG2E = 1.4426950408889634
NEG_BIG = -1e30
KEY_MIN = -2147483648
KEY_MAX = 2147483647
BUCKET_STEPS = (12, 16, 23, 32, 46, 64, 91)

F32 = jnp.float32
BF16 = jnp.bfloat16
I32 = jnp.int32


def _const_spec(shape):
    nd = len(shape)
    return pl.BlockSpec(shape, lambda *_: (0,) * nd, pipeline_mode=pl.Buffered(1))


def _params(sem):
    return pltpu.CompilerParams(dimension_semantics=sem, vmem_limit_bytes=VMEM_LIMIT_BYTES)


def _split3(a):
    hi = a.astype(BF16)
    lo = (a - hi.astype(F32)).astype(BF16)
    return hi, lo


def _dot3(a, w):
    ah, al = _split3(a)
    wh, wl = _split3(w)
    d = functools.partial(jnp.dot, preferred_element_type=F32)
    return d(ah, wh) + (d(ah, wl) + d(al, wh))


def _ada_kernel(c_ref, w_ref, b_ref, o_ref):
    c = c_ref[...]
    o_ref[...] = _dot3(c * jax.nn.sigmoid(c), w_ref[...]) + b_ref[...]


def _ada(c, w, b):
    bsz, d = c.shape
    n = w.shape[1]
    rows = 8
    cp = jnp.zeros((rows, d), F32).at[:bsz].set(c)
    tn = 1536
    out = pl.pallas_call(
        _ada_kernel,
        grid=(n // tn,),
        in_specs=[pl.BlockSpec((rows, d), lambda j: (0, 0)),
                  pl.BlockSpec((d, tn), lambda j: (0, j)),
                  pl.BlockSpec((1, tn), lambda j: (0, j))],
        out_specs=pl.BlockSpec((rows, tn), lambda j: (0, j)),
        out_shape=jax.ShapeDtypeStruct((rows, n), F32),
        compiler_params=_params(("arbitrary",)),
        name="adaln",
    )(cp, w, b.reshape(1, n))
    return out[:bsz]


def _proj_kernel(x_ref, mod_ref, g_ref, cos_ref, sin_ref, wqk_ref, wvg_ref, wdq_ref, wiq_ref,
                 wkv_ref, wikw_ref, kvg_ref, wup_ref, wgate_ref, bgate_ref,
                 rq_ref, rk_ref, rv_ref, rg_ref, dq_ref, iq_ref, vp_ref, dkt_ref, ikt_ref,
                 iwb_ref, ga_ref, gb_ref):
    dot = functools.partial(jnp.dot, preferred_element_type=F32)
    x = x_ref[0]
    tm = x.shape[0]
    xn = x * lax.rsqrt(jnp.mean(x * x, axis=-1, keepdims=True) + NORM_EPS) * g_ref[...]
    u = xn * (1.0 + mod_ref[0, 1:2, :]) + mod_ref[0, 0:1, :]
    ub = u.astype(BF16)

    lane = lax.broadcasted_iota(I32, (tm, LANES), 1)
    first_half = (lane % RET_QK_DIM) < (RET_QK_DIM // 2)
    cos = cos_ref[...]
    sin = sin_ref[...]
    zqk = dot(ub, wqk_ref[...])
    for which, out_ref, scale in ((0, rq_ref, 1.0), (1, rk_ref, RET_QK_DIM ** -0.5)):
        for pair in range(RET_HEADS // 2):
            c0 = which * RET_HEADS * RET_QK_DIM + pair * LANES
            z = zqk[:, c0:c0 + LANES]
            rot = jnp.where(first_half, pltpu.roll(z, LANES - 32, 1), pltpu.roll(z, 32, 1))
            r = (z * cos + rot * sin) * scale
            out_ref[0, 2 * pair] = r[:, :RET_QK_DIM].astype(BF16)
            out_ref[0, 2 * pair + 1] = r[:, RET_QK_DIM:].astype(BF16)

    zvg = dot(ub, wvg_ref[...])
    nv = RET_HEADS * RET_V_DIM
    rv_ref[0] = zvg[:, :nv].astype(BF16)
    g = zvg[:, nv:]
    rg_ref[0] = g * jax.nn.sigmoid(g)

    zdq = dot(ub, wdq_ref[...]) * (DSA_HEAD_DIM ** -0.5 * LOG2E)
    ziq = dot(ub, wiq_ref[...])
    for h in range(DSA_HEADS):
        dq_ref[0, h] = zdq[:, h * DSA_HEAD_DIM:(h + 1) * DSA_HEAD_DIM].astype(BF16)
        iq_ref[0, h] = ziq[:, h * IDX_DIM:(h + 1) * IDX_DIM].astype(BF16)

    zkv = dot(ub, wkv_ref[...])
    kvn = zkv * lax.rsqrt(jnp.mean(zkv * zkv, axis=-1, keepdims=True) + NORM_EPS) * kvg_ref[...]
    lat = dot(kvn.astype(BF16), wup_ref[...])
    dkt_ref[0, 0] = lat.T[:DSA_HEAD_DIM, :].astype(BF16)
    vp_ref[0] = jnp.where(lane < DSA_HEAD_DIM, pltpu.roll(lat, DSA_HEAD_DIM, 1), 1.0).astype(BF16)

    zik = dot(ub, wikw_ref[...])
    ikt_ref[0, 0] = zik.T[:IDX_DIM, :].astype(BF16)
    idx_scale = (IDX_DIM ** -0.5) * (IDX_HEADS ** -0.5)
    for h in range(IDX_HEADS):
        col = zik[:, IDX_DIM + h:IDX_DIM + h + 1] * idx_scale
        iwb_ref[0, :, h * LANES:(h + 1) * LANES] = jnp.broadcast_to(col, (tm, LANES))

    zg = jax.nn.sigmoid(dot(ub, wgate_ref[...]) + bgate_ref[...])
    d = zg.shape[1] // 2
    ga_ref[0] = zg[:, :d]
    gb_ref[0] = zg[:, d:]


def _proj(x, mod, g_mix, w_in, kv_g, w_up, w_gate, b_gate):
    bsz, s, d = x.shape
    tm = PROJ_TOKENS
    nt = s // tm
    offs = [0]
    for n in IN_SPLITS:
        offs.append(offs[-1] + n)
    seg = lambda i: w_in[:, offs[i]:offs[i + 1]]
    wqk = jnp.concatenate([seg(0), seg(1)], axis=1).astype(BF16)
    wvg = jnp.concatenate([seg(2), seg(3)], axis=1).astype(BF16)
    wdq = seg(4).astype(BF16)
    wkv = seg(5).astype(BF16)
    wiq = seg(6).astype(BF16)
    wikw = jnp.concatenate([seg(7), seg(8), jnp.zeros((d, LANES - IDX_DIM - IDX_HEADS), F32)],
                           axis=1).astype(BF16)

    pos = jnp.arange(s, dtype=jnp.int32)
    freqs = ROPE_BASE ** (-jnp.arange(0, RET_QK_DIM, 2, dtype=F32) / RET_QK_DIM)
    ang = pos.astype(F32)[:, None] * freqs[None, :]
    cos_h = jnp.concatenate([jnp.cos(ang), jnp.cos(ang)], axis=1)
    sin_h = jnp.concatenate([-jnp.sin(ang), jnp.sin(ang)], axis=1)
    cos_t = jnp.concatenate([cos_h, cos_h], axis=1)
    sin_t = jnp.concatenate([sin_h, sin_h], axis=1)

    tok = lambda n: pl.BlockSpec((1, tm, n), lambda b, i: (b, i, 0))
    heads = pl.BlockSpec((1, DSA_HEADS, tm, DSA_HEAD_DIM), lambda b, i: (b, 0, i, 0))
    trans = pl.BlockSpec((1, 1, DSA_HEAD_DIM, tm), lambda b, i: (b, i // 2, 0, i % 2))
    outs = pl.pallas_call(
        _proj_kernel,
        grid=(bsz, nt),
        in_specs=[tok(d),
                  pl.BlockSpec((1, 6, d), lambda b, i: (b, 0, 0)),
                  _const_spec((1, d)),
                  pl.BlockSpec((tm, LANES), lambda b, i: (i, 0)),
                  pl.BlockSpec((tm, LANES), lambda b, i: (i, 0)),
                  _const_spec(wqk.shape), _const_spec(wvg.shape), _const_spec(wdq.shape),
                  _const_spec(wiq.shape), _const_spec(wkv.shape), _const_spec(wikw.shape),
                  _const_spec((1, DSA_KV_LATENT)), _const_spec(w_up.shape),
                  _const_spec(w_gate.shape), _const_spec((1, w_gate.shape[1]))],
        out_specs=[heads, heads, tok(1024), tok(1024), heads, heads, tok(LANES), trans, trans,
                   tok(IDX_HEADS * LANES), tok(d), tok(d)],
        out_shape=[jax.ShapeDtypeStruct((bsz, RET_HEADS, s, RET_QK_DIM), BF16),
                   jax.ShapeDtypeStruct((bsz, RET_HEADS, s, RET_QK_DIM), BF16),
                   jax.ShapeDtypeStruct((bsz, s, RET_HEADS * RET_V_DIM), BF16),
                   jax.ShapeDtypeStruct((bsz, s, RET_HEADS * RET_V_DIM), F32),
                   jax.ShapeDtypeStruct((bsz, DSA_HEADS, s, DSA_HEAD_DIM), BF16),
                   jax.ShapeDtypeStruct((bsz, IDX_HEADS, s, IDX_DIM), BF16),
                   jax.ShapeDtypeStruct((bsz, s, LANES), BF16),
                   jax.ShapeDtypeStruct((bsz, nt // 2, DSA_HEAD_DIM, 2 * tm), BF16),
                   jax.ShapeDtypeStruct((bsz, nt // 2, IDX_DIM, 2 * tm), BF16),
                   jax.ShapeDtypeStruct((bsz, s, IDX_HEADS * LANES), F32),
                   jax.ShapeDtypeStruct((bsz, s, d), F32),
                   jax.ShapeDtypeStruct((bsz, s, d), F32)],
        compiler_params=_params(("arbitrary", "arbitrary")),
        name="in_proj",
    )(x, mod, g_mix.reshape(1, d), cos_t, sin_t, wqk, wvg, wdq, wiq, wkv, wikw,
      kv_g.reshape(1, -1), w_up.astype(BF16), w_gate.astype(BF16), b_gate.reshape(1, -1))
    return outs


def _ret_kernel(rq_ref, rk_ref, rv_ref, rg_ref, ga_ref, dmat_ref, qdec_ref, kdec_ref, cdec_ref,
                gng_ref, wout_ref, y_ref, state_ref, o_ref):
    dot = functools.partial(jnp.dot, preferred_element_type=F32)

    @pl.when(pl.program_id(1) == 0)
    def _():
        state_ref[...] = jnp.zeros_like(state_ref)

    for h in range(RET_HEADS):
        q = rq_ref[0, h]
        k = rk_ref[0, h]
        cols = slice(h * RET_V_DIM, (h + 1) * RET_V_DIM)
        v = rv_ref[0, :, cols]
        st = state_ref[h]
        s = lax.dot_general(q, k, (((1,), (1,)), ((), ())), preferred_element_type=F32)
        o = dot((s * dmat_ref[h]).astype(BF16), v)
        qd = (q.astype(F32) * qdec_ref[h]).astype(BF16)
        o = o + dot(qd, st.astype(BF16))
        kd = (k.astype(F32) * kdec_ref[h]).astype(BF16)
        kv = lax.dot_general(kd, v, (((0,), (0,)), ((), ())), preferred_element_type=F32)
        state_ref[h] = st * cdec_ref[h] + kv
        mu = jnp.mean(o, axis=-1, keepdims=True)
        oc = o - mu
        var = jnp.mean(oc * oc, axis=-1, keepdims=True)
        on = oc * lax.rsqrt(var + GN_EPS) * gng_ref[:, cols]
        o_ref[:, cols] = (rg_ref[0, :, cols] * on).astype(BF16)
    y_ref[0] = ga_ref[0] * dot(o_ref[...], wout_ref[...])


def _retention(rq, rk, rv, rg, ga, gn_g, w_out):
    bsz, s, nv = rv.shape
    d = w_out.shape[1]
    tr = RET_TOKENS
    log_gamma = jnp.log(1.0 - 2.0 ** (-5.0 - jnp.arange(RET_HEADS, dtype=F32)))
    idx = jnp.arange(tr, dtype=F32)
    ch = jnp.arange(tr, dtype=jnp.int32) // CHUNK
    dist = jnp.abs(idx[:, None] - idx[None, :])
    dmat = jnp.where((ch[None, :] <= ch[:, None])[None],
                     jnp.exp(log_gamma[:, None, None] * dist[None]), 0.0)
    qdec = jnp.broadcast_to(jnp.exp(log_gamma[:, None] * (idx + 1.0)[None, :])[:, :, None],
                            (RET_HEADS, tr, RET_QK_DIM))
    kdec = jnp.broadcast_to(jnp.exp(log_gamma[:, None] * (tr - 1.0 - idx)[None, :])[:, :, None],
                            (RET_HEADS, tr, RET_QK_DIM))
    cdec = jnp.broadcast_to(jnp.exp(log_gamma * tr)[:, None, None],
                            (RET_HEADS, RET_QK_DIM, RET_V_DIM))

    heads = pl.BlockSpec((1, RET_HEADS, tr, RET_QK_DIM), lambda b, i: (b, 0, i, 0))
    tok = lambda n: pl.BlockSpec((1, tr, n), lambda b, i: (b, i, 0))
    return pl.pallas_call(
        _ret_kernel,
        grid=(bsz, s // tr),
        in_specs=[heads, heads, tok(nv), tok(nv), tok(d),
                  _const_spec(dmat.shape), _const_spec(qdec.shape), _const_spec(kdec.shape),
                  _const_spec(cdec.shape), _const_spec((1, nv)), _const_spec(w_out.shape)],
        out_specs=tok(d),
        out_shape=jax.ShapeDtypeStruct((bsz, s, d), F32),
        scratch_shapes=[pltpu.VMEM((RET_HEADS, RET_QK_DIM, RET_V_DIM), F32),
                        pltpu.VMEM((tr, nv), BF16)],
        compiler_params=_params(("arbitrary", "arbitrary")),
        name="retention",
    )(rq, rk, rv, rg, ga, dmat, qdec, kdec, cdec, gn_g.reshape(1, nv), w_out.astype(BF16))


def _sortable(bits):
    return jnp.where(bits < 0, bits ^ 0x7FFFFFFF, bits)


def _midpoint(lo, hi):
    return (lo >> 1) + (hi >> 1) + (lo & hi & 1)


def _lanes(a, n):
    return jnp.concatenate([a] * (n // LANES), axis=1)


def _dsa_kernel(relb_ref, dq_ref, iq_ref, iwb_ref, gb_ref, ikt_ref, dkt_ref, vp_ref, wout_ref,
                y_ref, keys_ref, lo_ref, clo_ref, hi_ref, mid_ref, cnt_ref, strag_ref, smax_ref,
                smin_ref, bias_ref, m_ref, acc_ref, o_ref, *, topk):
    t = DSA_TILE
    w2 = DSA_SLOT
    dot = functools.partial(jnp.dot, preferred_element_type=F32)
    i = pl.program_id(1)
    odd = (i % 2) == 1
    last = i // 2
    n_slots = last + 1
    row = lax.broadcasted_iota(I32, (t, t), 0)
    col = lax.broadcasted_iota(I32, (t, t), 1)
    diag_adm = (col // CHUNK) <= (row // CHUNK)
    far = (N_BUCKETS // 2 - 1) * DSA_HEADS

    @pl.when((pl.program_id(0) == 0) & (i == 0))
    def _():
        for which in range(2):
            rel = col - row - (t if which == 0 else 0)
            n = jnp.abs(rel)
            large = jnp.full((t, t), 8, I32)
            for step in BUCKET_STEPS:
                large = large + (n >= step).astype(I32)
            bucket = jnp.where(n < 8, n, large) + jnp.where(rel > 0, N_BUCKETS // 2, 0)
            for h in range(DSA_HEADS):
                b = jnp.zeros((t, t), F32)
                for k in range(N_BUCKETS):
                    b = jnp.where(bucket == k, relb_ref[k * DSA_HEADS + h], b)
                bias_ref[h, which] = (b - relb_ref[far + h]) * LOG2E

    smax_ref[...] = jnp.full((t, LANES), -jnp.inf, F32)
    smin_ref[...] = jnp.full((t, LANES), jnp.inf, F32)

    def score_cols(kt, adm):
        n = kt.shape[1]
        s = dot(iq_ref[0].reshape(IDX_HEADS * t, IDX_DIM), kt).reshape(IDX_HEADS, t, n)
        acc = jnp.zeros((t, n), F32)
        for h in range(IDX_HEADS):
            acc = acc + jnp.maximum(s[h], 0.0) * _lanes(iwb_ref[0, :, h * LANES:(h + 1) * LANES], n)
        hi_src = acc if adm is None else jnp.where(adm, acc, -jnp.inf)
        lo_src = acc if adm is None else jnp.where(adm, acc, jnp.inf)
        mx = smax_ref[...]
        mn = smin_ref[...]
        for c0 in range(0, n, LANES):
            mx = jnp.maximum(mx, hi_src[:, c0:c0 + LANES])
            mn = jnp.minimum(mn, lo_src[:, c0:c0 + LANES])
        smax_ref[...] = mx
        smin_ref[...] = mn
        key = _sortable(pltpu.bitcast(acc, I32))
        return key if adm is None else jnp.where(adm, key, KEY_MIN)

    def score_body(slot, carry):
        keys_ref[slot] = score_cols(ikt_ref[0, slot], None)
        return carry

    lax.fori_loop(0, last, score_body, 0)

    @pl.when(odd)
    def _():
        keys_ref[last, :, :t] = score_cols(ikt_ref[0, last, :, :t], None)
        keys_ref[last, :, t:] = score_cols(ikt_ref[0, last, :, t:], diag_adm)

    @pl.when(jnp.logical_not(odd))
    def _():
        keys_ref[last, :, :t] = score_cols(ikt_ref[0, last, :, :t], diag_adm)
        keys_ref[last, :, t:] = jnp.full((t, t), KEY_MIN, I32)

    def spread(a):
        f = pltpu.bitcast(a[0:1, :], F32)
        return pltpu.bitcast(jnp.broadcast_to(f, (LANES, t)).T, I32)

    ones_row = jnp.ones((8, LANES), BF16)
    pos = i * t + lax.broadcasted_iota(I32, (8, t), 1)
    n_adm = (pos // CHUNK + 1) * CHUNK
    rmax = jnp.max(smax_ref[...].T, axis=0, keepdims=True)
    rmin = jnp.min(smin_ref[...].T, axis=0, keepdims=True)
    lo0 = jnp.broadcast_to(_sortable(pltpu.bitcast(rmin, I32)), (8, t))
    hi0 = jnp.broadcast_to(_sortable(pltpu.bitcast(rmax, I32)) + 1, (8, t))
    hi0 = jnp.where(n_adm <= topk, lo0 + 1, hi0)

    def unresolved(lo, hi, clo):
        return (clo != topk) & (_midpoint(lo, hi) > lo)

    def count_unresolved(lo, hi, clo):
        return jnp.sum(jnp.where(unresolved(lo, hi, clo)[0:1, :], 1.0, 0.0))

    def tail_quantile(q):
        q = jnp.clip(q, 1e-7, 1.0 - 1e-7)
        u = jnp.sqrt(-2.0 * jnp.log(jnp.minimum(q, 1.0 - q)))
        x = u - ((0.010328 * u + 0.802853) * u + 2.515517) / (
            ((0.001308 * u + 0.189269) * u + 1.432788) * u + 1.0)
        return jnp.where(q <= 0.5, x, -x)

    inv_n = 1.0 / (n_adm.astype(F32) + 1.0)
    z_k = tail_quantile((topk - 0.5) * inv_n)

    def split_point(step, lo, hi, clo, chi):
        lof = pltpu.bitcast(_sortable(lo), F32)
        hif = pltpu.bitcast(_sortable(hi - 1), F32)
        za = tail_quantile((clo.astype(F32) - 0.5) * inv_n)
        zb = tail_quantile(jnp.maximum(chi.astype(F32), 0.5) * inv_n)
        frac = jnp.clip((z_k - za) / (zb - za), INTERP_CLIP, 1.0 - INTERP_CLIP)
        mid_i = _sortable(pltpu.bitcast(lof + (hif - lof) * frac, I32))
        mid = jnp.where(step % BISECT_EVERY == BISECT_EVERY - 1, _midpoint(lo, hi), mid_i)
        return jnp.minimum(jnp.maximum(mid, lo + 1), hi - 1)

    def select_cond(carry):
        return (carry[0] < MAX_SELECT_STEPS) & (carry[1] > STRAGGLERS + 0.5)

    def select_body(carry):
        step, _, mid, lo, hi, clo, chi = carry
        for s0 in range(0, t, SELECT_ROWS):
            rows = slice(s0, s0 + SELECT_ROWS)
            midr = mid_ref[rows, :]

            def count_body(slot, cnt):
                for c0 in range(0, w2, LANES):
                    cnt = cnt + (keys_ref[slot, rows, c0:c0 + LANES] >= midr).astype(I32)
                return cnt

            cnt_ref[rows, :] = lax.fori_loop(0, n_slots, count_body,
                                             jnp.zeros((SELECT_ROWS, LANES), I32))

        cnt_b = cnt_ref[...].astype(F32).astype(BF16)
        c = lax.dot_general(ones_row, cnt_b, (((1,), (1,)), ((), ())),
                            preferred_element_type=F32).astype(I32)
        active = unresolved(lo, hi, clo)
        up = active & (c >= topk)
        down = active & (c < topk)
        lo = jnp.where(up, mid, lo)
        hi = jnp.where(down, mid, hi)
        clo = jnp.where(up, c, clo)
        chi = jnp.where(down, c, chi)
        mid = split_point(step + 1, lo, hi, clo, chi)
        mid_ref[...] = spread(mid)
        return step + 1, count_unresolved(lo, hi, clo), mid, lo, hi, clo, chi

    chi0 = jnp.zeros((8, t), I32)
    mid0 = split_point(0, lo0, hi0, n_adm, chi0)
    mid_ref[...] = spread(mid0)
    sel = lax.while_loop(select_cond, select_body,
                         (jnp.int32(0), count_unresolved(lo0, hi0, n_adm), mid0, lo0, hi0, n_adm,
                          chi0))
    lo_ref[...] = spread(sel[3])
    hi_ref[...] = spread(sel[4])
    clo_ref[...] = spread(sel[5])

    def pick_rows(mask8):
        lane_f = lax.broadcasted_iota(I32, (8, t), 1).astype(F32)
        picks = []
        for _ in range(STRAGGLERS):
            first = jnp.min(jnp.where(mask8, lane_f, float(t)))
            mask8 = mask8 & (lane_f != first)
            r = first.astype(I32)
            picks.append((jnp.minimum(r, t - 1), r < t))
        return picks, mask8

    def gather(picks):
        def gather_body(slot, carry):
            for j, (r, _) in enumerate(picks):
                strag_ref[slot, j:j + 1, :] = keys_ref[slot, pl.ds(r, 1), :]
            return carry

        lax.fori_loop(0, n_slots, gather_body, 0)

    def rows_of(ref, picks):
        return jnp.concatenate([ref[pl.ds(r, 1), :] for r, _ in picks], axis=0)

    def real_rows(picks):
        return jnp.concatenate([jnp.broadcast_to(ok.astype(I32), (1, LANES)) for _, ok in picks],
                               axis=0) > 0

    def strip_sum(cnt):
        return jnp.broadcast_to(jnp.sum(cnt.astype(F32), axis=1, keepdims=True),
                                (STRAGGLERS, LANES)).astype(I32)

    def straggler_round(left):
        picks, left = pick_rows(left > 0.5)
        gather(picks)
        lo8 = rows_of(lo_ref, picks)
        clo8 = rows_of(clo_ref, picks)
        hi8 = jnp.where(real_rows(picks), rows_of(hi_ref, picks), lo8 + 1)

        def open8(lo, hi, clo):
            return jnp.max(jnp.where(unresolved(lo, hi, clo), 1.0, 0.0))

        def finish_body(carry):
            _, lo, hi, clo = carry
            mid = _midpoint(lo, hi)

            def count_body(slot, c):
                cnt, above, below = c
                for c0 in range(0, w2, LANES):
                    k = strag_ref[slot, :, c0:c0 + LANES]
                    ge = k >= mid
                    cnt = cnt + ge.astype(I32)
                    above = jnp.minimum(above, jnp.where(ge, k, KEY_MAX))
                    below = jnp.maximum(below, jnp.where(ge, KEY_MIN, k))
                return cnt, above, below

            shape = (STRAGGLERS, LANES)
            cnt, above, below = lax.fori_loop(
                0, n_slots, count_body,
                (jnp.zeros(shape, I32), jnp.full(shape, KEY_MAX, I32), jnp.full(shape, KEY_MIN, I32)))
            c = strip_sum(cnt)
            above = jnp.broadcast_to(jnp.min(above, axis=1, keepdims=True), shape)
            below = jnp.broadcast_to(jnp.max(below, axis=1, keepdims=True), shape)
            active = unresolved(lo, hi, clo)
            up = active & (c >= topk)
            lo = jnp.where(up, above, lo)
            hi = jnp.where(active & (c < topk), below + 1, hi)
            clo = jnp.where(up, c, clo)
            return open8(lo, hi, clo), lo, hi, clo

        _, lo8, _, clo8 = lax.while_loop(lambda carry: carry[0] > 0.5, finish_body,
                                         (open8(lo8, hi8, clo8), lo8, hi8, clo8))
        for j, (r, ok) in enumerate(picks):
            @pl.when(ok)
            def _():
                lo_ref[pl.ds(r, 1), :] = lo8[j:j + 1, :]
                clo_ref[pl.ds(r, 1), :] = clo8[j:j + 1, :]
        return jnp.where(left, 1.0, 0.0)

    lax.while_loop(lambda left: jnp.max(left) > 0.5, straggler_round,
                   jnp.where(unresolved(sel[3], sel[4], sel[5]), 1.0, 0.0))

    def count_ties():
        return jnp.max(jnp.where(clo_ref[...] > topk, 1.0, 0.0))

    def tie_round(_):
        tied8 = jnp.where(clo_ref[...] > topk, 1.0, 0.0).T[0:8, :] > 0.5
        picks, _ = pick_rows(tied8)
        gather(picks)
        real = _lanes(real_rows(picks).astype(I32), w2) > 0
        thr = _lanes(rows_of(lo_ref, picks), w2)
        col2 = lax.broadcasted_iota(I32, (STRAGGLERS, w2), 1)

        def lane_fold(x):
            return sum(x[:, c0:c0 + LANES] for c0 in range(0, w2, LANES))

        def gt_body(slot, cnt):
            return cnt + lane_fold((strag_ref[slot] > thr).astype(I32))

        shape = (STRAGGLERS, LANES)
        need = topk - strip_sum(lax.fori_loop(0, n_slots, gt_body, jnp.zeros(shape, I32)))

        def cut_body(_, carry):
            jlo, jhi = carry
            jm = _lanes((jlo + jhi) >> 1, w2)

            def eq_body(slot, cnt):
                hit = (strag_ref[slot] == thr) & ((col2 + slot * w2) < jm)
                return cnt + lane_fold(hit.astype(I32))

            c = strip_sum(lax.fori_loop(0, n_slots, eq_body, jnp.zeros(shape, I32)))
            ok = c >= need
            mid = (jlo + jhi) >> 1
            return jnp.where(ok, jlo, mid), jnp.where(ok, mid, jhi)

        n_steps = (keys_ref.shape[0] * w2).bit_length() + 1
        _, cut = lax.fori_loop(0, n_steps, cut_body,
                               (jnp.zeros(shape, I32), jnp.full(shape, 1, I32) * (n_slots * w2)))
        cut = _lanes(cut, w2)

        def drop_body(slot, carry):
            key = strag_ref[slot]
            drop = real & (key == thr) & ((col2 + slot * w2) >= cut)
            strag_ref[slot] = jnp.where(drop, KEY_MIN, key)
            for j, (r, ok) in enumerate(picks):
                @pl.when(ok)
                def _():
                    keys_ref[slot, pl.ds(r, 1), :] = strag_ref[slot, j:j + 1, :]
            return carry

        lax.fori_loop(0, n_slots, drop_body, 0)
        for j, (r, ok) in enumerate(picks):
            @pl.when(ok)
            def _():
                clo_ref[pl.ds(r, 1), :] = jnp.full((1, LANES), topk, I32)
        return count_ties()

    lax.while_loop(lambda n: n > 0.5, tie_round, count_ties())

    m_ref[...] = jnp.full(m_ref.shape, NEG_BIG, F32)
    acc_ref[...] = jnp.zeros(acc_ref.shape, F32)

    def attend(slot, biases):
        n = len(biases) * t
        nh = DSA_HEADS
        thr = _lanes(lo_ref[...], n)
        maskb = jnp.where(keys_ref[slot, :, :n] >= thr, 0.0, NEG_BIG)
        v = vp_ref[0, pl.ds(pl.multiple_of(slot * w2, w2), n), :]
        q = dq_ref[0].reshape(nh * t, DSA_HEAD_DIM)
        lg = dot(q, dkt_ref[0, slot, :, :n]).reshape(nh, t, n) + maskb[None]
        if any(which is not None for which in biases):
            lg = jnp.concatenate(
                [lg[:, :, k * t:(k + 1) * t] if which is None
                 else lg[:, :, k * t:(k + 1) * t] + bias_ref[:, which]
                 for k, which in enumerate(biases)], axis=2)
        m_old = m_ref[...]
        m_new = jnp.maximum(m_old, jnp.broadcast_to(jnp.max(lg, axis=2, keepdims=True),
                                                    (nh, t, LANES)))
        p = jnp.exp2(lg - jnp.concatenate([m_new] * (n // LANES), axis=2))
        pv = dot(p.astype(BF16).reshape(nh * t, n), v).reshape(nh, t, LANES)
        acc_ref[...] = acc_ref[...] * jnp.exp2(m_old - m_new) + pv
        m_ref[...] = m_new

    def far_body(slot, carry):
        attend(slot, (None, None))
        return carry

    lax.fori_loop(0, jnp.where(odd, last, last - 1), far_body, 0)

    @pl.when(odd)
    def _():
        attend(last, (0, 1))

    @pl.when(jnp.logical_not(odd) & (i >= 2))
    def _():
        attend(last - 1, (None, 0))

    @pl.when(jnp.logical_not(odd))
    def _():
        attend(last, (1,))

    lane = lax.broadcasted_iota(I32, (t, LANES), 1)
    for pair in range(DSA_HEADS // 2):
        a0 = acc_ref[2 * pair]
        a1 = acc_ref[2 * pair + 1]
        even = a0 / pltpu.roll(a0, DSA_HEAD_DIM, 1)
        odd_h = pltpu.roll(a1, DSA_HEAD_DIM, 1) / a1
        o_ref[:, pair * LANES:(pair + 1) * LANES] = jnp.where(lane < DSA_HEAD_DIM, even,
                                                              odd_h).astype(BF16)
    y_ref[0] = gb_ref[0] * dot(o_ref[...], wout_ref[...])


def _dsa(dq, iq, iwb, gb, ikt, dkt, vp, rel_bias, w_out):
    bsz, _, s, _ = dq.shape
    d = w_out.shape[1]
    t = DSA_TILE
    nq = s // t
    n_slots = nq // 2
    topk = min(IDX_TOPK, s // 4)
    heads = pl.BlockSpec((1, DSA_HEADS, t, DSA_HEAD_DIM), lambda b, i: (b, 0, i, 0))
    tok = lambda n: pl.BlockSpec((1, t, n), lambda b, i: (b, i, 0))
    per_batch = lambda shape: pl.BlockSpec((1,) + shape, lambda b, i: (b,) + (0,) * len(shape),
                                           pipeline_mode=pl.Buffered(1))
    row_state = lambda dt: pltpu.VMEM((t, LANES), dt)
    return pl.pallas_call(
        functools.partial(_dsa_kernel, topk=topk),
        grid=(bsz, nq),
        in_specs=[pl.BlockSpec(memory_space=pltpu.SMEM),
                  heads, heads, tok(IDX_HEADS * LANES), tok(d),
                  per_batch((n_slots, IDX_DIM, DSA_SLOT)), per_batch((n_slots, DSA_HEAD_DIM, DSA_SLOT)),
                  per_batch((s, LANES)), _const_spec(w_out.shape)],
        out_specs=tok(d),
        out_shape=jax.ShapeDtypeStruct((bsz, s, d), F32),
        scratch_shapes=[pltpu.VMEM((n_slots, t, DSA_SLOT), I32),
                        row_state(I32), row_state(I32), row_state(I32),
                        row_state(I32), row_state(I32),
                        pltpu.VMEM((n_slots, STRAGGLERS, DSA_SLOT), I32),
                        row_state(F32), row_state(F32),
                        pltpu.VMEM((DSA_HEADS, 2, t, t), F32),
                        pltpu.VMEM((DSA_HEADS, t, LANES), F32),
                        pltpu.VMEM((DSA_HEADS, t, LANES), F32),
                        pltpu.VMEM((t, DSA_HEADS * DSA_HEAD_DIM), BF16)],
        compiler_params=_params(("arbitrary", "arbitrary")),
        name="dsa",
    )(rel_bias.reshape(-1), dq, iq, iwb, gb, ikt, dkt, vp, w_out.astype(BF16))


def _moe_kernel(x_ref, yr_ref, yd_ref, mod_ref, wo_ref, gffn_ref, wr_ref, br_ref, w1_ref, w3_ref,
                w2_ref, gfin_ref, out_ref, gate_ref, hid_ref):
    dot = functools.partial(jnp.dot, preferred_element_type=F32)
    tm = x_ref.shape[1]
    mix = dot((yr_ref[0] + yd_ref[0]).astype(BF16), wo_ref[...])
    h1 = x_ref[0] + mod_ref[0, 2:3, :] * mix
    hn = h1 * lax.rsqrt(jnp.mean(h1 * h1, axis=-1, keepdims=True) + NORM_EPS) * gffn_ref[...]
    u2 = hn * (1.0 + mod_ref[0, 4:5, :]) + mod_ref[0, 3:4, :]

    logits = _dot3(u2, wr_ref[...]) + br_ref[...]
    lane = lax.broadcasted_iota(I32, (tm, LANES), 1)
    big = jnp.int32(LANES)
    rmax = lambda a: jnp.max(a, axis=1, keepdims=True)
    rmin = lambda a: jnp.min(a, axis=1, keepdims=True)
    is_g = lane < N_GROUPS
    gl = jnp.where(is_g, logits, -jnp.inf)
    gmax = rmax(gl)
    gsel = rmin(jnp.where(is_g & (gl == gmax), lane, big))
    gp = 1.0 / jnp.sum(jnp.where(is_g, jnp.exp(gl - gmax), 0.0), axis=1, keepdims=True)
    e_lane = lane - N_GROUPS
    in_grp = (e_lane >= 0) & (e_lane < N_EXPERTS) & ((e_lane // EXPERTS_PER_GROUP) == gsel)
    el = jnp.where(in_grp, logits, -jnp.inf)
    v1 = rmax(el)
    i1 = rmin(jnp.where(in_grp & (el == v1), lane, big))
    el2 = jnp.where(lane == i1, -jnp.inf, el)
    v2 = rmax(el2)
    i2 = rmin(jnp.where(in_grp & (lane != i1) & (el2 == v2), lane, big))
    e2 = jnp.exp(v2 - v1)
    den = 1.0 + e2
    w1 = gp * (1.0 / den)
    w2 = gp * (e2 / den)
    for e in range(N_EXPERTS):
        ge = jnp.where(i1 == e + N_GROUPS, w1, 0.0) + jnp.where(i2 == e + N_GROUPS, w2, 0.0)
        gate_ref[e] = jnp.broadcast_to(ge, (tm, LANES))

    u2b = u2.astype(BF16)
    for e in range(N_EXPERTS):
        a = dot(u2b, w1_ref[e])
        b = dot(u2b, w3_ref[e])
        g = gate_ref[e]
        hid = a * jax.nn.sigmoid(a) * b * jnp.concatenate([g] * (EXPERT_FF // LANES), axis=1)
        hid_ref[:, e * EXPERT_FF:(e + 1) * EXPERT_FF] = hid.astype(BF16)
    y = dot(hid_ref[...], w2_ref[...].reshape(N_EXPERTS * EXPERT_FF, w2_ref.shape[2]))
    h2 = h1 + mod_ref[0, 5:6, :] * y
    out_ref[0] = h2 * lax.rsqrt(jnp.mean(h2 * h2, axis=-1, keepdims=True) + NORM_EPS) * gfin_ref[...]


def _merge_moe(x, y_ret, y_dsa, mod, w_o, g_ffn, w_gr, b_gr, w_er, b_er, w1, w3, w2, g_fin):
    bsz, s, d = x.shape
    tm = MOE_TOKENS
    pad = LANES - N_GROUPS - N_EXPERTS
    wr = jnp.concatenate([w_gr, w_er, jnp.zeros((d, pad), F32)], axis=1)
    br = jnp.concatenate([b_gr, b_er, jnp.zeros((pad,), F32)]).reshape(1, LANES)
    tok = pl.BlockSpec((1, tm, d), lambda b, i: (b, i, 0))
    return pl.pallas_call(
        _moe_kernel,
        grid=(bsz, s // tm),
        in_specs=[tok, tok, tok,
                  pl.BlockSpec((1, 6, d), lambda b, i: (b, 0, 0)),
                  _const_spec(w_o.shape), _const_spec((1, d)), _const_spec(wr.shape),
                  _const_spec((1, LANES)), _const_spec(w1.shape), _const_spec(w3.shape),
                  _const_spec(w2.shape), _const_spec((1, d))],
        out_specs=tok,
        out_shape=jax.ShapeDtypeStruct((bsz, s, d), F32),
        scratch_shapes=[pltpu.VMEM((N_EXPERTS, tm, LANES), F32),
                        pltpu.VMEM((tm, N_EXPERTS * EXPERT_FF), BF16)],
        compiler_params=_params(("arbitrary", "arbitrary")),
        name="merge_moe",
    )(x, y_ret, y_dsa, mod, w_o.astype(BF16), g_ffn.reshape(1, d), wr, br, w1.astype(BF16),
      w3.astype(BF16), w2.astype(BF16), g_fin.reshape(1, d))


def kernel(x, c, w_ada, b_ada, norm_mix_g, w_in, ret_gn_g, dsa_kv_norm_g, w_dsa_kv_up, rel_bias,
           w_ret_out, w_dsa_out, w_gate, b_gate, w_o, norm_ffn_g, w_group_router, b_group_router,
           w_expert_router, b_expert_router, w_exp_gate, w_exp_up, w_exp_down, norm_final_g):
    assert w_ada.shape[0] == 1, "single-layer block"
    bsz, s, d = x.shape
    assert s % DSA_SLOT == 0 and DSA_TILE == PROJ_TOKENS
    mod = _ada(c, w_ada[0], b_ada[0]).reshape(bsz, 6, d)
    (rq, rk, rv, rg, dq, iq, vp, dkt, ikt, iwb, ga, gb) = _proj(
        x, mod, norm_mix_g[0], w_in[0], dsa_kv_norm_g[0], w_dsa_kv_up[0], w_gate[0], b_gate[0])
    y_ret = _retention(rq, rk, rv, rg, ga, ret_gn_g[0], w_ret_out[0])
    y_dsa = _dsa(dq, iq, iwb, gb, ikt, dkt, vp, rel_bias, w_dsa_out[0])
    return _merge_moe(x, y_ret, y_dsa, mod, w_o[0], norm_ffn_g[0], w_group_router[0],
                      b_group_router[0], w_expert_router[0], b_expert_router[0], w_exp_gate[0],
                      w_exp_up[0], w_exp_down[0], norm_final_g)
```

```python
import functools

import jax
import jax.numpy as jnp
from jax import lax
from jax.experimental import pallas as pl
from jax.experimental.pallas import tpu as pltpu

CHUNK = 64
RET_HEADS = 8
RET_QK_DIM = 64
RET_V_DIM = 128
DSA_HEADS = 8
DSA_HEAD_DIM = 64
DSA_KV_LATENT = 128
IDX_HEADS = 8
IDX_DIM = 64
IDX_TOPK = 256
N_BUCKETS = 32
N_GROUPS = 4
EXPERTS_PER_GROUP = 4
N_EXPERTS = N_GROUPS * EXPERTS_PER_GROUP
EXPERT_FF = 256
ROPE_BASE = 10000.0
NORM_EPS = 1e-6
GN_EPS = 1e-5
IN_SPLITS = (512, 512, 1024, 1024, 512, 128, 512, 64, 8)

LANES = 128
VMEM_LIMIT_BYTES = 56 * 1024 * 1024

PROJ_TOKENS = 256
RET_TOKENS = 256
DSA_TILE = 256
DSA_SLOT = 2 * DSA_TILE
SELECT_ROWS = 128
MOE_TOKENS = 256

INTERP_CLIP = 0.02
BISECT_EVERY = 4
MAX_SELECT_STEPS = 128
STRAGGLERS = 8

LOG2E = 1.4426950408889634
NEG_BIG = -1e30
KEY_MIN = -2147483648
KEY_MAX = 2147483647
BUCKET_STEPS = (12, 16, 23, 32, 46, 64, 91)

F32 = jnp.float32
BF16 = jnp.bfloat16
I32 = jnp.int32


def _const_spec(shape):
    nd = len(shape)
    return pl.BlockSpec(shape, lambda *_: (0,) * nd, pipeline_mode=pl.Buffered(1))


def _params(sem):
    return pltpu.CompilerParams(dimension_semantics=sem, vmem_limit_bytes=VMEM_LIMIT_BYTES)


def _split3(a):
    hi = a.astype(BF16)
    lo = (a - hi.astype(F32)).astype(BF16)
    return hi, lo


def _dot3(a, w):
    ah, al = _split3(a)
    wh, wl = _split3(w)
    d = functools.partial(jnp.dot, preferred_element_type=F32)
    return d(ah, wh) + (d(ah, wl) + d(al, wh))


def _ada_kernel(c_ref, w_ref, b_ref, o_ref):
    c = c_ref[...]
    o_ref[...] = _dot3(c * jax.nn.sigmoid(c), w_ref[...]) + b_ref[...]


def _ada(c, w, b):
    bsz, d = c.shape
    n = w.shape[1]
    rows = 8
    cp = jnp.zeros((rows, d), F32).at[:bsz].set(c)
    tn = 1536
    out = pl.pallas_call(
        _ada_kernel,
        grid=(n // tn,),
        in_specs=[pl.BlockSpec((rows, d), lambda j: (0, 0)),
                  pl.BlockSpec((d, tn), lambda j: (0, j)),
                  pl.BlockSpec((1, tn), lambda j: (0, j))],
        out_specs=pl.BlockSpec((rows, tn), lambda j: (0, j)),
        out_shape=jax.ShapeDtypeStruct((rows, n), F32),
        compiler_params=_params(("arbitrary",)),
        name="adaln",
    )(cp, w, b.reshape(1, n))
    return out[:bsz]


def _proj_kernel(x_ref, mod_ref, g_ref, cos_ref, sin_ref, wqk_ref, wvg_ref, wdq_ref, wiq_ref,
                 wkv_ref, wikw_ref, kvg_ref, wup_ref, wgate_ref, bgate_ref,
                 rq_ref, rk_ref, rv_ref, rg_ref, dq_ref, iq_ref, vp_ref, dkt_ref, ikt_ref,
                 iwb_ref, ga_ref, gb_ref):
    dot = functools.partial(jnp.dot, preferred_element_type=F32)
    x = x_ref[0]
    tm = x.shape[0]
    xn = x * lax.rsqrt(jnp.mean(x * x, axis=-1, keepdims=True) + NORM_EPS) * g_ref[...]
    u = xn * (1.0 + mod_ref[0, 1:2, :]) + mod_ref[0, 0:1, :]
    ub = u.astype(BF16)

    lane = lax.broadcasted_iota(I32, (tm, LANES), 1)
    first_half = (lane % RET_QK_DIM) < (RET_QK_DIM // 2)
    cos = cos_ref[...]
    sin = sin_ref[...]
    zqk = dot(ub, wqk_ref[...])
    for which, out_ref, scale in ((0, rq_ref, 1.0), (1, rk_ref, RET_QK_DIM ** -0.5)):
        for pair in range(RET_HEADS // 2):
            c0 = which * RET_HEADS * RET_QK_DIM + pair * LANES
            z = zqk[:, c0:c0 + LANES]
            rot = jnp.where(first_half, pltpu.roll(z, LANES - 32, 1), pltpu.roll(z, 32, 1))
            r = (z * cos + rot * sin) * scale
            out_ref[0, 2 * pair] = r[:, :RET_QK_DIM].astype(BF16)
            out_ref[0, 2 * pair + 1] = r[:, RET_QK_DIM:].astype(BF16)

    zvg = dot(ub, wvg_ref[...])
    nv = RET_HEADS * RET_V_DIM
    rv_ref[0] = zvg[:, :nv].astype(BF16)
    g = zvg[:, nv:]
    rg_ref[0] = g * jax.nn.sigmoid(g)

    zdq = dot(ub, wdq_ref[...]) * (DSA_HEAD_DIM ** -0.5 * LOG2E)
    ziq = dot(ub, wiq_ref[...])
    for h in range(DSA_HEADS):
        dq_ref[0, h] = zdq[:, h * DSA_HEAD_DIM:(h + 1) * DSA_HEAD_DIM].astype(BF16)
        iq_ref[0, h] = ziq[:, h * IDX_DIM:(h + 1) * IDX_DIM].astype(BF16)

    zkv = dot(ub, wkv_ref[...])
    kvn = zkv * lax.rsqrt(jnp.mean(zkv * zkv, axis=-1, keepdims=True) + NORM_EPS) * kvg_ref[...]
    lat = dot(kvn.astype(BF16), wup_ref[...])
    dkt_ref[0, 0] = lat.T[:DSA_HEAD_DIM, :].astype(BF16)
    vp_ref[0] = jnp.where(lane < DSA_HEAD_DIM, pltpu.roll(lat, DSA_HEAD_DIM, 1), 1.0).astype(BF16)

    zik = dot(ub, wikw_ref[...])
    ikt_ref[0, 0] = zik.T[:IDX_DIM, :].astype(BF16)
    idx_scale = (IDX_DIM ** -0.5) * (IDX_HEADS ** -0.5)
    for h in range(IDX_HEADS):
        col = zik[:, IDX_DIM + h:IDX_DIM + h + 1] * idx_scale
        iwb_ref[0, :, h * LANES:(h + 1) * LANES] = jnp.broadcast_to(col, (tm, LANES))

    zg = jax.nn.sigmoid(dot(ub, wgate_ref[...]) + bgate_ref[...])
    d = zg.shape[1] // 2
    ga_ref[0] = zg[:, :d]
    gb_ref[0] = zg[:, d:]


def _proj(x, mod, g_mix, w_in, kv_g, w_up, w_gate, b_gate):
    bsz, s, d = x.shape
    tm = PROJ_TOKENS
    nt = s // tm
    offs = [0]
    for n in IN_SPLITS:
        offs.append(offs[-1] + n)
    seg = lambda i: w_in[:, offs[i]:offs[i + 1]]
    wqk = jnp.concatenate([seg(0), seg(1)], axis=1).astype(BF16)
    wvg = jnp.concatenate([seg(2), seg(3)], axis=1).astype(BF16)
    wdq = seg(4).astype(BF16)
    wkv = seg(5).astype(BF16)
    wiq = seg(6).astype(BF16)
    wikw = jnp.concatenate([seg(7), seg(8), jnp.zeros((d, LANES - IDX_DIM - IDX_HEADS), F32)],
                           axis=1).astype(BF16)

    pos = jnp.arange(s, dtype=jnp.int32)
    freqs = ROPE_BASE ** (-jnp.arange(0, RET_QK_DIM, 2, dtype=F32) / RET_QK_DIM)
    ang = pos.astype(F32)[:, None] * freqs[None, :]
    cos_h = jnp.concatenate([jnp.cos(ang), jnp.cos(ang)], axis=1)
    sin_h = jnp.concatenate([-jnp.sin(ang), jnp.sin(ang)], axis=1)
    cos_t = jnp.concatenate([cos_h, cos_h], axis=1)
    sin_t = jnp.concatenate([sin_h, sin_h], axis=1)

    tok = lambda n: pl.BlockSpec((1, tm, n), lambda b, i: (b, i, 0))
    heads = pl.BlockSpec((1, DSA_HEADS, tm, DSA_HEAD_DIM), lambda b, i: (b, 0, i, 0))
    trans = pl.BlockSpec((1, 1, DSA_HEAD_DIM, tm), lambda b, i: (b, i // 2, 0, i % 2))
    outs = pl.pallas_call(
        _proj_kernel,
        grid=(bsz, nt),
        in_specs=[tok(d),
                  pl.BlockSpec((1, 6, d), lambda b, i: (b, 0, 0)),
                  _const_spec((1, d)),
                  pl.BlockSpec((tm, LANES), lambda b, i: (i, 0)),
                  pl.BlockSpec((tm, LANES), lambda b, i: (i, 0)),
                  _const_spec(wqk.shape), _const_spec(wvg.shape), _const_spec(wdq.shape),
                  _const_spec(wiq.shape), _const_spec(wkv.shape), _const_spec(wikw.shape),
                  _const_spec((1, DSA_KV_LATENT)), _const_spec(w_up.shape),
                  _const_spec(w_gate.shape), _const_spec((1, w_gate.shape[1]))],
        out_specs=[heads, heads, tok(1024), tok(1024), heads, heads, tok(LANES), trans, trans,
                   tok(IDX_HEADS * LANES), tok(d), tok(d)],
        out_shape=[jax.ShapeDtypeStruct((bsz, RET_HEADS, s, RET_QK_DIM), BF16),
                   jax.ShapeDtypeStruct((bsz, RET_HEADS, s, RET_QK_DIM), BF16),
                   jax.ShapeDtypeStruct((bsz, s, RET_HEADS * RET_V_DIM), BF16),
                   jax.ShapeDtypeStruct((bsz, s, RET_HEADS * RET_V_DIM), F32),
                   jax.ShapeDtypeStruct((bsz, DSA_HEADS, s, DSA_HEAD_DIM), BF16),
                   jax.ShapeDtypeStruct((bsz, IDX_HEADS, s, IDX_DIM), BF16),
                   jax.ShapeDtypeStruct((bsz, s, LANES), BF16),
                   jax.ShapeDtypeStruct((bsz, nt // 2, DSA_HEAD_DIM, 2 * tm), BF16),
                   jax.ShapeDtypeStruct((bsz, nt // 2, IDX_DIM, 2 * tm), BF16),
                   jax.ShapeDtypeStruct((bsz, s, IDX_HEADS * LANES), F32),
                   jax.ShapeDtypeStruct((bsz, s, d), F32),
                   jax.ShapeDtypeStruct((bsz, s, d), F32)],
        compiler_params=_params(("arbitrary", "arbitrary")),
        name="in_proj",
    )(x, mod, g_mix.reshape(1, d), cos_t, sin_t, wqk, wvg, wdq, wiq, wkv, wikw,
      kv_g.reshape(1, -1), w_up.astype(BF16), w_gate.astype(BF16), b_gate.reshape(1, -1))
    return outs


def _ret_kernel(rq_ref, rk_ref, rv_ref, rg_ref, ga_ref, dmat_ref, qdec_ref, kdec_ref, cdec_ref,
                gng_ref, wout_ref, y_ref, state_ref, o_ref):
    dot = functools.partial(jnp.dot, preferred_element_type=F32)

    @pl.when(pl.program_id(1) == 0)
    def _():
        state_ref[...] = jnp.zeros_like(state_ref)

    for h in range(RET_HEADS):
        q = rq_ref[0, h]
        k = rk_ref[0, h]
        cols = slice(h * RET_V_DIM, (h + 1) * RET_V_DIM)
        v = rv_ref[0, :, cols]
        st = state_ref[h]
        s = lax.dot_general(q, k, (((1,), (1,)), ((), ())), preferred_element_type=F32)
        o = dot((s * dmat_ref[h]).astype(BF16), v)
        qd = (q.astype(F32) * qdec_ref[h]).astype(BF16)
        o = o + dot(qd, st.astype(BF16))
        kd = (k.astype(F32) * kdec_ref[h]).astype(BF16)
        kv = lax.dot_general(kd, v, (((0,), (0,)), ((), ())), preferred_element_type=F32)
        state_ref[h] = st * cdec_ref[h] + kv
        mu = jnp.mean(o, axis=-1, keepdims=True)
        oc = o - mu
        var = jnp.mean(oc * oc, axis=-1, keepdims=True)
        on = oc * lax.rsqrt(var + GN_EPS) * gng_ref[:, cols]
        o_ref[:, cols] = (rg_ref[0, :, cols] * on).astype(BF16)
    y_ref[0] = ga_ref[0] * dot(o_ref[...], wout_ref[...])


def _retention(rq, rk, rv, rg, ga, gn_g, w_out):
    bsz, s, nv = rv.shape
    d = w_out.shape[1]
    tr = RET_TOKENS
    log_gamma = jnp.log(1.0 - 2.0 ** (-5.0 - jnp.arange(RET_HEADS, dtype=F32)))
    idx = jnp.arange(tr, dtype=F32)
    ch = jnp.arange(tr, dtype=jnp.int32) // CHUNK
    dist = jnp.abs(idx[:, None] - idx[None, :])
    dmat = jnp.where((ch[None, :] <= ch[:, None])[None],
                     jnp.exp(log_gamma[:, None, None] * dist[None]), 0.0)
    qdec = jnp.broadcast_to(jnp.exp(log_gamma[:, None] * (idx + 1.0)[None, :])[:, :, None],
                            (RET_HEADS, tr, RET_QK_DIM))
    kdec = jnp.broadcast_to(jnp.exp(log_gamma[:, None] * (tr - 1.0 - idx)[None, :])[:, :, None],
                            (RET_HEADS, tr, RET_QK_DIM))
    cdec = jnp.broadcast_to(jnp.exp(log_gamma * tr)[:, None, None],
                            (RET_HEADS, RET_QK_DIM, RET_V_DIM))

    heads = pl.BlockSpec((1, RET_HEADS, tr, RET_QK_DIM), lambda b, i: (b, 0, i, 0))
    tok = lambda n: pl.BlockSpec((1, tr, n), lambda b, i: (b, i, 0))
    return pl.pallas_call(
        _ret_kernel,
        grid=(bsz, s // tr),
        in_specs=[heads, heads, tok(nv), tok(nv), tok(d),
                  _const_spec(dmat.shape), _const_spec(qdec.shape), _const_spec(kdec.shape),
                  _const_spec(cdec.shape), _const_spec((1, nv)), _const_spec(w_out.shape)],
        out_specs=tok(d),
        out_shape=jax.ShapeDtypeStruct((bsz, s, d), F32),
        scratch_shapes=[pltpu.VMEM((RET_HEADS, RET_QK_DIM, RET_V_DIM), F32),
                        pltpu.VMEM((tr, nv), BF16)],
        compiler_params=_params(("arbitrary", "arbitrary")),
        name="retention",
    )(rq, rk, rv, rg, ga, dmat, qdec, kdec, cdec, gn_g.reshape(1, nv), w_out.astype(BF16))


def _sortable(bits):
    return jnp.where(bits < 0, bits ^ 0x7FFFFFFF, bits)


def _midpoint(lo, hi):
    return (lo >> 1) + (hi >> 1) + (lo & hi & 1)


def _lanes(a, n):
    return jnp.concatenate([a] * (n // LANES), axis=1)


def _dsa_kernel(relb_ref, dq_ref, iq_ref, iwb_ref, gb_ref, ikt_ref, dkt_ref, vp_ref, wout_ref,
                y_ref, keys_ref, lo_ref, clo_ref, hi_ref, mid_ref, cnt_ref, strag_ref, sa_ref, sb_ref,
                smax_ref, smin_ref, bias_ref, m_ref, acc_ref, o_ref, *, topk):
    t = DSA_TILE
    w2 = DSA_SLOT
    dot = functools.partial(jnp.dot, preferred_element_type=F32)
    i = pl.program_id(1)
    odd = (i % 2) == 1
    last = i // 2
    n_slots = last + 1
    row = lax.broadcasted_iota(I32, (t, t), 0)
    col = lax.broadcasted_iota(I32, (t, t), 1)
    diag_adm = (col // CHUNK) <= (row // CHUNK)
    far = (N_BUCKETS // 2 - 1) * DSA_HEADS

    @pl.when((pl.program_id(0) == 0) & (i == 0))
    def _():
        for which in range(2):
            rel = col - row - (t if which == 0 else 0)
            n = jnp.abs(rel)
            large = jnp.full((t, t), 8, I32)
            for step in BUCKET_STEPS:
                large = large + (n >= step).astype(I32)
            bucket = jnp.where(n < 8, n, large) + jnp.where(rel > 0, N_BUCKETS // 2, 0)
            for h in range(DSA_HEADS):
                b = jnp.zeros((t, t), F32)
                for k in range(N_BUCKETS):
                    b = jnp.where(bucket == k, relb_ref[k * DSA_HEADS + h], b)
                bias_ref[h, which] = (b - relb_ref[far + h]) * LOG2E

    smax_ref[...] = jnp.full((t, LANES), -jnp.inf, F32)
    smin_ref[...] = jnp.full((t, LANES), jnp.inf, F32)

    def head_scores(slot, buf_ref):
        buf_ref[...] = dot(iq_ref[0].reshape(IDX_HEADS * t, IDX_DIM),
                           ikt_ref[0, slot]).reshape(IDX_HEADS, t, w2)

    def score_cols(buf_ref, cols, adm):
        n = cols.stop - cols.start
        acc = jnp.zeros((t, n), F32)
        for h in range(IDX_HEADS):
            acc = acc + (jnp.maximum(buf_ref[h, :, cols], 0.0)
                         * _lanes(iwb_ref[0, :, h * LANES:(h + 1) * LANES], n))
        hi_src = acc if adm is None else jnp.where(adm, acc, -jnp.inf)
        lo_src = acc if adm is None else jnp.where(adm, acc, jnp.inf)
        mx = smax_ref[...]
        mn = smin_ref[...]
        for c0 in range(0, n, LANES):
            mx = jnp.maximum(mx, hi_src[:, c0:c0 + LANES])
            mn = jnp.minimum(mn, lo_src[:, c0:c0 + LANES])
        smax_ref[...] = mx
        smin_ref[...] = mn
        key = _sortable(pltpu.bitcast(acc, I32))
        return key if adm is None else jnp.where(adm, key, KEY_MIN)

    left_cols, right_cols, all_cols = slice(0, t), slice(t, w2), slice(0, w2)

    def score_pair(pair, carry):
        head_scores(2 * pair + 1, sb_ref)
        keys_ref[2 * pair] = score_cols(sa_ref, all_cols, None)
        head_scores(2 * pair + 2, sa_ref)
        keys_ref[2 * pair + 1] = score_cols(sb_ref, all_cols, None)
        return carry

    def score_tail(buf_ref):
        @pl.when(odd)
        def _():
            keys_ref[last, :, :t] = score_cols(buf_ref, left_cols, None)
            keys_ref[last, :, t:] = score_cols(buf_ref, right_cols, diag_adm)

        @pl.when(jnp.logical_not(odd))
        def _():
            keys_ref[last, :, :t] = score_cols(buf_ref, left_cols, diag_adm)
            keys_ref[last, :, t:] = jnp.full((t, t), KEY_MIN, I32)

    head_scores(0, sa_ref)
    lax.fori_loop(0, last // 2, score_pair, 0)

    @pl.when(last % 2 == 0)
    def _():
        score_tail(sa_ref)

    @pl.when(last % 2 == 1)
    def _():
        head_scores(last, sb_ref)
        keys_ref[last - 1] = score_cols(sa_ref, all_cols, None)
        score_tail(sb_ref)

    def spread(a):
        f = pltpu.bitcast(a[0:1, :], F32)
        return pltpu.bitcast(jnp.broadcast_to(f, (LANES, t)).T, I32)

    ones_row = jnp.ones((8, LANES), BF16)
    pos = i * t + lax.broadcasted_iota(I32, (8, t), 1)
    n_adm = (pos // CHUNK + 1) * CHUNK
    rmax = jnp.max(smax_ref[...].T, axis=0, keepdims=True)
    rmin = jnp.min(smin_ref[...].T, axis=0, keepdims=True)
    lo0 = jnp.broadcast_to(_sortable(pltpu.bitcast(rmin, I32)), (8, t))
    hi0 = jnp.broadcast_to(_sortable(pltpu.bitcast(rmax, I32)) + 1, (8, t))
    hi0 = jnp.where(n_adm <= topk, lo0 + 1, hi0)

    def unresolved(lo, hi, clo):
        return (clo != topk) & (_midpoint(lo, hi) > lo)

    def count_unresolved(lo, hi, clo):
        return jnp.sum(jnp.where(unresolved(lo, hi, clo)[0:1, :], 1.0, 0.0))

    def tail_quantile(q):
        q = jnp.clip(q, 1e-7, 1.0 - 1e-7)
        u = jnp.sqrt(-2.0 * jnp.log(jnp.minimum(q, 1.0 - q)))
        x = u - ((0.010328 * u + 0.802853) * u + 2.515517) / (
            ((0.001308 * u + 0.189269) * u + 1.432788) * u + 1.0)
        return jnp.where(q <= 0.5, x, -x)

    inv_n = 1.0 / (n_adm.astype(F32) + 1.0)
    z_k = tail_quantile((topk - 0.5) * inv_n)

    def split_point(step, lo, hi, clo, chi):
        lof = pltpu.bitcast(_sortable(lo), F32)
        hif = pltpu.bitcast(_sortable(hi - 1), F32)
        za = tail_quantile((clo.astype(F32) - 0.5) * inv_n)
        zb = tail_quantile(jnp.maximum(chi.astype(F32), 0.5) * inv_n)
        frac = jnp.clip((z_k - za) / (zb - za), INTERP_CLIP, 1.0 - INTERP_CLIP)
        mid_i = _sortable(pltpu.bitcast(lof + (hif - lof) * frac, I32))
        mid = jnp.where(step % BISECT_EVERY == BISECT_EVERY - 1, _midpoint(lo, hi), mid_i)
        return jnp.minimum(jnp.maximum(mid, lo + 1), hi - 1)

    def select_cond(carry):
        return (carry[0] < MAX_SELECT_STEPS) & (carry[1] > STRAGGLERS + 0.5)

    def select_body(carry):
        step, _, mid, lo, hi, clo, chi = carry
        for s0 in range(0, t, SELECT_ROWS):
            rows = slice(s0, s0 + SELECT_ROWS)
            midr = mid_ref[rows, :]

            def count_body(slot, cnt):
                for c0 in range(0, w2, LANES):
                    cnt = cnt + (keys_ref[slot, rows, c0:c0 + LANES] >= midr).astype(I32)
                return cnt

            cnt_ref[rows, :] = lax.fori_loop(0, n_slots, count_body,
                                             jnp.zeros((SELECT_ROWS, LANES), I32))

        cnt_b = cnt_ref[...].astype(F32).astype(BF16)
        c = lax.dot_general(ones_row, cnt_b, (((1,), (1,)), ((), ())),
                            preferred_element_type=F32).astype(I32)
        active = unresolved(lo, hi, clo)
        up = active & (c >= topk)
        down = active & (c < topk)
        lo = jnp.where(up, mid, lo)
        hi = jnp.where(down, mid, hi)
        clo = jnp.where(up, c, clo)
        chi = jnp.where(down, c, chi)
        mid = split_point(step + 1, lo, hi, clo, chi)
        mid_ref[...] = spread(mid)
        return step + 1, count_unresolved(lo, hi, clo), mid, lo, hi, clo, chi

    chi0 = jnp.zeros((8, t), I32)
    mid0 = split_point(0, lo0, hi0, n_adm, chi0)
    mid_ref[...] = spread(mid0)
    sel = lax.while_loop(select_cond, select_body,
                         (jnp.int32(0), count_unresolved(lo0, hi0, n_adm), mid0, lo0, hi0, n_adm,
                          chi0))
    lo_ref[...] = spread(sel[3])
    hi_ref[...] = spread(sel[4])
    clo_ref[...] = spread(sel[5])

    def pick_rows(mask8):
        lane_f = lax.broadcasted_iota(I32, (8, t), 1).astype(F32)
        picks = []
        for _ in range(STRAGGLERS):
            first = jnp.min(jnp.where(mask8, lane_f, float(t)))
            mask8 = mask8 & (lane_f != first)
            r = first.astype(I32)
            picks.append((jnp.minimum(r, t - 1), r < t))
        return picks, mask8

    def gather(picks):
        def gather_body(slot, carry):
            for j, (r, _) in enumerate(picks):
                strag_ref[slot, j:j + 1, :] = keys_ref[slot, pl.ds(r, 1), :]
            return carry

        lax.fori_loop(0, n_slots, gather_body, 0)

    def rows_of(ref, picks):
        return jnp.concatenate([ref[pl.ds(r, 1), :] for r, _ in picks], axis=0)

    def real_rows(picks):
        return jnp.concatenate([jnp.broadcast_to(ok.astype(I32), (1, LANES)) for _, ok in picks],
                               axis=0) > 0

    def strip_sum(cnt):
        return jnp.broadcast_to(jnp.sum(cnt.astype(F32), axis=1, keepdims=True),
                                (STRAGGLERS, LANES)).astype(I32)

    def straggler_round(left):
        picks, left = pick_rows(left > 0.5)
        gather(picks)
        lo8 = rows_of(lo_ref, picks)
        clo8 = rows_of(clo_ref, picks)
        hi8 = jnp.where(real_rows(picks), rows_of(hi_ref, picks), lo8 + 1)

        def open8(lo, hi, clo):
            return jnp.max(jnp.where(unresolved(lo, hi, clo), 1.0, 0.0))

        def finish_body(carry):
            _, lo, hi, clo = carry
            mid = _midpoint(lo, hi)

            def count_body(slot, c):
                cnt, above, below = c
                for c0 in range(0, w2, LANES):
                    k = strag_ref[slot, :, c0:c0 + LANES]
                    ge = k >= mid
                    cnt = cnt + ge.astype(I32)
                    above = jnp.minimum(above, jnp.where(ge, k, KEY_MAX))
                    below = jnp.maximum(below, jnp.where(ge, KEY_MIN, k))
                return cnt, above, below

            shape = (STRAGGLERS, LANES)
            cnt, above, below = lax.fori_loop(
                0, n_slots, count_body,
                (jnp.zeros(shape, I32), jnp.full(shape, KEY_MAX, I32), jnp.full(shape, KEY_MIN, I32)))
            c = strip_sum(cnt)
            above = jnp.broadcast_to(jnp.min(above, axis=1, keepdims=True), shape)
            below = jnp.broadcast_to(jnp.max(below, axis=1, keepdims=True), shape)
            active = unresolved(lo, hi, clo)
            up = active & (c >= topk)
            lo = jnp.where(up, above, lo)
            hi = jnp.where(active & (c < topk), below + 1, hi)
            clo = jnp.where(up, c, clo)
            return open8(lo, hi, clo), lo, hi, clo

        _, lo8, _, clo8 = lax.while_loop(lambda carry: carry[0] > 0.5, finish_body,
                                         (open8(lo8, hi8, clo8), lo8, hi8, clo8))
        for j, (r, ok) in enumerate(picks):
            @pl.when(ok)
            def _():
                lo_ref[pl.ds(r, 1), :] = lo8[j:j + 1, :]
                clo_ref[pl.ds(r, 1), :] = clo8[j:j + 1, :]
        return jnp.where(left, 1.0, 0.0)

    lax.while_loop(lambda left: jnp.max(left) > 0.5, straggler_round,
                   jnp.where(unresolved(sel[3], sel[4], sel[5]), 1.0, 0.0))

    def count_ties():
        return jnp.max(jnp.where(clo_ref[...] > topk, 1.0, 0.0))

    def tie_round(_):
        tied8 = jnp.where(clo_ref[...] > topk, 1.0, 0.0).T[0:8, :] > 0.5
        picks, _ = pick_rows(tied8)
        gather(picks)
        real = _lanes(real_rows(picks).astype(I32), w2) > 0
        thr = _lanes(rows_of(lo_ref, picks), w2)
        col2 = lax.broadcasted_iota(I32, (STRAGGLERS, w2), 1)

        def lane_fold(x):
            return sum(x[:, c0:c0 + LANES] for c0 in range(0, w2, LANES))

        def gt_body(slot, cnt):
            return cnt + lane_fold((strag_ref[slot] > thr).astype(I32))

        shape = (STRAGGLERS, LANES)
        need = topk - strip_sum(lax.fori_loop(0, n_slots, gt_body, jnp.zeros(shape, I32)))

        def cut_body(_, carry):
            jlo, jhi = carry
            jm = _lanes((jlo + jhi) >> 1, w2)

            def eq_body(slot, cnt):
                hit = (strag_ref[slot] == thr) & ((col2 + slot * w2) < jm)
                return cnt + lane_fold(hit.astype(I32))

            c = strip_sum(lax.fori_loop(0, n_slots, eq_body, jnp.zeros(shape, I32)))
            ok = c >= need
            mid = (jlo + jhi) >> 1
            return jnp.where(ok, jlo, mid), jnp.where(ok, mid, jhi)

        n_steps = (keys_ref.shape[0] * w2).bit_length() + 1
        _, cut = lax.fori_loop(0, n_steps, cut_body,
                               (jnp.zeros(shape, I32), jnp.full(shape, 1, I32) * (n_slots * w2)))
        cut = _lanes(cut, w2)

        def drop_body(slot, carry):
            key = strag_ref[slot]
            drop = real & (key == thr) & ((col2 + slot * w2) >= cut)
            strag_ref[slot] = jnp.where(drop, KEY_MIN, key)
            for j, (r, ok) in enumerate(picks):
                @pl.when(ok)
                def _():
                    keys_ref[slot, pl.ds(r, 1), :] = strag_ref[slot, j:j + 1, :]
            return carry

        lax.fori_loop(0, n_slots, drop_body, 0)
        for j, (r, ok) in enumerate(picks):
            @pl.when(ok)
            def _():
                clo_ref[pl.ds(r, 1), :] = jnp.full((1, LANES), topk, I32)
        return count_ties()

    lax.while_loop(lambda n: n > 0.5, tie_round, count_ties())

    m_ref[...] = jnp.full(m_ref.shape, NEG_BIG, F32)
    acc_ref[...] = jnp.zeros(acc_ref.shape, F32)

    def attend(slot, biases):
        n = len(biases) * t
        nh = DSA_HEADS
        thr = _lanes(lo_ref[...], n)
        maskb = jnp.where(keys_ref[slot, :, :n] >= thr, 0.0, NEG_BIG)
        v = vp_ref[0, pl.ds(pl.multiple_of(slot * w2, w2), n), :]
        q = dq_ref[0].reshape(nh * t, DSA_HEAD_DIM)
        lg = dot(q, dkt_ref[0, slot, :, :n]).reshape(nh, t, n) + maskb[None]
        if any(which is not None for which in biases):
            lg = jnp.concatenate(
                [lg[:, :, k * t:(k + 1) * t] if which is None
                 else lg[:, :, k * t:(k + 1) * t] + bias_ref[:, which]
                 for k, which in enumerate(biases)], axis=2)
        m_old = m_ref[...]
        m_new = jnp.maximum(m_old, jnp.broadcast_to(jnp.max(lg, axis=2, keepdims=True),
                                                    (nh, t, LANES)))
        p = jnp.exp2(lg - jnp.concatenate([m_new] * (n // LANES), axis=2))
        pv = dot(p.astype(BF16).reshape(nh * t, n), v).reshape(nh, t, LANES)
        acc_ref[...] = acc_ref[...] * jnp.exp2(m_old - m_new) + pv
        m_ref[...] = m_new

    def far_body(slot, carry):
        attend(slot, (None, None))
        return carry

    lax.fori_loop(0, jnp.where(odd, last, last - 1), far_body, 0)

    @pl.when(odd)
    def _():
        attend(last, (0, 1))

    @pl.when(jnp.logical_not(odd) & (i >= 2))
    def _():
        attend(last - 1, (None, 0))

    @pl.when(jnp.logical_not(odd))
    def _():
        attend(last, (1,))

    lane = lax.broadcasted_iota(I32, (t, LANES), 1)
    for pair in range(DSA_HEADS // 2):
        a0 = acc_ref[2 * pair]
        a1 = acc_ref[2 * pair + 1]
        even = a0 / pltpu.roll(a0, DSA_HEAD_DIM, 1)
        odd_h = pltpu.roll(a1, DSA_HEAD_DIM, 1) / a1
        o_ref[:, pair * LANES:(pair + 1) * LANES] = jnp.where(lane < DSA_HEAD_DIM, even,
                                                              odd_h).astype(BF16)
    y_ref[0] = gb_ref[0] * dot(o_ref[...], wout_ref[...])


def _dsa(dq, iq, iwb, gb, ikt, dkt, vp, rel_bias, w_out):
    bsz, _, s, _ = dq.shape
    d = w_out.shape[1]
    t = DSA_TILE
    nq = s // t
    n_slots = nq // 2
    topk = min(IDX_TOPK, s // 4)
    heads = pl.BlockSpec((1, DSA_HEADS, t, DSA_HEAD_DIM), lambda b, i: (b, 0, i, 0))
    tok = lambda n: pl.BlockSpec((1, t, n), lambda b, i: (b, i, 0))
    per_batch = lambda shape: pl.BlockSpec((1,) + shape, lambda b, i: (b,) + (0,) * len(shape),
                                           pipeline_mode=pl.Buffered(1))
    row_state = lambda dt: pltpu.VMEM((t, LANES), dt)
    return pl.pallas_call(
        functools.partial(_dsa_kernel, topk=topk),
        grid=(bsz, nq),
        in_specs=[pl.BlockSpec(memory_space=pltpu.SMEM),
                  heads, heads, tok(IDX_HEADS * LANES), tok(d),
                  per_batch((n_slots, IDX_DIM, DSA_SLOT)), per_batch((n_slots, DSA_HEAD_DIM, DSA_SLOT)),
                  per_batch((s, LANES)), _const_spec(w_out.shape)],
        out_specs=tok(d),
        out_shape=jax.ShapeDtypeStruct((bsz, s, d), F32),
        scratch_shapes=[pltpu.VMEM((n_slots, t, DSA_SLOT), I32),
                        row_state(I32), row_state(I32), row_state(I32),
                        row_state(I32), row_state(I32),
                        pltpu.VMEM((n_slots, STRAGGLERS, DSA_SLOT), I32),
                        pltpu.VMEM((IDX_HEADS, t, DSA_SLOT), F32),
                        pltpu.VMEM((IDX_HEADS, t, DSA_SLOT), F32),
                        row_state(F32), row_state(F32),
                        pltpu.VMEM((DSA_HEADS, 2, t, t), F32),
                        pltpu.VMEM((DSA_HEADS, t, LANES), F32),
                        pltpu.VMEM((DSA_HEADS, t, LANES), F32),
                        pltpu.VMEM((t, DSA_HEADS * DSA_HEAD_DIM), BF16)],
        compiler_params=_params(("arbitrary", "arbitrary")),
        name="dsa",
    )(rel_bias.reshape(-1), dq, iq, iwb, gb, ikt, dkt, vp, w_out.astype(BF16))


def _moe_kernel(x_ref, yr_ref, yd_ref, mod_ref, wo_ref, gffn_ref, wr_ref, br_ref, w1_ref, w3_ref,
                w2_ref, gfin_ref, out_ref, gate_ref, hid_ref):
    dot = functools.partial(jnp.dot, preferred_element_type=F32)
    tm = x_ref.shape[1]
    mix = dot((yr_ref[0] + yd_ref[0]).astype(BF16), wo_ref[...])
    h1 = x_ref[0] + mod_ref[0, 2:3, :] * mix
    hn = h1 * lax.rsqrt(jnp.mean(h1 * h1, axis=-1, keepdims=True) + NORM_EPS) * gffn_ref[...]
    u2 = hn * (1.0 + mod_ref[0, 4:5, :]) + mod_ref[0, 3:4, :]

    logits = _dot3(u2, wr_ref[...]) + br_ref[...]
    lane = lax.broadcasted_iota(I32, (tm, LANES), 1)
    big = jnp.int32(LANES)
    rmax = lambda a: jnp.max(a, axis=1, keepdims=True)
    rmin = lambda a: jnp.min(a, axis=1, keepdims=True)
    is_g = lane < N_GROUPS
    gl = jnp.where(is_g, logits, -jnp.inf)
    gmax = rmax(gl)
    gsel = rmin(jnp.where(is_g & (gl == gmax), lane, big))
    gp = 1.0 / jnp.sum(jnp.where(is_g, jnp.exp(gl - gmax), 0.0), axis=1, keepdims=True)
    e_lane = lane - N_GROUPS
    in_grp = (e_lane >= 0) & (e_lane < N_EXPERTS) & ((e_lane // EXPERTS_PER_GROUP) == gsel)
    el = jnp.where(in_grp, logits, -jnp.inf)
    v1 = rmax(el)
    i1 = rmin(jnp.where(in_grp & (el == v1), lane, big))
    el2 = jnp.where(lane == i1, -jnp.inf, el)
    v2 = rmax(el2)
    i2 = rmin(jnp.where(in_grp & (lane != i1) & (el2 == v2), lane, big))
    e2 = jnp.exp(v2 - v1)
    den = 1.0 + e2
    w1 = gp * (1.0 / den)
    w2 = gp * (e2 / den)
    for e in range(N_EXPERTS):
        ge = jnp.where(i1 == e + N_GROUPS, w1, 0.0) + jnp.where(i2 == e + N_GROUPS, w2, 0.0)
        gate_ref[e] = jnp.broadcast_to(ge, (tm, LANES))

    u2b = u2.astype(BF16)
    for e in range(N_EXPERTS):
        a = dot(u2b, w1_ref[e])
        b = dot(u2b, w3_ref[e])
        g = gate_ref[e]
        hid = a * jax.nn.sigmoid(a) * b * jnp.concatenate([g] * (EXPERT_FF // LANES), axis=1)
        hid_ref[:, e * EXPERT_FF:(e + 1) * EXPERT_FF] = hid.astype(BF16)
    y = dot(hid_ref[...], w2_ref[...].reshape(N_EXPERTS * EXPERT_FF, w2_ref.shape[2]))
    h2 = h1 + mod_ref[0, 5:6, :] * y
    out_ref[0] = h2 * lax.rsqrt(jnp.mean(h2 * h2, axis=-1, keepdims=True) + NORM_EPS) * gfin_ref[...]


def _merge_moe(x, y_ret, y_dsa, mod, w_o, g_ffn, w_gr, b_gr, w_er, b_er, w1, w3, w2, g_fin):
    bsz, s, d = x.shape
    tm = MOE_TOKENS
    pad = LANES - N_GROUPS - N_EXPERTS
    wr = jnp.concatenate([w_gr, w_er, jnp.zeros((d, pad), F32)], axis=1)
    br = jnp.concatenate([b_gr, b_er, jnp.zeros((pad,), F32)]).reshape(1, LANES)
    tok = pl.BlockSpec((1, tm, d), lambda b, i: (b, i, 0))
    return pl.pallas_call(
        _moe_kernel,
        grid=(bsz, s // tm),
        in_specs=[tok, tok, tok,
                  pl.BlockSpec((1, 6, d), lambda b, i: (b, 0, 0)),
                  _const_spec(w_o.shape), _const_spec((1, d)), _const_spec(wr.shape),
                  _const_spec((1, LANES)), _const_spec(w1.shape), _const_spec(w3.shape),
                  _const_spec(w2.shape), _const_spec((1, d))],
        out_specs=tok,
        out_shape=jax.ShapeDtypeStruct((bsz, s, d), F32),
        scratch_shapes=[pltpu.VMEM((N_EXPERTS, tm, LANES), F32),
                        pltpu.VMEM((tm, N_EXPERTS * EXPERT_FF), BF16)],
        compiler_params=_params(("arbitrary", "arbitrary")),
        name="merge_moe",
    )(x, y_ret, y_dsa, mod, w_o.astype(BF16), g_ffn.reshape(1, d), wr, br, w1.astype(BF16),
      w3.astype(BF16), w2.astype(BF16), g_fin.reshape(1, d))


def kernel(x, c, w_ada, b_ada, norm_mix_g, w_in, ret_gn_g, dsa_kv_norm_g, w_dsa_kv_up, rel_bias,
           w_ret_out, w_dsa_out, w_gate, b_gate, w_o, norm_ffn_g, w_group_router, b_group_router,
           w_expert_router, b_expert_router, w_exp_gate, w_exp_up, w_exp_down, norm_final_g):
    assert w_ada.shape[0] == 1, "single-layer block"
    bsz, s, d = x.shape
    assert s % DSA_SLOT == 0 and DSA_TILE == PROJ_TOKENS
    mod = _ada(c, w_ada[0], b_ada[0]).reshape(bsz, 6, d)
    (rq, rk, rv, rg, dq, iq, vp, dkt, ikt, iwb, ga, gb) = _proj(
        x, mod, norm_mix_g[0], w_in[0], dsa_kv_norm_g[0], w_dsa_kv_up[0], w_gate[0], b_gate[0])
    y_ret = _retention(rq, rk, rv, rg, ga, ret_gn_g[0], w_ret_out[0])
    y_dsa = _dsa(dq, iq, iwb, gb, ikt, dkt, vp, rel_bias, w_dsa_out[0])
    return _merge_moe(x, y_ret, y_dsa, mod, w_o[0], norm_ffn_g[0], w_group_router[0],
                      b_group_router[0], w_expert_router[0], b_expert_router[0], w_exp_gate[0],
                      w_exp_up[0], w_exp_down[0], norm_final_g)
```

```python
import functools

import jax
import jax.numpy as jnp
from jax import lax
from jax.experimental import pallas as pl
from jax.experimental.pallas import tpu as pltpu

CHUNK = 64
RET_HEADS = 8
RET_QK_DIM = 64
RET_V_DIM = 128
DSA_HEADS = 8
DSA_HEAD_DIM = 64
DSA_KV_LATENT = 128
IDX_HEADS = 8
IDX_DIM = 64
IDX_TOPK = 256
N_BUCKETS = 32
N_GROUPS = 4
EXPERTS_PER_GROUP = 4
N_EXPERTS = N_GROUPS * EXPERTS_PER_GROUP
EXPERT_FF = 256
ROPE_BASE = 10000.0
NORM_EPS = 1e-6
GN_EPS = 1e-5
IN_SPLITS = (512, 512, 1024, 1024, 512, 128, 512, 64, 8)

LANES = 128
VMEM_LIMIT_BYTES = 56 * 1024 * 1024

PROJ_TOKENS = 256
RET_TOKENS = 256
DSA_TILE = 256
DSA_SLOT = 2 * DSA_TILE
SELECT_ROWS = 128
MOE_TOKENS = 256

INTERP_CLIP = 0.02
BISECT_EVERY = 5
MAX_SELECT_STEPS = 128
STRAGGLERS = 8

LOG2E = 1.4426950408889634
NEG_BIG = -1e30
KEY_MIN = -2147483648
KEY_MAX = 2147483647
BUCKET_STEPS = (12, 16, 23, 32, 46, 64, 91)

F32 = jnp.float32
BF16 = jnp.bfloat16
I32 = jnp.int32


def _const_spec(shape):
    nd = len(shape)
    return pl.BlockSpec(shape, lambda *_: (0,) * nd, pipeline_mode=pl.Buffered(1))


def _params(sem):
    return pltpu.CompilerParams(dimension_semantics=sem, vmem_limit_bytes=VMEM_LIMIT_BYTES)


def _split3(a):
    hi = a.astype(BF16)
    lo = (a - hi.astype(F32)).astype(BF16)
    return hi, lo


def _dot3(a, w):
    ah, al = _split3(a)
    wh, wl = _split3(w)
    d = functools.partial(jnp.dot, preferred_element_type=F32)
    return d(ah, wh) + (d(ah, wl) + d(al, wh))


def _ada_kernel(c_ref, w_ref, b_ref, o_ref):
    c = c_ref[...]
    o_ref[...] = _dot3(c * jax.nn.sigmoid(c), w_ref[...]) + b_ref[...]


def _ada(c, w, b):
    bsz, d = c.shape
    n = w.shape[1]
    rows = 8
    cp = jnp.zeros((rows, d), F32).at[:bsz].set(c)
    tn = 1536
    out = pl.pallas_call(
        _ada_kernel,
        grid=(n // tn,),
        in_specs=[pl.BlockSpec((rows, d), lambda j: (0, 0)),
                  pl.BlockSpec((d, tn), lambda j: (0, j)),
                  pl.BlockSpec((1, tn), lambda j: (0, j))],
        out_specs=pl.BlockSpec((rows, tn), lambda j: (0, j)),
        out_shape=jax.ShapeDtypeStruct((rows, n), F32),
        compiler_params=_params(("arbitrary",)),
        name="adaln",
    )(cp, w, b.reshape(1, n))
    return out[:bsz]


def _proj_kernel(x_ref, mod_ref, g_ref, cos_ref, sin_ref, wqk_ref, wvg_ref, wdq_ref, wiq_ref,
                 wkv_ref, wikw_ref, kvg_ref, wup_ref, wgate_ref, bgate_ref,
                 rq_ref, rk_ref, rv_ref, rg_ref, dq_ref, iq_ref, vp_ref, dkt_ref, ikt_ref,
                 iwb_ref, ga_ref, gb_ref):
    dot = functools.partial(jnp.dot, preferred_element_type=F32)
    x = x_ref[0]
    tm = x.shape[0]
    xn = x * lax.rsqrt(jnp.mean(x * x, axis=-1, keepdims=True) + NORM_EPS) * g_ref[...]
    u = xn * (1.0 + mod_ref[0, 1:2, :]) + mod_ref[0, 0:1, :]
    ub = u.astype(BF16)

    lane = lax.broadcasted_iota(I32, (tm, LANES), 1)
    first_half = (lane % RET_QK_DIM) < (RET_QK_DIM // 2)
    cos = cos_ref[...]
    sin = sin_ref[...]
    zqk = dot(ub, wqk_ref[...])
    for which, out_ref, scale in ((0, rq_ref, 1.0), (1, rk_ref, RET_QK_DIM ** -0.5)):
        for pair in range(RET_HEADS // 2):
            c0 = which * RET_HEADS * RET_QK_DIM + pair * LANES
            z = zqk[:, c0:c0 + LANES]
            rot = jnp.where(first_half, pltpu.roll(z, LANES - 32, 1), pltpu.roll(z, 32, 1))
            r = (z * cos + rot * sin) * scale
            out_ref[0, 2 * pair] = r[:, :RET_QK_DIM].astype(BF16)
            out_ref[0, 2 * pair + 1] = r[:, RET_QK_DIM:].astype(BF16)

    zvg = dot(ub, wvg_ref[...])
    nv = RET_HEADS * RET_V_DIM
    rv_ref[0] = zvg[:, :nv].astype(BF16)
    g = zvg[:, nv:]
    rg_ref[0] = g * jax.nn.sigmoid(g)

    zdq = dot(ub, wdq_ref[...]) * (DSA_HEAD_DIM ** -0.5 * LOG2E)
    ziq = dot(ub, wiq_ref[...])
    for h in range(DSA_HEADS):
        dq_ref[0, h] = zdq[:, h * DSA_HEAD_DIM:(h + 1) * DSA_HEAD_DIM].astype(BF16)
        iq_ref[0, h] = ziq[:, h * IDX_DIM:(h + 1) * IDX_DIM].astype(BF16)

    zkv = dot(ub, wkv_ref[...])
    kvn = zkv * lax.rsqrt(jnp.mean(zkv * zkv, axis=-1, keepdims=True) + NORM_EPS) * kvg_ref[...]
    lat = dot(kvn.astype(BF16), wup_ref[...])
    dkt_ref[0, 0] = lat.T[:DSA_HEAD_DIM, :].astype(BF16)
    vp_ref[0] = jnp.where(lane < DSA_HEAD_DIM, pltpu.roll(lat, DSA_HEAD_DIM, 1), 1.0).astype(BF16)

    zik = dot(ub, wikw_ref[...])
    ikt_ref[0, 0] = zik.T[:IDX_DIM, :].astype(BF16)
    idx_scale = (IDX_DIM ** -0.5) * (IDX_HEADS ** -0.5)
    for h in range(IDX_HEADS):
        col = zik[:, IDX_DIM + h:IDX_DIM + h + 1] * idx_scale
        iwb_ref[0, :, h * LANES:(h + 1) * LANES] = jnp.broadcast_to(col, (tm, LANES))

    zg = jax.nn.sigmoid(dot(ub, wgate_ref[...]) + bgate_ref[...])
    d = zg.shape[1] // 2
    ga_ref[0] = zg[:, :d]
    gb_ref[0] = zg[:, d:]


def _proj(x, mod, g_mix, w_in, kv_g, w_up, w_gate, b_gate):
    bsz, s, d = x.shape
    tm = PROJ_TOKENS
    nt = s // tm
    offs = [0]
    for n in IN_SPLITS:
        offs.append(offs[-1] + n)
    seg = lambda i: w_in[:, offs[i]:offs[i + 1]]
    wqk = jnp.concatenate([seg(0), seg(1)], axis=1).astype(BF16)
    wvg = jnp.concatenate([seg(2), seg(3)], axis=1).astype(BF16)
    wdq = seg(4).astype(BF16)
    wkv = seg(5).astype(BF16)
    wiq = seg(6).astype(BF16)
    wikw = jnp.concatenate([seg(7), seg(8), jnp.zeros((d, LANES - IDX_DIM - IDX_HEADS), F32)],
                           axis=1).astype(BF16)

    pos = jnp.arange(s, dtype=jnp.int32)
    freqs = ROPE_BASE ** (-jnp.arange(0, RET_QK_DIM, 2, dtype=F32) / RET_QK_DIM)
    ang = pos.astype(F32)[:, None] * freqs[None, :]
    cos_h = jnp.concatenate([jnp.cos(ang), jnp.cos(ang)], axis=1)
    sin_h = jnp.concatenate([-jnp.sin(ang), jnp.sin(ang)], axis=1)
    cos_t = jnp.concatenate([cos_h, cos_h], axis=1)
    sin_t = jnp.concatenate([sin_h, sin_h], axis=1)

    tok = lambda n: pl.BlockSpec((1, tm, n), lambda b, i: (b, i, 0))
    heads = pl.BlockSpec((1, DSA_HEADS, tm, DSA_HEAD_DIM), lambda b, i: (b, 0, i, 0))
    trans = pl.BlockSpec((1, 1, DSA_HEAD_DIM, tm), lambda b, i: (b, i // 2, 0, i % 2))
    outs = pl.pallas_call(
        _proj_kernel,
        grid=(bsz, nt),
        in_specs=[tok(d),
                  pl.BlockSpec((1, 6, d), lambda b, i: (b, 0, 0)),
                  _const_spec((1, d)),
                  pl.BlockSpec((tm, LANES), lambda b, i: (i, 0)),
                  pl.BlockSpec((tm, LANES), lambda b, i: (i, 0)),
                  _const_spec(wqk.shape), _const_spec(wvg.shape), _const_spec(wdq.shape),
                  _const_spec(wiq.shape), _const_spec(wkv.shape), _const_spec(wikw.shape),
                  _const_spec((1, DSA_KV_LATENT)), _const_spec(w_up.shape),
                  _const_spec(w_gate.shape), _const_spec((1, w_gate.shape[1]))],
        out_specs=[heads, heads, tok(1024), tok(1024), heads, heads, tok(LANES), trans, trans,
                   tok(IDX_HEADS * LANES), tok(d), tok(d)],
        out_shape=[jax.ShapeDtypeStruct((bsz, RET_HEADS, s, RET_QK_DIM), BF16),
                   jax.ShapeDtypeStruct((bsz, RET_HEADS, s, RET_QK_DIM), BF16),
                   jax.ShapeDtypeStruct((bsz, s, RET_HEADS * RET_V_DIM), BF16),
                   jax.ShapeDtypeStruct((bsz, s, RET_HEADS * RET_V_DIM), F32),
                   jax.ShapeDtypeStruct((bsz, DSA_HEADS, s, DSA_HEAD_DIM), BF16),
                   jax.ShapeDtypeStruct((bsz, IDX_HEADS, s, IDX_DIM), BF16),
                   jax.ShapeDtypeStruct((bsz, s, LANES), BF16),
                   jax.ShapeDtypeStruct((bsz, nt // 2, DSA_HEAD_DIM, 2 * tm), BF16),
                   jax.ShapeDtypeStruct((bsz, nt // 2, IDX_DIM, 2 * tm), BF16),
                   jax.ShapeDtypeStruct((bsz, s, IDX_HEADS * LANES), F32),
                   jax.ShapeDtypeStruct((bsz, s, d), F32),
                   jax.ShapeDtypeStruct((bsz, s, d), F32)],
        compiler_params=_params(("arbitrary", "arbitrary")),
        name="in_proj",
    )(x, mod, g_mix.reshape(1, d), cos_t, sin_t, wqk, wvg, wdq, wiq, wkv, wikw,
      kv_g.reshape(1, -1), w_up.astype(BF16), w_gate.astype(BF16), b_gate.reshape(1, -1))
    return outs


def _ret_kernel(rq_ref, rk_ref, rv_ref, rg_ref, ga_ref, dmat_ref, qdec_ref, kdec_ref, cdec_ref,
                gng_ref, wout_ref, y_ref, state_ref, o_ref):
    dot = functools.partial(jnp.dot, preferred_element_type=F32)

    @pl.when(pl.program_id(1) == 0)
    def _():
        state_ref[...] = jnp.zeros_like(state_ref)

    for h in range(RET_HEADS):
        q = rq_ref[0, h]
        k = rk_ref[0, h]
        cols = slice(h * RET_V_DIM, (h + 1) * RET_V_DIM)
        v = rv_ref[0, :, cols]
        st = state_ref[h]
        s = lax.dot_general(q, k, (((1,), (1,)), ((), ())), preferred_element_type=F32)
        o = dot((s * dmat_ref[h]).astype(BF16), v)
        qd = (q.astype(F32) * qdec_ref[h]).astype(BF16)
        o = o + dot(qd, st.astype(BF16))
        kd = (k.astype(F32) * kdec_ref[h]).astype(BF16)
        kv = lax.dot_general(kd, v, (((0,), (0,)), ((), ())), preferred_element_type=F32)
        state_ref[h] = st * cdec_ref[h] + kv
        mu = jnp.mean(o, axis=-1, keepdims=True)
        oc = o - mu
        var = jnp.mean(oc * oc, axis=-1, keepdims=True)
        on = oc * lax.rsqrt(var + GN_EPS) * gng_ref[:, cols]
        o_ref[:, cols] = (rg_ref[0, :, cols] * on).astype(BF16)
    y_ref[0] = ga_ref[0] * dot(o_ref[...], wout_ref[...])


def _retention(rq, rk, rv, rg, ga, gn_g, w_out):
    bsz, s, nv = rv.shape
    d = w_out.shape[1]
    tr = RET_TOKENS
    log_gamma = jnp.log(1.0 - 2.0 ** (-5.0 - jnp.arange(RET_HEADS, dtype=F32)))
    idx = jnp.arange(tr, dtype=F32)
    ch = jnp.arange(tr, dtype=jnp.int32) // CHUNK
    dist = jnp.abs(idx[:, None] - idx[None, :])
    dmat = jnp.where((ch[None, :] <= ch[:, None])[None],
                     jnp.exp(log_gamma[:, None, None] * dist[None]), 0.0)
    qdec = jnp.broadcast_to(jnp.exp(log_gamma[:, None] * (idx + 1.0)[None, :])[:, :, None],
                            (RET_HEADS, tr, RET_QK_DIM))
    kdec = jnp.broadcast_to(jnp.exp(log_gamma[:, None] * (tr - 1.0 - idx)[None, :])[:, :, None],
                            (RET_HEADS, tr, RET_QK_DIM))
    cdec = jnp.broadcast_to(jnp.exp(log_gamma * tr)[:, None, None],
                            (RET_HEADS, RET_QK_DIM, RET_V_DIM))

    heads = pl.BlockSpec((1, RET_HEADS, tr, RET_QK_DIM), lambda b, i: (b, 0, i, 0))
    tok = lambda n: pl.BlockSpec((1, tr, n), lambda b, i: (b, i, 0))
    return pl.pallas_call(
        _ret_kernel,
        grid=(bsz, s // tr),
        in_specs=[heads, heads, tok(nv), tok(nv), tok(d),
                  _const_spec(dmat.shape), _const_spec(qdec.shape), _const_spec(kdec.shape),
                  _const_spec(cdec.shape), _const_spec((1, nv)), _const_spec(w_out.shape)],
        out_specs=tok(d),
        out_shape=jax.ShapeDtypeStruct((bsz, s, d), F32),
        scratch_shapes=[pltpu.VMEM((RET_HEADS, RET_QK_DIM, RET_V_DIM), F32),
                        pltpu.VMEM((tr, nv), BF16)],
        compiler_params=_params(("arbitrary", "arbitrary")),
        name="retention",
    )(rq, rk, rv, rg, ga, dmat, qdec, kdec, cdec, gn_g.reshape(1, nv), w_out.astype(BF16))


def _sortable(bits):
    return jnp.where(bits < 0, bits ^ 0x7FFFFFFF, bits)


def _midpoint(lo, hi):
    return (lo >> 1) + (hi >> 1) + (lo & hi & 1)


def _lanes(a, n):
    return jnp.concatenate([a] * (n // LANES), axis=1)


def _dsa_kernel(relb_ref, dq_ref, iq_ref, iwb_ref, gb_ref, ikt_ref, dkt_ref, vp_ref, wout_ref,
                y_ref, keys_ref, lo_ref, clo_ref, hi_ref, mid_ref, cnt_ref, strag_ref, sa_ref, sb_ref,
                smax_ref, smin_ref, ssum_ref, ssq_ref, bias_ref, m_ref, acc_ref, o_ref, *, topk):
    t = DSA_TILE
    w2 = DSA_SLOT
    dot = functools.partial(jnp.dot, preferred_element_type=F32)
    i = pl.program_id(1)
    odd = (i % 2) == 1
    last = i // 2
    n_slots = last + 1
    row = lax.broadcasted_iota(I32, (t, t), 0)
    col = lax.broadcasted_iota(I32, (t, t), 1)
    diag_adm = (col // CHUNK) <= (row // CHUNK)
    far = (N_BUCKETS // 2 - 1) * DSA_HEADS

    @pl.when((pl.program_id(0) == 0) & (i == 0))
    def _():
        for which in range(2):
            rel = col - row - (t if which == 0 else 0)
            n = jnp.abs(rel)
            large = jnp.full((t, t), 8, I32)
            for step in BUCKET_STEPS:
                large = large + (n >= step).astype(I32)
            bucket = jnp.where(n < 8, n, large) + jnp.where(rel > 0, N_BUCKETS // 2, 0)
            for h in range(DSA_HEADS):
                b = jnp.zeros((t, t), F32)
                for k in range(N_BUCKETS):
                    b = jnp.where(bucket == k, relb_ref[k * DSA_HEADS + h], b)
                bias_ref[h, which] = (b - relb_ref[far + h]) * LOG2E

    smax_ref[...] = jnp.full((t, LANES), -jnp.inf, F32)
    smin_ref[...] = jnp.full((t, LANES), jnp.inf, F32)
    ssum_ref[...] = jnp.zeros((t, LANES), F32)
    ssq_ref[...] = jnp.zeros((t, LANES), F32)

    def head_scores(slot, buf_ref):
        buf_ref[...] = dot(iq_ref[0].reshape(IDX_HEADS * t, IDX_DIM),
                           ikt_ref[0, slot]).reshape(IDX_HEADS, t, w2)

    def score_cols(buf_ref, cols, adm):
        n = cols.stop - cols.start
        acc = jnp.zeros((t, n), F32)
        for h in range(IDX_HEADS):
            acc = acc + (jnp.maximum(buf_ref[h, :, cols], 0.0)
                         * _lanes(iwb_ref[0, :, h * LANES:(h + 1) * LANES], n))
        hi_src = acc if adm is None else jnp.where(adm, acc, -jnp.inf)
        lo_src = acc if adm is None else jnp.where(adm, acc, jnp.inf)
        in_src = acc if adm is None else jnp.where(adm, acc, 0.0)
        mx = smax_ref[...]
        mn = smin_ref[...]
        s1 = ssum_ref[...]
        s2 = ssq_ref[...]
        for c0 in range(0, n, LANES):
            mx = jnp.maximum(mx, hi_src[:, c0:c0 + LANES])
            mn = jnp.minimum(mn, lo_src[:, c0:c0 + LANES])
            part = in_src[:, c0:c0 + LANES]
            s1 = s1 + part
            s2 = s2 + part * part
        smax_ref[...] = mx
        smin_ref[...] = mn
        ssum_ref[...] = s1
        ssq_ref[...] = s2
        key = _sortable(pltpu.bitcast(acc, I32))
        return key if adm is None else jnp.where(adm, key, KEY_MIN)

    left_cols, right_cols, all_cols = slice(0, t), slice(t, w2), slice(0, w2)

    def score_pair(pair, carry):
        head_scores(2 * pair + 1, sb_ref)
        keys_ref[2 * pair] = score_cols(sa_ref, all_cols, None)
        head_scores(2 * pair + 2, sa_ref)
        keys_ref[2 * pair + 1] = score_cols(sb_ref, all_cols, None)
        return carry

    def score_tail(buf_ref):
        @pl.when(odd)
        def _():
            keys_ref[last, :, :t] = score_cols(buf_ref, left_cols, None)
            keys_ref[last, :, t:] = score_cols(buf_ref, right_cols, diag_adm)

        @pl.when(jnp.logical_not(odd))
        def _():
            keys_ref[last, :, :t] = score_cols(buf_ref, left_cols, diag_adm)
            keys_ref[last, :, t:] = jnp.full((t, t), KEY_MIN, I32)

    head_scores(0, sa_ref)
    lax.fori_loop(0, last // 2, score_pair, 0)

    @pl.when(last % 2 == 0)
    def _():
        score_tail(sa_ref)

    @pl.when(last % 2 == 1)
    def _():
        head_scores(last, sb_ref)
        keys_ref[last - 1] = score_cols(sa_ref, all_cols, None)
        score_tail(sb_ref)

    def spread(a):
        f = pltpu.bitcast(a[0:1, :], F32)
        return pltpu.bitcast(jnp.broadcast_to(f, (LANES, t)).T, I32)

    ones_row = jnp.ones((8, LANES), BF16)
    pos = i * t + lax.broadcasted_iota(I32, (8, t), 1)
    n_adm = (pos // CHUNK + 1) * CHUNK
    rmax = jnp.max(smax_ref[...].T, axis=0, keepdims=True)
    rmin = jnp.min(smin_ref[...].T, axis=0, keepdims=True)
    lo0 = jnp.broadcast_to(_sortable(pltpu.bitcast(rmin, I32)), (8, t))
    hi0 = jnp.broadcast_to(_sortable(pltpu.bitcast(rmax, I32)) + 1, (8, t))
    hi0 = jnp.where(n_adm <= topk, lo0 + 1, hi0)

    def unresolved(lo, hi, clo):
        return (clo != topk) & (_midpoint(lo, hi) > lo)

    def count_unresolved(lo, hi, clo):
        return jnp.sum(jnp.where(unresolved(lo, hi, clo)[0:1, :], 1.0, 0.0))

    def tail_quantile(q):
        q = jnp.clip(q, 1e-7, 1.0 - 1e-7)
        u = jnp.sqrt(-2.0 * jnp.log(jnp.minimum(q, 1.0 - q)))
        x = u - ((0.010328 * u + 0.802853) * u + 2.515517) / (
            ((0.001308 * u + 0.189269) * u + 1.432788) * u + 1.0)
        return jnp.where(q <= 0.5, x, -x)

    inv_n = 1.0 / (n_adm.astype(F32) + 1.0)
    z_k = tail_quantile((topk - 0.5) * inv_n)

    def split_point(step, lo, hi, clo, chi):
        lof = pltpu.bitcast(_sortable(lo), F32)
        hif = pltpu.bitcast(_sortable(hi - 1), F32)
        za = tail_quantile((clo.astype(F32) - 0.5) * inv_n)
        zb = tail_quantile(jnp.maximum(chi.astype(F32), 0.5) * inv_n)
        frac = jnp.clip((z_k - za) / (zb - za), INTERP_CLIP, 1.0 - INTERP_CLIP)
        mid_i = _sortable(pltpu.bitcast(lof + (hif - lof) * frac, I32))
        mid = jnp.where(step % BISECT_EVERY == BISECT_EVERY - 1, _midpoint(lo, hi), mid_i)
        return jnp.minimum(jnp.maximum(mid, lo + 1), hi - 1)

    def select_cond(carry):
        return (carry[0] < MAX_SELECT_STEPS) & (carry[1] > STRAGGLERS + 0.5)

    def select_body(carry):
        step, _, mid, lo, hi, clo, chi = carry
        for s0 in range(0, t, SELECT_ROWS):
            rows = slice(s0, s0 + SELECT_ROWS)
            midr = mid_ref[rows, :]

            def count_body(slot, cnt):
                for c0 in range(0, w2, LANES):
                    cnt = cnt + (keys_ref[slot, rows, c0:c0 + LANES] >= midr).astype(I32)
                return cnt

            cnt_ref[rows, :] = lax.fori_loop(0, n_slots, count_body,
                                             jnp.zeros((SELECT_ROWS, LANES), I32))

        cnt_b = cnt_ref[...].astype(F32).astype(BF16)
        c = lax.dot_general(ones_row, cnt_b, (((1,), (1,)), ((), ())),
                            preferred_element_type=F32).astype(I32)
        active = unresolved(lo, hi, clo)
        up = active & (c >= topk)
        down = active & (c < topk)
        lo = jnp.where(up, mid, lo)
        hi = jnp.where(down, mid, hi)
        clo = jnp.where(up, c, clo)
        chi = jnp.where(down, c, chi)
        mid = split_point(step + 1, lo, hi, clo, chi)
        mid_ref[...] = spread(mid)
        return step + 1, count_unresolved(lo, hi, clo), mid, lo, hi, clo, chi

    chi0 = jnp.zeros((8, t), I32)
    n_f = n_adm.astype(F32)
    mean = jnp.sum(ssum_ref[...].T, axis=0, keepdims=True) / n_f
    var = jnp.sum(ssq_ref[...].T, axis=0, keepdims=True) / n_f - mean * mean
    guess = mean + z_k * jnp.sqrt(jnp.maximum(var, 0.0))
    mid0 = jnp.minimum(jnp.maximum(_sortable(pltpu.bitcast(guess, I32)), lo0 + 1), hi0 - 1)
    mid_ref[...] = spread(mid0)
    sel = lax.while_loop(select_cond, select_body,
                         (jnp.int32(0), count_unresolved(lo0, hi0, n_adm), mid0, lo0, hi0, n_adm,
                          chi0))
    lo_ref[...] = spread(sel[3])
    hi_ref[...] = spread(sel[4])
    clo_ref[...] = spread(sel[5])

    def pick_rows(mask8):
        lane_f = lax.broadcasted_iota(I32, (8, t), 1).astype(F32)
        picks = []
        for _ in range(STRAGGLERS):
            first = jnp.min(jnp.where(mask8, lane_f, float(t)))
            mask8 = mask8 & (lane_f != first)
            r = first.astype(I32)
            picks.append((jnp.minimum(r, t - 1), r < t))
        return picks, mask8

    def gather(picks):
        def gather_body(slot, carry):
            for j, (r, _) in enumerate(picks):
                strag_ref[slot, j:j + 1, :] = keys_ref[slot, pl.ds(r, 1), :]
            return carry

        lax.fori_loop(0, n_slots, gather_body, 0)

    def rows_of(ref, picks):
        return jnp.concatenate([ref[pl.ds(r, 1), :] for r, _ in picks], axis=0)

    def real_rows(picks):
        return jnp.concatenate([jnp.broadcast_to(ok.astype(I32), (1, LANES)) for _, ok in picks],
                               axis=0) > 0

    def strip_sum(cnt):
        return jnp.broadcast_to(jnp.sum(cnt.astype(F32), axis=1, keepdims=True),
                                (STRAGGLERS, LANES)).astype(I32)

    def straggler_round(left):
        picks, left = pick_rows(left > 0.5)
        gather(picks)
        lo8 = rows_of(lo_ref, picks)
        clo8 = rows_of(clo_ref, picks)
        hi8 = jnp.where(real_rows(picks), rows_of(hi_ref, picks), lo8 + 1)

        def open8(lo, hi, clo):
            return jnp.max(jnp.where(unresolved(lo, hi, clo), 1.0, 0.0))

        def finish_body(carry):
            _, lo, hi, clo = carry
            mid = _midpoint(lo, hi)

            def count_body(slot, c):
                cnt, above, below = c
                for c0 in range(0, w2, LANES):
                    k = strag_ref[slot, :, c0:c0 + LANES]
                    ge = k >= mid
                    cnt = cnt + ge.astype(I32)
                    above = jnp.minimum(above, jnp.where(ge, k, KEY_MAX))
                    below = jnp.maximum(below, jnp.where(ge, KEY_MIN, k))
                return cnt, above, below

            shape = (STRAGGLERS, LANES)
            cnt, above, below = lax.fori_loop(
                0, n_slots, count_body,
                (jnp.zeros(shape, I32), jnp.full(shape, KEY_MAX, I32), jnp.full(shape, KEY_MIN, I32)))
            c = strip_sum(cnt)
            above = jnp.broadcast_to(jnp.min(above, axis=1, keepdims=True), shape)
            below = jnp.broadcast_to(jnp.max(below, axis=1, keepdims=True), shape)
            active = unresolved(lo, hi, clo)
            up = active & (c >= topk)
            lo = jnp.where(up, above, lo)
            hi = jnp.where(active & (c < topk), below + 1, hi)
            clo = jnp.where(up, c, clo)
            return open8(lo, hi, clo), lo, hi, clo

        _, lo8, _, clo8 = lax.while_loop(lambda carry: carry[0] > 0.5, finish_body,
                                         (open8(lo8, hi8, clo8), lo8, hi8, clo8))
        for j, (r, ok) in enumerate(picks):
            @pl.when(ok)
            def _():
                lo_ref[pl.ds(r, 1), :] = lo8[j:j + 1, :]
                clo_ref[pl.ds(r, 1), :] = clo8[j:j + 1, :]
        return jnp.where(left, 1.0, 0.0)

    lax.while_loop(lambda left: jnp.max(left) > 0.5, straggler_round,
                   jnp.where(unresolved(sel[3], sel[4], sel[5]), 1.0, 0.0))

    def count_ties():
        return jnp.max(jnp.where(clo_ref[...] > topk, 1.0, 0.0))

    def tie_round(_):
        tied8 = jnp.where(clo_ref[...] > topk, 1.0, 0.0).T[0:8, :] > 0.5
        picks, _ = pick_rows(tied8)
        gather(picks)
        real = _lanes(real_rows(picks).astype(I32), w2) > 0
        thr = _lanes(rows_of(lo_ref, picks), w2)
        col2 = lax.broadcasted_iota(I32, (STRAGGLERS, w2), 1)

        def lane_fold(x):
            return sum(x[:, c0:c0 + LANES] for c0 in range(0, w2, LANES))

        def gt_body(slot, cnt):
            return cnt + lane_fold((strag_ref[slot] > thr).astype(I32))

        shape = (STRAGGLERS, LANES)
        need = topk - strip_sum(lax.fori_loop(0, n_slots, gt_body, jnp.zeros(shape, I32)))

        def cut_body(_, carry):
            jlo, jhi = carry
            jm = _lanes((jlo + jhi) >> 1, w2)

            def eq_body(slot, cnt):
                hit = (strag_ref[slot] == thr) & ((col2 + slot * w2) < jm)
                return cnt + lane_fold(hit.astype(I32))

            c = strip_sum(lax.fori_loop(0, n_slots, eq_body, jnp.zeros(shape, I32)))
            ok = c >= need
            mid = (jlo + jhi) >> 1
            return jnp.where(ok, jlo, mid), jnp.where(ok, mid, jhi)

        n_steps = (keys_ref.shape[0] * w2).bit_length() + 1
        _, cut = lax.fori_loop(0, n_steps, cut_body,
                               (jnp.zeros(shape, I32), jnp.full(shape, 1, I32) * (n_slots * w2)))
        cut = _lanes(cut, w2)

        def drop_body(slot, carry):
            key = strag_ref[slot]
            drop = real & (key == thr) & ((col2 + slot * w2) >= cut)
            strag_ref[slot] = jnp.where(drop, KEY_MIN, key)
            for j, (r, ok) in enumerate(picks):
                @pl.when(ok)
                def _():
                    keys_ref[slot, pl.ds(r, 1), :] = strag_ref[slot, j:j + 1, :]
            return carry

        lax.fori_loop(0, n_slots, drop_body, 0)
        for j, (r, ok) in enumerate(picks):
            @pl.when(ok)
            def _():
                clo_ref[pl.ds(r, 1), :] = jnp.full((1, LANES), topk, I32)
        return count_ties()

    lax.while_loop(lambda n: n > 0.5, tie_round, count_ties())

    m_ref[...] = jnp.full(m_ref.shape, NEG_BIG, F32)
    acc_ref[...] = jnp.zeros(acc_ref.shape, F32)

    def attend(slot, biases):
        n = len(biases) * t
        nh = DSA_HEADS
        thr = _lanes(lo_ref[...], n)
        maskb = jnp.where(keys_ref[slot, :, :n] >= thr, 0.0, NEG_BIG)
        v = vp_ref[0, pl.ds(pl.multiple_of(slot * w2, w2), n), :]
        q = dq_ref[0].reshape(nh * t, DSA_HEAD_DIM)
        lg = dot(q, dkt_ref[0, slot, :, :n]).reshape(nh, t, n) + maskb[None]
        if any(which is not None for which in biases):
            lg = jnp.concatenate(
                [lg[:, :, k * t:(k + 1) * t] if which is None
                 else lg[:, :, k * t:(k + 1) * t] + bias_ref[:, which]
                 for k, which in enumerate(biases)], axis=2)
        m_old = m_ref[...]
        m_new = jnp.maximum(m_old, jnp.broadcast_to(jnp.max(lg, axis=2, keepdims=True),
                                                    (nh, t, LANES)))
        p = jnp.exp2(lg - jnp.concatenate([m_new] * (n // LANES), axis=2))
        pv = dot(p.astype(BF16).reshape(nh * t, n), v).reshape(nh, t, LANES)
        acc_ref[...] = acc_ref[...] * jnp.exp2(m_old - m_new) + pv
        m_ref[...] = m_new

    def far_body(slot, carry):
        attend(slot, (None, None))
        return carry

    lax.fori_loop(0, jnp.where(odd, last, last - 1), far_body, 0)

    @pl.when(odd)
    def _():
        attend(last, (0, 1))

    @pl.when(jnp.logical_not(odd) & (i >= 2))
    def _():
        attend(last - 1, (None, 0))

    @pl.when(jnp.logical_not(odd))
    def _():
        attend(last, (1,))

    lane = lax.broadcasted_iota(I32, (t, LANES), 1)
    for pair in range(DSA_HEADS // 2):
        a0 = acc_ref[2 * pair]
        a1 = acc_ref[2 * pair + 1]
        even = a0 / pltpu.roll(a0, DSA_HEAD_DIM, 1)
        odd_h = pltpu.roll(a1, DSA_HEAD_DIM, 1) / a1
        o_ref[:, pair * LANES:(pair + 1) * LANES] = jnp.where(lane < DSA_HEAD_DIM, even,
                                                              odd_h).astype(BF16)
    y_ref[0] = gb_ref[0] * dot(o_ref[...], wout_ref[...])


def _dsa(dq, iq, iwb, gb, ikt, dkt, vp, rel_bias, w_out):
    bsz, _, s, _ = dq.shape
    d = w_out.shape[1]
    t = DSA_TILE
    nq = s // t
    n_slots = nq // 2
    topk = min(IDX_TOPK, s // 4)
    heads = pl.BlockSpec((1, DSA_HEADS, t, DSA_HEAD_DIM), lambda b, i: (b, 0, i, 0))
    tok = lambda n: pl.BlockSpec((1, t, n), lambda b, i: (b, i, 0))
    per_batch = lambda shape: pl.BlockSpec((1,) + shape, lambda b, i: (b,) + (0,) * len(shape),
                                           pipeline_mode=pl.Buffered(1))
    row_state = lambda dt: pltpu.VMEM((t, LANES), dt)
    return pl.pallas_call(
        functools.partial(_dsa_kernel, topk=topk),
        grid=(bsz, nq),
        in_specs=[pl.BlockSpec(memory_space=pltpu.SMEM),
                  heads, heads, tok(IDX_HEADS * LANES), tok(d),
                  per_batch((n_slots, IDX_DIM, DSA_SLOT)), per_batch((n_slots, DSA_HEAD_DIM, DSA_SLOT)),
                  per_batch((s, LANES)), _const_spec(w_out.shape)],
        out_specs=tok(d),
        out_shape=jax.ShapeDtypeStruct((bsz, s, d), F32),
        scratch_shapes=[pltpu.VMEM((n_slots, t, DSA_SLOT), I32),
                        row_state(I32), row_state(I32), row_state(I32),
                        row_state(I32), row_state(I32),
                        pltpu.VMEM((n_slots, STRAGGLERS, DSA_SLOT), I32),
                        pltpu.VMEM((IDX_HEADS, t, DSA_SLOT), F32),
                        pltpu.VMEM((IDX_HEADS, t, DSA_SLOT), F32),
                        row_state(F32), row_state(F32),
                        row_state(F32), row_state(F32),
                        pltpu.VMEM((DSA_HEADS, 2, t, t), F32),
                        pltpu.VMEM((DSA_HEADS, t, LANES), F32),
                        pltpu.VMEM((DSA_HEADS, t, LANES), F32),
                        pltpu.VMEM((t, DSA_HEADS * DSA_HEAD_DIM), BF16)],
        compiler_params=_params(("arbitrary", "arbitrary")),
        name="dsa",
    )(rel_bias.reshape(-1), dq, iq, iwb, gb, ikt, dkt, vp, w_out.astype(BF16))


def _moe_kernel(x_ref, yr_ref, yd_ref, mod_ref, wo_ref, gffn_ref, wr_ref, br_ref, w1_ref, w3_ref,
                w2_ref, gfin_ref, out_ref, gate_ref, hid_ref):
    dot = functools.partial(jnp.dot, preferred_element_type=F32)
    tm = x_ref.shape[1]
    mix = dot((yr_ref[0] + yd_ref[0]).astype(BF16), wo_ref[...])
    h1 = x_ref[0] + mod_ref[0, 2:3, :] * mix
    hn = h1 * lax.rsqrt(jnp.mean(h1 * h1, axis=-1, keepdims=True) + NORM_EPS) * gffn_ref[...]
    u2 = hn * (1.0 + mod_ref[0, 4:5, :]) + mod_ref[0, 3:4, :]

    logits = _dot3(u2, wr_ref[...]) + br_ref[...]
    lane = lax.broadcasted_iota(I32, (tm, LANES), 1)
    big = jnp.int32(LANES)
    rmax = lambda a: jnp.max(a, axis=1, keepdims=True)
    rmin = lambda a: jnp.min(a, axis=1, keepdims=True)
    is_g = lane < N_GROUPS
    gl = jnp.where(is_g, logits, -jnp.inf)
    gmax = rmax(gl)
    gsel = rmin(jnp.where(is_g & (gl == gmax), lane, big))
    gp = 1.0 / jnp.sum(jnp.where(is_g, jnp.exp(gl - gmax), 0.0), axis=1, keepdims=True)
    e_lane = lane - N_GROUPS
    in_grp = (e_lane >= 0) & (e_lane < N_EXPERTS) & ((e_lane // EXPERTS_PER_GROUP) == gsel)
    el = jnp.where(in_grp, logits, -jnp.inf)
    v1 = rmax(el)
    i1 = rmin(jnp.where(in_grp & (el == v1), lane, big))
    el2 = jnp.where(lane == i1, -jnp.inf, el)
    v2 = rmax(el2)
    i2 = rmin(jnp.where(in_grp & (lane != i1) & (el2 == v2), lane, big))
    e2 = jnp.exp(v2 - v1)
    den = 1.0 + e2
    w1 = gp * (1.0 / den)
    w2 = gp * (e2 / den)
    for e in range(N_EXPERTS):
        ge = jnp.where(i1 == e + N_GROUPS, w1, 0.0) + jnp.where(i2 == e + N_GROUPS, w2, 0.0)
        gate_ref[e] = jnp.broadcast_to(ge, (tm, LANES))

    u2b = u2.astype(BF16)
    for e in range(N_EXPERTS):
        a = dot(u2b, w1_ref[e])
        b = dot(u2b, w3_ref[e])
        g = gate_ref[e]
        hid = a * jax.nn.sigmoid(a) * b * jnp.concatenate([g] * (EXPERT_FF // LANES), axis=1)
        hid_ref[:, e * EXPERT_FF:(e + 1) * EXPERT_FF] = hid.astype(BF16)
    y = dot(hid_ref[...], w2_ref[...].reshape(N_EXPERTS * EXPERT_FF, w2_ref.shape[2]))
    h2 = h1 + mod_ref[0, 5:6, :] * y
    out_ref[0] = h2 * lax.rsqrt(jnp.mean(h2 * h2, axis=-1, keepdims=True) + NORM_EPS) * gfin_ref[...]


def _merge_moe(x, y_ret, y_dsa, mod, w_o, g_ffn, w_gr, b_gr, w_er, b_er, w1, w3, w2, g_fin):
    bsz, s, d = x.shape
    tm = MOE_TOKENS
    pad = LANES - N_GROUPS - N_EXPERTS
    wr = jnp.concatenate([w_gr, w_er, jnp.zeros((d, pad), F32)], axis=1)
    br = jnp.concatenate([b_gr, b_er, jnp.zeros((pad,), F32)]).reshape(1, LANES)
    tok = pl.BlockSpec((1, tm, d), lambda b, i: (b, i, 0))
    return pl.pallas_call(
        _moe_kernel,
        grid=(bsz, s // tm),
        in_specs=[tok, tok, tok,
                  pl.BlockSpec((1, 6, d), lambda b, i: (b, 0, 0)),
                  _const_spec(w_o.shape), _const_spec((1, d)), _const_spec(wr.shape),
                  _const_spec((1, LANES)), _const_spec(w1.shape), _const_spec(w3.shape),
                  _const_spec(w2.shape), _const_spec((1, d))],
        out_specs=tok,
        out_shape=jax.ShapeDtypeStruct((bsz, s, d), F32),
        scratch_shapes=[pltpu.VMEM((N_EXPERTS, tm, LANES), F32),
                        pltpu.VMEM((tm, N_EXPERTS * EXPERT_FF), BF16)],
        compiler_params=_params(("arbitrary", "arbitrary")),
        name="merge_moe",
    )(x, y_ret, y_dsa, mod, w_o.astype(BF16), g_ffn.reshape(1, d), wr, br, w1.astype(BF16),
      w3.astype(BF16), w2.astype(BF16), g_fin.reshape(1, d))


def kernel(x, c, w_ada, b_ada, norm_mix_g, w_in, ret_gn_g, dsa_kv_norm_g, w_dsa_kv_up, rel_bias,
           w_ret_out, w_dsa_out, w_gate, b_gate, w_o, norm_ffn_g, w_group_router, b_group_router,
           w_expert_router, b_expert_router, w_exp_gate, w_exp_up, w_exp_down, norm_final_g):
    assert w_ada.shape[0] == 1, "single-layer block"
    bsz, s, d = x.shape
    assert s % DSA_SLOT == 0 and DSA_TILE == PROJ_TOKENS
    mod = _ada(c, w_ada[0], b_ada[0]).reshape(bsz, 6, d)
    (rq, rk, rv, rg, dq, iq, vp, dkt, ikt, iwb, ga, gb) = _proj(
        x, mod, norm_mix_g[0], w_in[0], dsa_kv_norm_g[0], w_dsa_kv_up[0], w_gate[0], b_gate[0])
    y_ret = _retention(rq, rk, rv, rg, ga, ret_gn_g[0], w_ret_out[0])
    y_dsa = _dsa(dq, iq, iwb, gb, ikt, dkt, vp, rel_bias, w_dsa_out[0])
    return _merge_moe(x, y_ret, y_dsa, mod, w_o[0], norm_ffn_g[0], w_group_router[0],
                      b_group_router[0], w_expert_router[0], b_expert_router[0], w_exp_gate[0],
                      w_exp_up[0], w_exp_down[0], norm_final_g)
```

```python
import functools

import jax
import jax.numpy as jnp
from jax import lax
from jax.experimental import pallas as pl
from jax.experimental.pallas import tpu as pltpu

CHUNK = 64
RET_HEADS = 8
RET_QK_DIM = 64
RET_V_DIM = 128
DSA_HEADS = 8
DSA_HEAD_DIM = 64
DSA_KV_LATENT = 128
IDX_HEADS = 8
IDX_DIM = 64
IDX_TOPK = 256
N_BUCKETS = 32
N_GROUPS = 4
EXPERTS_PER_GROUP = 4
N_EXPERTS = N_GROUPS * EXPERTS_PER_GROUP
EXPERT_FF = 256
ROPE_BASE = 10000.0
NORM_EPS = 1e-6
GN_EPS = 1e-5
IN_SPLITS = (512, 512, 1024, 1024, 512, 128, 512, 64, 8)

LANES = 128
VMEM_LIMIT_BYTES = 56 * 1024 * 1024

PROJ_TOKENS = 256
RET_TOKENS = 256
DSA_TILE = 256
DSA_SLOT = 2 * DSA_TILE
SELECT_ROWS = 128
MOE_TOKENS = 256

INTERP_CLIP = 0.02
BISECT_EVERY = 5
MAX_SELECT_STEPS = 128
STRAGGLERS = 8

LOG2E = 1.4426950408889634
NEG_BIG = -1e30
KEY_LOWEST = -2139095040
BUCKET_STEPS = (12, 16, 23, 32, 46, 64, 91)

F32 = jnp.float32
BF16 = jnp.bfloat16
I32 = jnp.int32


def _const_spec(shape):
    nd = len(shape)
    return pl.BlockSpec(shape, lambda *_: (0,) * nd, pipeline_mode=pl.Buffered(1))


def _params(sem):
    return pltpu.CompilerParams(dimension_semantics=sem, vmem_limit_bytes=VMEM_LIMIT_BYTES)


def _split3(a):
    hi = a.astype(BF16)
    lo = (a - hi.astype(F32)).astype(BF16)
    return hi, lo


def _dot3(a, w):
    ah, al = _split3(a)
    wh, wl = _split3(w)
    d = functools.partial(jnp.dot, preferred_element_type=F32)
    return d(ah, wh) + (d(ah, wl) + d(al, wh))


def _ada_kernel(c_ref, w_ref, b_ref, o_ref):
    c = c_ref[...]
    o_ref[...] = _dot3(c * jax.nn.sigmoid(c), w_ref[...]) + b_ref[...]


def _ada(c, w, b):
    bsz, d = c.shape
    n = w.shape[1]
    rows = 8
    cp = jnp.zeros((rows, d), F32).at[:bsz].set(c)
    tn = 1536
    out = pl.pallas_call(
        _ada_kernel,
        grid=(n // tn,),
        in_specs=[pl.BlockSpec((rows, d), lambda j: (0, 0)),
                  pl.BlockSpec((d, tn), lambda j: (0, j)),
                  pl.BlockSpec((1, tn), lambda j: (0, j))],
        out_specs=pl.BlockSpec((rows, tn), lambda j: (0, j)),
        out_shape=jax.ShapeDtypeStruct((rows, n), F32),
        compiler_params=_params(("arbitrary",)),
        name="adaln",
    )(cp, w, b.reshape(1, n))
    return out[:bsz]


def _proj_kernel(x_ref, mod_ref, g_ref, cos_ref, sin_ref, wqk_ref, wvg_ref, wdq_ref, wiq_ref,
                 wkv_ref, wikw_ref, kvg_ref, wup_ref, wgate_ref, bgate_ref,
                 rq_ref, rk_ref, rv_ref, rg_ref, dq_ref, iq_ref, vp_ref, dkt_ref, ikt_ref,
                 iwb_ref, ga_ref, gb_ref):
    dot = functools.partial(jnp.dot, preferred_element_type=F32)
    x = x_ref[0]
    tm = x.shape[0]
    xn = x * lax.rsqrt(jnp.mean(x * x, axis=-1, keepdims=True) + NORM_EPS) * g_ref[...]
    u = xn * (1.0 + mod_ref[0, 1:2, :]) + mod_ref[0, 0:1, :]
    ub = u.astype(BF16)

    lane = lax.broadcasted_iota(I32, (tm, LANES), 1)
    first_half = (lane % RET_QK_DIM) < (RET_QK_DIM // 2)
    cos = cos_ref[...]
    sin = sin_ref[...]
    zqk = dot(ub, wqk_ref[...])
    for which, out_ref, scale in ((0, rq_ref, 1.0), (1, rk_ref, RET_QK_DIM ** -0.5)):
        for pair in range(RET_HEADS // 2):
            c0 = which * RET_HEADS * RET_QK_DIM + pair * LANES
            z = zqk[:, c0:c0 + LANES]
            rot = jnp.where(first_half, pltpu.roll(z, LANES - 32, 1), pltpu.roll(z, 32, 1))
            r = (z * cos + rot * sin) * scale
            out_ref[0, 2 * pair] = r[:, :RET_QK_DIM].astype(BF16)
            out_ref[0, 2 * pair + 1] = r[:, RET_QK_DIM:].astype(BF16)

    zvg = dot(ub, wvg_ref[...])
    nv = RET_HEADS * RET_V_DIM
    rv_ref[0] = zvg[:, :nv].astype(BF16)
    g = zvg[:, nv:]
    rg_ref[0] = g * jax.nn.sigmoid(g)

    zdq = dot(ub, wdq_ref[...]) * (DSA_HEAD_DIM ** -0.5 * LOG2E)
    ziq = dot(ub, wiq_ref[...])
    for h in range(DSA_HEADS):
        dq_ref[0, h] = zdq[:, h * DSA_HEAD_DIM:(h + 1) * DSA_HEAD_DIM].astype(BF16)
        iq_ref[0, h] = ziq[:, h * IDX_DIM:(h + 1) * IDX_DIM].astype(BF16)

    zkv = dot(ub, wkv_ref[...])
    kvn = zkv * lax.rsqrt(jnp.mean(zkv * zkv, axis=-1, keepdims=True) + NORM_EPS) * kvg_ref[...]
    lat = dot(kvn.astype(BF16), wup_ref[...])
    dkt_ref[0, 0] = lat.T[:DSA_HEAD_DIM, :].astype(BF16)
    vp_ref[0] = jnp.where(lane < DSA_HEAD_DIM, pltpu.roll(lat, DSA_HEAD_DIM, 1), 1.0).astype(BF16)

    zik = dot(ub, wikw_ref[...])
    ikt_ref[0, 0] = zik.T[:IDX_DIM, :].astype(BF16)
    idx_scale = (IDX_DIM ** -0.5) * (IDX_HEADS ** -0.5)
    for h in range(IDX_HEADS):
        col = zik[:, IDX_DIM + h:IDX_DIM + h + 1] * idx_scale
        iwb_ref[0, :, h * LANES:(h + 1) * LANES] = jnp.broadcast_to(col, (tm, LANES))

    zg = jax.nn.sigmoid(dot(ub, wgate_ref[...]) + bgate_ref[...])
    d = zg.shape[1] // 2
    ga_ref[0] = zg[:, :d]
    gb_ref[0] = zg[:, d:]


def _proj(x, mod, g_mix, w_in, kv_g, w_up, w_gate, b_gate):
    bsz, s, d = x.shape
    tm = PROJ_TOKENS
    nt = s // tm
    offs = [0]
    for n in IN_SPLITS:
        offs.append(offs[-1] + n)
    seg = lambda i: w_in[:, offs[i]:offs[i + 1]]
    wqk = jnp.concatenate([seg(0), seg(1)], axis=1).astype(BF16)
    wvg = jnp.concatenate([seg(2), seg(3)], axis=1).astype(BF16)
    wdq = seg(4).astype(BF16)
    wkv = seg(5).astype(BF16)
    wiq = seg(6).astype(BF16)
    wikw = jnp.concatenate([seg(7), seg(8), jnp.zeros((d, LANES - IDX_DIM - IDX_HEADS), F32)],
                           axis=1).astype(BF16)

    pos = jnp.arange(s, dtype=jnp.int32)
    freqs = ROPE_BASE ** (-jnp.arange(0, RET_QK_DIM, 2, dtype=F32) / RET_QK_DIM)
    ang = pos.astype(F32)[:, None] * freqs[None, :]
    cos_h = jnp.concatenate([jnp.cos(ang), jnp.cos(ang)], axis=1)
    sin_h = jnp.concatenate([-jnp.sin(ang), jnp.sin(ang)], axis=1)
    cos_t = jnp.concatenate([cos_h, cos_h], axis=1)
    sin_t = jnp.concatenate([sin_h, sin_h], axis=1)

    tok = lambda n: pl.BlockSpec((1, tm, n), lambda b, i: (b, i, 0))
    heads = pl.BlockSpec((1, DSA_HEADS, tm, DSA_HEAD_DIM), lambda b, i: (b, 0, i, 0))
    trans = pl.BlockSpec((1, 1, DSA_HEAD_DIM, tm), lambda b, i: (b, i // 2, 0, i % 2))
    outs = pl.pallas_call(
        _proj_kernel,
        grid=(bsz, nt),
        in_specs=[tok(d),
                  pl.BlockSpec((1, 6, d), lambda b, i: (b, 0, 0)),
                  _const_spec((1, d)),
                  pl.BlockSpec((tm, LANES), lambda b, i: (i, 0)),
                  pl.BlockSpec((tm, LANES), lambda b, i: (i, 0)),
                  _const_spec(wqk.shape), _const_spec(wvg.shape), _const_spec(wdq.shape),
                  _const_spec(wiq.shape), _const_spec(wkv.shape), _const_spec(wikw.shape),
                  _const_spec((1, DSA_KV_LATENT)), _const_spec(w_up.shape),
                  _const_spec(w_gate.shape), _const_spec((1, w_gate.shape[1]))],
        out_specs=[heads, heads, tok(1024), tok(1024), heads, heads, tok(LANES), trans, trans,
                   tok(IDX_HEADS * LANES), tok(d), tok(d)],
        out_shape=[jax.ShapeDtypeStruct((bsz, RET_HEADS, s, RET_QK_DIM), BF16),
                   jax.ShapeDtypeStruct((bsz, RET_HEADS, s, RET_QK_DIM), BF16),
                   jax.ShapeDtypeStruct((bsz, s, RET_HEADS * RET_V_DIM), BF16),
                   jax.ShapeDtypeStruct((bsz, s, RET_HEADS * RET_V_DIM), F32),
                   jax.ShapeDtypeStruct((bsz, DSA_HEADS, s, DSA_HEAD_DIM), BF16),
                   jax.ShapeDtypeStruct((bsz, IDX_HEADS, s, IDX_DIM), BF16),
                   jax.ShapeDtypeStruct((bsz, s, LANES), BF16),
                   jax.ShapeDtypeStruct((bsz, nt // 2, DSA_HEAD_DIM, 2 * tm), BF16),
                   jax.ShapeDtypeStruct((bsz, nt // 2, IDX_DIM, 2 * tm), BF16),
                   jax.ShapeDtypeStruct((bsz, s, IDX_HEADS * LANES), F32),
                   jax.ShapeDtypeStruct((bsz, s, d), F32),
                   jax.ShapeDtypeStruct((bsz, s, d), F32)],
        compiler_params=_params(("arbitrary", "arbitrary")),
        name="in_proj",
    )(x, mod, g_mix.reshape(1, d), cos_t, sin_t, wqk, wvg, wdq, wiq, wkv, wikw,
      kv_g.reshape(1, -1), w_up.astype(BF16), w_gate.astype(BF16), b_gate.reshape(1, -1))
    return outs


def _ret_kernel(rq_ref, rk_ref, rv_ref, rg_ref, ga_ref, dmat_ref, qdec_ref, kdec_ref, cdec_ref,
                gng_ref, wout_ref, y_ref, state_ref, o_ref):
    dot = functools.partial(jnp.dot, preferred_element_type=F32)

    @pl.when(pl.program_id(1) == 0)
    def _():
        state_ref[...] = jnp.zeros_like(state_ref)

    for h in range(RET_HEADS):
        q = rq_ref[0, h]
        k = rk_ref[0, h]
        cols = slice(h * RET_V_DIM, (h + 1) * RET_V_DIM)
        v = rv_ref[0, :, cols]
        st = state_ref[h]
        s = lax.dot_general(q, k, (((1,), (1,)), ((), ())), preferred_element_type=F32)
        o = dot((s * dmat_ref[h]).astype(BF16), v)
        qd = (q.astype(F32) * qdec_ref[h]).astype(BF16)
        o = o + dot(qd, st.astype(BF16))
        kd = (k.astype(F32) * kdec_ref[h]).astype(BF16)
        kv = lax.dot_general(kd, v, (((0,), (0,)), ((), ())), preferred_element_type=F32)
        state_ref[h] = st * cdec_ref[h] + kv
        mu = jnp.mean(o, axis=-1, keepdims=True)
        oc = o - mu
        var = jnp.mean(oc * oc, axis=-1, keepdims=True)
        on = oc * lax.rsqrt(var + GN_EPS) * gng_ref[:, cols]
        o_ref[:, cols] = (rg_ref[0, :, cols] * on).astype(BF16)
    y_ref[0] = ga_ref[0] * dot(o_ref[...], wout_ref[...])


def _retention(rq, rk, rv, rg, ga, gn_g, w_out):
    bsz, s, nv = rv.shape
    d = w_out.shape[1]
    tr = RET_TOKENS
    log_gamma = jnp.log(1.0 - 2.0 ** (-5.0 - jnp.arange(RET_HEADS, dtype=F32)))
    idx = jnp.arange(tr, dtype=F32)
    ch = jnp.arange(tr, dtype=jnp.int32) // CHUNK
    dist = jnp.abs(idx[:, None] - idx[None, :])
    dmat = jnp.where((ch[None, :] <= ch[:, None])[None],
                     jnp.exp(log_gamma[:, None, None] * dist[None]), 0.0)
    qdec = jnp.broadcast_to(jnp.exp(log_gamma[:, None] * (idx + 1.0)[None, :])[:, :, None],
                            (RET_HEADS, tr, RET_QK_DIM))
    kdec = jnp.broadcast_to(jnp.exp(log_gamma[:, None] * (tr - 1.0 - idx)[None, :])[:, :, None],
                            (RET_HEADS, tr, RET_QK_DIM))
    cdec = jnp.broadcast_to(jnp.exp(log_gamma * tr)[:, None, None],
                            (RET_HEADS, RET_QK_DIM, RET_V_DIM))

    heads = pl.BlockSpec((1, RET_HEADS, tr, RET_QK_DIM), lambda b, i: (b, 0, i, 0))
    tok = lambda n: pl.BlockSpec((1, tr, n), lambda b, i: (b, i, 0))
    return pl.pallas_call(
        _ret_kernel,
        grid=(bsz, s // tr),
        in_specs=[heads, heads, tok(nv), tok(nv), tok(d),
                  _const_spec(dmat.shape), _const_spec(qdec.shape), _const_spec(kdec.shape),
                  _const_spec(cdec.shape), _const_spec((1, nv)), _const_spec(w_out.shape)],
        out_specs=tok(d),
        out_shape=jax.ShapeDtypeStruct((bsz, s, d), F32),
        scratch_shapes=[pltpu.VMEM((RET_HEADS, RET_QK_DIM, RET_V_DIM), F32),
                        pltpu.VMEM((tr, nv), BF16)],
        compiler_params=_params(("arbitrary", "arbitrary")),
        name="retention",
    )(rq, rk, rv, rg, ga, dmat, qdec, kdec, cdec, gn_g.reshape(1, nv), w_out.astype(BF16))


def _sortable(bits):
    return jnp.where(bits < 0, bits ^ 0x7FFFFFFF, bits)


def _midpoint(lo, hi):
    return (lo >> 1) + (hi >> 1) + (lo & hi & 1)


def _lanes(a, n):
    return jnp.concatenate([a] * (n // LANES), axis=1)


def _dsa_kernel(relb_ref, dq_ref, iq_ref, iwb_ref, gb_ref, ikt_ref, dkt_ref, vp_ref, wout_ref,
                y_ref, keys_ref, lo_ref, clo_ref, hi_ref, mid_ref, cnt_ref, strag_ref, sa_ref, sb_ref,
                smax_ref, smin_ref, ssum_ref, ssq_ref, bias_ref, m_ref, acc_ref, o_ref, *, topk):
    t = DSA_TILE
    w2 = DSA_SLOT
    dot = functools.partial(jnp.dot, preferred_element_type=F32)
    i = pl.program_id(1)
    odd = (i % 2) == 1
    last = i // 2
    n_slots = last + 1
    row = lax.broadcasted_iota(I32, (t, t), 0)
    col = lax.broadcasted_iota(I32, (t, t), 1)
    diag_adm = (col // CHUNK) <= (row // CHUNK)
    far = (N_BUCKETS // 2 - 1) * DSA_HEADS

    @pl.when((pl.program_id(0) == 0) & (i == 0))
    def _():
        for which in range(2):
            rel = col - row - (t if which == 0 else 0)
            n = jnp.abs(rel)
            large = jnp.full((t, t), 8, I32)
            for step in BUCKET_STEPS:
                large = large + (n >= step).astype(I32)
            bucket = jnp.where(n < 8, n, large) + jnp.where(rel > 0, N_BUCKETS // 2, 0)
            for h in range(DSA_HEADS):
                b = jnp.zeros((t, t), F32)
                for k in range(N_BUCKETS):
                    b = jnp.where(bucket == k, relb_ref[k * DSA_HEADS + h], b)
                bias_ref[h, which] = (b - relb_ref[far + h]) * LOG2E

    smax_ref[...] = jnp.full((t, LANES), -jnp.inf, F32)
    smin_ref[...] = jnp.full((t, LANES), jnp.inf, F32)
    ssum_ref[...] = jnp.zeros((t, LANES), F32)
    ssq_ref[...] = jnp.zeros((t, LANES), F32)

    def head_scores(slot, buf_ref):
        buf_ref[...] = dot(iq_ref[0].reshape(IDX_HEADS * t, IDX_DIM),
                           ikt_ref[0, slot]).reshape(IDX_HEADS, t, w2)

    def score_cols(buf_ref, cols, adm):
        n = cols.stop - cols.start
        acc = jnp.zeros((t, n), F32)
        for h in range(IDX_HEADS):
            acc = acc + (jnp.maximum(buf_ref[h, :, cols], 0.0)
                         * _lanes(iwb_ref[0, :, h * LANES:(h + 1) * LANES], n))
        hi_src = acc if adm is None else jnp.where(adm, acc, -jnp.inf)
        lo_src = acc if adm is None else jnp.where(adm, acc, jnp.inf)
        in_src = acc if adm is None else jnp.where(adm, acc, 0.0)
        mx = smax_ref[...]
        mn = smin_ref[...]
        s1 = ssum_ref[...]
        s2 = ssq_ref[...]
        for c0 in range(0, n, LANES):
            mx = jnp.maximum(mx, hi_src[:, c0:c0 + LANES])
            mn = jnp.minimum(mn, lo_src[:, c0:c0 + LANES])
            part = in_src[:, c0:c0 + LANES]
            s1 = s1 + part
            s2 = s2 + part * part
        smax_ref[...] = mx
        smin_ref[...] = mn
        ssum_ref[...] = s1
        ssq_ref[...] = s2
        return hi_src

    left_cols, right_cols, all_cols = slice(0, t), slice(t, w2), slice(0, w2)

    def score_pair(pair, carry):
        head_scores(2 * pair + 1, sb_ref)
        keys_ref[2 * pair] = score_cols(sa_ref, all_cols, None)
        head_scores(2 * pair + 2, sa_ref)
        keys_ref[2 * pair + 1] = score_cols(sb_ref, all_cols, None)
        return carry

    def score_tail(buf_ref):
        @pl.when(odd)
        def _():
            keys_ref[last, :, :t] = score_cols(buf_ref, left_cols, None)
            keys_ref[last, :, t:] = score_cols(buf_ref, right_cols, diag_adm)

        @pl.when(jnp.logical_not(odd))
        def _():
            keys_ref[last, :, :t] = score_cols(buf_ref, left_cols, diag_adm)
            keys_ref[last, :, t:] = jnp.full((t, t), -jnp.inf, F32)

    head_scores(0, sa_ref)
    lax.fori_loop(0, last // 2, score_pair, 0)

    @pl.when(last % 2 == 0)
    def _():
        score_tail(sa_ref)

    @pl.when(last % 2 == 1)
    def _():
        head_scores(last, sb_ref)
        keys_ref[last - 1] = score_cols(sa_ref, all_cols, None)
        score_tail(sb_ref)

    def as_score(key):
        return pltpu.bitcast(_sortable(key), F32)

    def as_key(score):
        return _sortable(pltpu.bitcast(score, I32))

    def spread(a):
        f = pltpu.bitcast(a[0:1, :], F32)
        return pltpu.bitcast(jnp.broadcast_to(f, (LANES, t)).T, I32)

    ones_row = jnp.ones((8, LANES), BF16)
    pos = i * t + lax.broadcasted_iota(I32, (8, t), 1)
    n_adm = (pos // CHUNK + 1) * CHUNK
    rmax = jnp.max(smax_ref[...].T, axis=0, keepdims=True)
    rmin = jnp.min(smin_ref[...].T, axis=0, keepdims=True)
    keep_all = n_adm <= topk
    lo0 = jnp.where(keep_all, KEY_LOWEST, jnp.broadcast_to(as_key(rmin), (8, t)))
    hi0 = jnp.where(keep_all, lo0 + 1, jnp.broadcast_to(as_key(rmax) + 1, (8, t)))

    def unresolved(lo, hi, clo):
        return (clo != topk) & (_midpoint(lo, hi) > lo)

    def count_unresolved(lo, hi, clo):
        return jnp.sum(jnp.where(unresolved(lo, hi, clo)[0:1, :], 1.0, 0.0))

    def tail_quantile(q):
        q = jnp.clip(q, 1e-7, 1.0 - 1e-7)
        u = jnp.sqrt(-2.0 * jnp.log(jnp.minimum(q, 1.0 - q)))
        x = u - ((0.010328 * u + 0.802853) * u + 2.515517) / (
            ((0.001308 * u + 0.189269) * u + 1.432788) * u + 1.0)
        return jnp.where(q <= 0.5, x, -x)

    inv_n = 1.0 / (n_adm.astype(F32) + 1.0)
    z_k = tail_quantile((topk - 0.5) * inv_n)

    def split_point(step, lo, hi, clo, chi):
        lof = as_score(lo)
        hif = as_score(hi - 1)
        za = tail_quantile((clo.astype(F32) - 0.5) * inv_n)
        zb = tail_quantile(jnp.maximum(chi.astype(F32), 0.5) * inv_n)
        frac = jnp.clip((z_k - za) / (zb - za), INTERP_CLIP, 1.0 - INTERP_CLIP)
        mid_i = as_key(lof + (hif - lof) * frac)
        mid = jnp.where(step % BISECT_EVERY == BISECT_EVERY - 1, _midpoint(lo, hi), mid_i)
        return jnp.minimum(jnp.maximum(mid, lo + 1), hi - 1)

    def select_cond(carry):
        return (carry[0] < MAX_SELECT_STEPS) & (carry[1] > STRAGGLERS + 0.5)

    def select_body(carry):
        step, _, mid, lo, hi, clo, chi = carry
        for s0 in range(0, t, SELECT_ROWS):
            rows = slice(s0, s0 + SELECT_ROWS)
            midr = mid_ref[rows, :]

            def count_body(slot, cnt):
                for c0 in range(0, w2, LANES):
                    cnt = cnt + (keys_ref[slot, rows, c0:c0 + LANES] >= midr).astype(I32)
                return cnt

            cnt_ref[rows, :] = lax.fori_loop(0, n_slots, count_body,
                                             jnp.zeros((SELECT_ROWS, LANES), I32))

        cnt_b = cnt_ref[...].astype(F32).astype(BF16)
        c = lax.dot_general(ones_row, cnt_b, (((1,), (1,)), ((), ())),
                            preferred_element_type=F32).astype(I32)
        active = unresolved(lo, hi, clo)
        up = active & (c >= topk)
        down = active & (c < topk)
        lo = jnp.where(up, mid, lo)
        hi = jnp.where(down, mid, hi)
        clo = jnp.where(up, c, clo)
        chi = jnp.where(down, c, chi)
        mid = split_point(step + 1, lo, hi, clo, chi)
        mid_ref[...] = as_score(spread(mid))
        return step + 1, count_unresolved(lo, hi, clo), mid, lo, hi, clo, chi

    chi0 = jnp.zeros((8, t), I32)
    n_f = n_adm.astype(F32)
    mean = jnp.sum(ssum_ref[...].T, axis=0, keepdims=True) / n_f
    var = jnp.sum(ssq_ref[...].T, axis=0, keepdims=True) / n_f - mean * mean
    guess = mean + z_k * jnp.sqrt(jnp.maximum(var, 0.0))
    mid0 = jnp.minimum(jnp.maximum(as_key(guess), lo0 + 1), hi0 - 1)
    mid_ref[...] = as_score(spread(mid0))
    sel = lax.while_loop(select_cond, select_body,
                         (jnp.int32(0), count_unresolved(lo0, hi0, n_adm), mid0, lo0, hi0, n_adm,
                          chi0))
    lo_ref[...] = as_score(spread(sel[3]))
    hi_ref[...] = spread(sel[4])
    clo_ref[...] = spread(sel[5])

    def pick_rows(mask8):
        lane_i = lax.broadcasted_iota(I32, (8, t), 1)
        before = (lax.broadcasted_iota(I32, (t, t), 0) < lax.broadcasted_iota(I32, (t, t), 1))
        flags = jnp.where(mask8, 1.0, 0.0).astype(BF16)
        rank = dot(flags, jnp.where(before, 1.0, 0.0).astype(BF16)).astype(I32)
        hit = mask8 & (rank == lax.broadcasted_iota(I32, (8, t), 0))
        row = jnp.sum(jnp.where(hit, lane_i, 0).astype(F32), axis=1, keepdims=True).astype(I32)
        found = jnp.sum(jnp.where(hit, 1.0, 0.0), axis=1, keepdims=True)
        picks = [(row[j, 0], found[j, 0] > 0.5) for j in range(STRAGGLERS)]
        return picks, mask8 & (rank >= STRAGGLERS)

    def gather(picks):
        def gather_body(slot, carry):
            for j, (r, _) in enumerate(picks):
                strag_ref[slot, j:j + 1, :] = keys_ref[slot, pl.ds(r, 1), :]
            return carry

        lax.fori_loop(0, n_slots, gather_body, 0)

    def rows_of(ref, picks):
        return jnp.concatenate([ref[pl.ds(r, 1), :] for r, _ in picks], axis=0)

    def real_rows(picks):
        return jnp.concatenate([jnp.broadcast_to(ok.astype(I32), (1, LANES)) for _, ok in picks],
                               axis=0) > 0

    def strip_sum(cnt):
        return jnp.broadcast_to(jnp.sum(cnt.astype(F32), axis=1, keepdims=True),
                                (STRAGGLERS, LANES)).astype(I32)

    def straggler_round(left):
        picks, left = pick_rows(left > 0.5)
        gather(picks)
        thr8 = rows_of(lo_ref, picks)
        lo8 = as_key(thr8)
        clo8 = rows_of(clo_ref, picks)
        hi8 = jnp.where(real_rows(picks), rows_of(hi_ref, picks), lo8 + 1)

        def open8(lo, hi, clo):
            return jnp.max(jnp.where(unresolved(lo, hi, clo), 1.0, 0.0))

        def finish_body(carry):
            _, thr, lo, hi, clo = carry
            mid_key = _midpoint(lo, hi)
            mid = as_score(mid_key)

            def count_body(slot, c):
                cnt, above, below = c
                for c0 in range(0, w2, LANES):
                    k = strag_ref[slot, :, c0:c0 + LANES]
                    ge = k >= mid
                    cnt = cnt + ge.astype(I32)
                    above = jnp.minimum(above, jnp.where(ge, k, jnp.inf))
                    below = jnp.maximum(below, jnp.where(ge, -jnp.inf, k))
                return cnt, above, below

            shape = (STRAGGLERS, LANES)
            cnt, above, below = lax.fori_loop(
                0, n_slots, count_body,
                (jnp.zeros(shape, I32), jnp.full(shape, jnp.inf, F32), jnp.full(shape, -jnp.inf, F32)))
            c = strip_sum(cnt)
            above = jnp.broadcast_to(jnp.min(above, axis=1, keepdims=True), shape)
            below = jnp.broadcast_to(jnp.max(below, axis=1, keepdims=True), shape)
            active = unresolved(lo, hi, clo)
            up = active & (c >= topk)
            thr = jnp.where(up, above, thr)
            lo = jnp.where(up, jnp.maximum(as_key(above), mid_key), lo)
            hi = jnp.where(active & (c < topk), jnp.minimum(as_key(below) + 1, mid_key), hi)
            clo = jnp.where(up, c, clo)
            return open8(lo, hi, clo), thr, lo, hi, clo

        _, thr8, _, _, clo8 = lax.while_loop(lambda carry: carry[0] > 0.5, finish_body,
                                             (open8(lo8, hi8, clo8), thr8, lo8, hi8, clo8))
        for j, (r, ok) in enumerate(picks):
            @pl.when(ok)
            def _():
                lo_ref[pl.ds(r, 1), :] = thr8[j:j + 1, :]
                clo_ref[pl.ds(r, 1), :] = clo8[j:j + 1, :]
        return jnp.where(left, 1.0, 0.0)

    lax.while_loop(lambda left: jnp.max(left) > 0.5, straggler_round,
                   jnp.where(unresolved(sel[3], sel[4], sel[5]), 1.0, 0.0))

    def count_ties():
        return jnp.max(jnp.where(clo_ref[...] > topk, 1.0, 0.0))

    def tie_round(_):
        tied8 = jnp.where(clo_ref[...] > topk, 1.0, 0.0).T[0:8, :] > 0.5
        picks, _ = pick_rows(tied8)
        gather(picks)
        real = _lanes(real_rows(picks).astype(I32), w2) > 0
        thr = _lanes(rows_of(lo_ref, picks), w2)
        col2 = lax.broadcasted_iota(I32, (STRAGGLERS, w2), 1)

        def lane_fold(x):
            return sum(x[:, c0:c0 + LANES] for c0 in range(0, w2, LANES))

        def gt_body(slot, cnt):
            return cnt + lane_fold((strag_ref[slot] > thr).astype(I32))

        shape = (STRAGGLERS, LANES)
        need = topk - strip_sum(lax.fori_loop(0, n_slots, gt_body, jnp.zeros(shape, I32)))

        def cut_body(_, carry):
            jlo, jhi = carry
            jm = _lanes((jlo + jhi) >> 1, w2)

            def eq_body(slot, cnt):
                hit = (strag_ref[slot] == thr) & ((col2 + slot * w2) < jm)
                return cnt + lane_fold(hit.astype(I32))

            c = strip_sum(lax.fori_loop(0, n_slots, eq_body, jnp.zeros(shape, I32)))
            ok = c >= need
            mid = (jlo + jhi) >> 1
            return jnp.where(ok, jlo, mid), jnp.where(ok, mid, jhi)

        n_steps = (keys_ref.shape[0] * w2).bit_length() + 1
        _, cut = lax.fori_loop(0, n_steps, cut_body,
                               (jnp.zeros(shape, I32), jnp.full(shape, 1, I32) * (n_slots * w2)))
        cut = _lanes(cut, w2)

        def drop_body(slot, carry):
            key = strag_ref[slot]
            drop = real & (key == thr) & ((col2 + slot * w2) >= cut)
            strag_ref[slot] = jnp.where(drop, -jnp.inf, key)
            for j, (r, ok) in enumerate(picks):
                @pl.when(ok)
                def _():
                    keys_ref[slot, pl.ds(r, 1), :] = strag_ref[slot, j:j + 1, :]
            return carry

        lax.fori_loop(0, n_slots, drop_body, 0)
        for j, (r, ok) in enumerate(picks):
            @pl.when(ok)
            def _():
                clo_ref[pl.ds(r, 1), :] = jnp.full((1, LANES), topk, I32)
        return count_ties()

    lax.while_loop(lambda n: n > 0.5, tie_round, count_ties())

    m_ref[...] = jnp.full(m_ref.shape, NEG_BIG, F32)
    acc_ref[...] = jnp.zeros(acc_ref.shape, F32)

    def attend(slot, biases):
        n = len(biases) * t
        nh = DSA_HEADS
        thr = _lanes(lo_ref[...], n)
        maskb = jnp.where(keys_ref[slot, :, :n] >= thr, 0.0, NEG_BIG)
        v = vp_ref[0, pl.ds(pl.multiple_of(slot * w2, w2), n), :]
        q = dq_ref[0].reshape(nh * t, DSA_HEAD_DIM)
        lg = dot(q, dkt_ref[0, slot, :, :n]).reshape(nh, t, n) + maskb[None]
        if any(which is not None for which in biases):
            lg = jnp.concatenate(
                [lg[:, :, k * t:(k + 1) * t] if which is None
                 else lg[:, :, k * t:(k + 1) * t] + bias_ref[:, which]
                 for k, which in enumerate(biases)], axis=2)
        m_old = m_ref[...]
        m_new = jnp.maximum(m_old, jnp.broadcast_to(jnp.max(lg, axis=2, keepdims=True),
                                                    (nh, t, LANES)))
        p = jnp.exp2(lg - jnp.concatenate([m_new] * (n // LANES), axis=2))
        pv = dot(p.astype(BF16).reshape(nh * t, n), v).reshape(nh, t, LANES)
        acc_ref[...] = acc_ref[...] * jnp.exp2(m_old - m_new) + pv
        m_ref[...] = m_new

    def far_body(slot, carry):
        attend(slot, (None, None))
        return carry

    lax.fori_loop(0, jnp.where(odd, last, last - 1), far_body, 0)

    @pl.when(odd)
    def _():
        attend(last, (0, 1))

    @pl.when(jnp.logical_not(odd) & (i >= 2))
    def _():
        attend(last - 1, (None, 0))

    @pl.when(jnp.logical_not(odd))
    def _():
        attend(last, (1,))

    lane = lax.broadcasted_iota(I32, (t, LANES), 1)
    for pair in range(DSA_HEADS // 2):
        a0 = acc_ref[2 * pair]
        a1 = acc_ref[2 * pair + 1]
        even = a0 / pltpu.roll(a0, DSA_HEAD_DIM, 1)
        odd_h = pltpu.roll(a1, DSA_HEAD_DIM, 1) / a1
        o_ref[:, pair * LANES:(pair + 1) * LANES] = jnp.where(lane < DSA_HEAD_DIM, even,
                                                              odd_h).astype(BF16)
    y_ref[0] = gb_ref[0] * dot(o_ref[...], wout_ref[...])


def _dsa(dq, iq, iwb, gb, ikt, dkt, vp, rel_bias, w_out):
    bsz, _, s, _ = dq.shape
    d = w_out.shape[1]
    t = DSA_TILE
    nq = s // t
    n_slots = nq // 2
    topk = min(IDX_TOPK, s // 4)
    heads = pl.BlockSpec((1, DSA_HEADS, t, DSA_HEAD_DIM), lambda b, i: (b, 0, i, 0))
    tok = lambda n: pl.BlockSpec((1, t, n), lambda b, i: (b, i, 0))
    per_batch = lambda shape: pl.BlockSpec((1,) + shape, lambda b, i: (b,) + (0,) * len(shape),
                                           pipeline_mode=pl.Buffered(1))
    row_state = lambda dt: pltpu.VMEM((t, LANES), dt)
    return pl.pallas_call(
        functools.partial(_dsa_kernel, topk=topk),
        grid=(bsz, nq),
        in_specs=[pl.BlockSpec(memory_space=pltpu.SMEM),
                  heads, heads, tok(IDX_HEADS * LANES), tok(d),
                  per_batch((n_slots, IDX_DIM, DSA_SLOT)), per_batch((n_slots, DSA_HEAD_DIM, DSA_SLOT)),
                  per_batch((s, LANES)), _const_spec(w_out.shape)],
        out_specs=tok(d),
        out_shape=jax.ShapeDtypeStruct((bsz, s, d), F32),
        scratch_shapes=[pltpu.VMEM((n_slots, t, DSA_SLOT), F32),
                        row_state(F32), row_state(I32), row_state(I32),
                        row_state(F32), row_state(I32),
                        pltpu.VMEM((n_slots, STRAGGLERS, DSA_SLOT), F32),
                        pltpu.VMEM((IDX_HEADS, t, DSA_SLOT), F32),
                        pltpu.VMEM((IDX_HEADS, t, DSA_SLOT), F32),
                        row_state(F32), row_state(F32),
                        row_state(F32), row_state(F32),
                        pltpu.VMEM((DSA_HEADS, 2, t, t), F32),
                        pltpu.VMEM((DSA_HEADS, t, LANES), F32),
                        pltpu.VMEM((DSA_HEADS, t, LANES), F32),
                        pltpu.VMEM((t, DSA_HEADS * DSA_HEAD_DIM), BF16)],
        compiler_params=_params(("arbitrary", "arbitrary")),
        name="dsa",
    )(rel_bias.reshape(-1), dq, iq, iwb, gb, ikt, dkt, vp, w_out.astype(BF16))


def _moe_kernel(x_ref, yr_ref, yd_ref, mod_ref, wo_ref, gffn_ref, wr_ref, br_ref, w1_ref, w3_ref,
                w2_ref, gfin_ref, out_ref, gate_ref, hid_ref):
    dot = functools.partial(jnp.dot, preferred_element_type=F32)
    tm = x_ref.shape[1]
    mix = dot((yr_ref[0] + yd_ref[0]).astype(BF16), wo_ref[...])
    h1 = x_ref[0] + mod_ref[0, 2:3, :] * mix
    hn = h1 * lax.rsqrt(jnp.mean(h1 * h1, axis=-1, keepdims=True) + NORM_EPS) * gffn_ref[...]
    u2 = hn * (1.0 + mod_ref[0, 4:5, :]) + mod_ref[0, 3:4, :]

    logits = _dot3(u2, wr_ref[...]) + br_ref[...]
    lane = lax.broadcasted_iota(I32, (tm, LANES), 1)
    big = jnp.int32(LANES)
    rmax = lambda a: jnp.max(a, axis=1, keepdims=True)
    rmin = lambda a: jnp.min(a, axis=1, keepdims=True)
    is_g = lane < N_GROUPS
    gl = jnp.where(is_g, logits, -jnp.inf)
    gmax = rmax(gl)
    gsel = rmin(jnp.where(is_g & (gl == gmax), lane, big))
    gp = 1.0 / jnp.sum(jnp.where(is_g, jnp.exp(gl - gmax), 0.0), axis=1, keepdims=True)
    e_lane = lane - N_GROUPS
    in_grp = (e_lane >= 0) & (e_lane < N_EXPERTS) & ((e_lane // EXPERTS_PER_GROUP) == gsel)
    el = jnp.where(in_grp, logits, -jnp.inf)
    v1 = rmax(el)
    i1 = rmin(jnp.where(in_grp & (el == v1), lane, big))
    el2 = jnp.where(lane == i1, -jnp.inf, el)
    v2 = rmax(el2)
    i2 = rmin(jnp.where(in_grp & (lane != i1) & (el2 == v2), lane, big))
    e2 = jnp.exp(v2 - v1)
    den = 1.0 + e2
    w1 = gp * (1.0 / den)
    w2 = gp * (e2 / den)
    for e in range(N_EXPERTS):
        ge = jnp.where(i1 == e + N_GROUPS, w1, 0.0) + jnp.where(i2 == e + N_GROUPS, w2, 0.0)
        gate_ref[e] = jnp.broadcast_to(ge, (tm, LANES))

    u2b = u2.astype(BF16)
    for e in range(N_EXPERTS):
        a = dot(u2b, w1_ref[e])
        b = dot(u2b, w3_ref[e])
        g = gate_ref[e]
        hid = a * jax.nn.sigmoid(a) * b * jnp.concatenate([g] * (EXPERT_FF // LANES), axis=1)
        hid_ref[:, e * EXPERT_FF:(e + 1) * EXPERT_FF] = hid.astype(BF16)
    y = dot(hid_ref[...], w2_ref[...].reshape(N_EXPERTS * EXPERT_FF, w2_ref.shape[2]))
    h2 = h1 + mod_ref[0, 5:6, :] * y
    out_ref[0] = h2 * lax.rsqrt(jnp.mean(h2 * h2, axis=-1, keepdims=True) + NORM_EPS) * gfin_ref[...]


def _merge_moe(x, y_ret, y_dsa, mod, w_o, g_ffn, w_gr, b_gr, w_er, b_er, w1, w3, w2, g_fin):
    bsz, s, d = x.shape
    tm = MOE_TOKENS
    pad = LANES - N_GROUPS - N_EXPERTS
    wr = jnp.concatenate([w_gr, w_er, jnp.zeros((d, pad), F32)], axis=1)
    br = jnp.concatenate([b_gr, b_er, jnp.zeros((pad,), F32)]).reshape(1, LANES)
    tok = pl.BlockSpec((1, tm, d), lambda b, i: (b, i, 0))
    return pl.pallas_call(
        _moe_kernel,
        grid=(bsz, s // tm),
        in_specs=[tok, tok, tok,
                  pl.BlockSpec((1, 6, d), lambda b, i: (b, 0, 0)),
                  _const_spec(w_o.shape), _const_spec((1, d)), _const_spec(wr.shape),
                  _const_spec((1, LANES)), _const_spec(w1.shape), _const_spec(w3.shape),
                  _const_spec(w2.shape), _const_spec((1, d))],
        out_specs=tok,
        out_shape=jax.ShapeDtypeStruct((bsz, s, d), F32),
        scratch_shapes=[pltpu.VMEM((N_EXPERTS, tm, LANES), F32),
                        pltpu.VMEM((tm, N_EXPERTS * EXPERT_FF), BF16)],
        compiler_params=_params(("arbitrary", "arbitrary")),
        name="merge_moe",
    )(x, y_ret, y_dsa, mod, w_o.astype(BF16), g_ffn.reshape(1, d), wr, br, w1.astype(BF16),
      w3.astype(BF16), w2.astype(BF16), g_fin.reshape(1, d))


def kernel(x, c, w_ada, b_ada, norm_mix_g, w_in, ret_gn_g, dsa_kv_norm_g, w_dsa_kv_up, rel_bias,
           w_ret_out, w_dsa_out, w_gate, b_gate, w_o, norm_ffn_g, w_group_router, b_group_router,
           w_expert_router, b_expert_router, w_exp_gate, w_exp_up, w_exp_down, norm_final_g):
    assert w_ada.shape[0] == 1, "single-layer block"
    bsz, s, d = x.shape
    assert s % DSA_SLOT == 0 and DSA_TILE == PROJ_TOKENS
    mod = _ada(c, w_ada[0], b_ada[0]).reshape(bsz, 6, d)
    (rq, rk, rv, rg, dq, iq, vp, dkt, ikt, iwb, ga, gb) = _proj(
        x, mod, norm_mix_g[0], w_in[0], dsa_kv_norm_g[0], w_dsa_kv_up[0], w_gate[0], b_gate[0])
    y_ret = _retention(rq, rk, rv, rg, ga, ret_gn_g[0], w_ret_out[0])
    y_dsa = _dsa(dq, iq, iwb, gb, ikt, dkt, vp, rel_bias, w_dsa_out[0])
    return _merge_moe(x, y_ret, y_dsa, mod, w_o[0], norm_ffn_g[0], w_group_router[0],
                      b_group_router[0], w_expert_router[0], b_expert_router[0], w_exp_gate[0],
                      w_exp_up[0], w_exp_down[0], norm_final_g)
```
